```python
import jax, jax.numpy as jnp
from jax import lax
import numpy as np

D_MODEL = 1024
BATCH = 8
SEQ = 2048
DEPTH = 4

GRID_W = 64
CTX_LEN = 256
N_MIXERS = 3
HEAD_DIM = 64
N_Q_HEADS = D_MODEL // HEAD_DIM
N_KV_HEADS = N_Q_HEADS // 4
Q_BLOCK = 128
ROPE_THETA = 10000.0
RET_HEAD_DIM = 256
RET_HEADS = D_MODEL // RET_HEAD_DIM
RET_V_DIM = 2 * RET_HEAD_DIM
RET_CHUNK = 128
RET_DECAY_BASE = 5.0
D_FF = ((8 * D_MODEL // 3 + 127) // 128) * 128
NORM_EPS = 1e-6
N_CONV_LAYERS = len(range(0, DEPTH, N_MIXERS))
N_ATTN_LAYERS = len(range(1, DEPTH, N_MIXERS))
N_RET_LAYERS = len(range(2, DEPTH, N_MIXERS))

kernel_name = "hybrid_shortconv_gqa_retention_dit"


def rmsnorm(x, g=None):
    xf = x.astype(jnp.float32)
    y = xf * lax.rsqrt(jnp.mean(xf * xf, axis=-1, keepdims=True) + NORM_EPS)
    if g is not None:
        y = y * g.astype(jnp.float32)
    return y.astype(x.dtype)


def modulate(x, g, shift, scale):
    return rmsnorm(x, g) * (1 + scale) + shift


def dwconv3(x, w):
    xp = jnp.pad(x, ((0, 0), (1, 1), (0, 0)))
    return xp[:, :-2] * w[0] + xp[:, 1:-1] * w[1] + xp[:, 2:] * w[2]


def axial_angles(rows, cols, head_dim):
    quarter = head_dim // 4
    inv = ROPE_THETA ** (-jnp.arange(quarter, dtype=jnp.float32) / quarter)
    ang = jnp.stack([rows[:, None] * inv, cols[:, None] * inv], axis=1)
    return jnp.cos(ang), jnp.sin(ang)


def apply_axial_rope(x, cos, sin):
    B, L, H, d = x.shape
    xr = x.reshape(B, L, H, 2, 2, d // 4)
    c = cos[None, :, None].astype(x.dtype)
    s = sin[None, :, None].astype(x.dtype)
    x1 = xr[..., 0, :]
    x2 = xr[..., 1, :]
    out = jnp.stack([x1 * c - x2 * s, x1 * s + x2 * c], axis=-2)
    return out.reshape(B, L, H, d)


def short_conv_mixer(h, w_in, conv_k, w_out):
    b_gate, c_gate, v = jnp.split(h @ w_in, 3, axis=-1)
    return (b_gate * dwconv3(c_gate * v, conv_k)) @ w_out


def gqa_project(t, w_qkv, q_g, k_g, with_q):
    B, L, _ = t.shape
    nq = N_Q_HEADS * HEAD_DIM
    nkv = N_KV_HEADS * HEAD_DIM
    if with_q:
        q, k, v = jnp.split(t @ w_qkv, [nq, nq + nkv], axis=-1)
        q = rmsnorm(q.reshape(B, L, N_Q_HEADS, HEAD_DIM), q_g)
    else:
        k, v = jnp.split(t @ w_qkv[:, nq:], 2, axis=-1)
        q = None
    k = rmsnorm(k.reshape(B, L, N_KV_HEADS, HEAD_DIM), k_g)
    v = v.reshape(B, L, N_KV_HEADS, HEAD_DIM)
    return q, k, v


def sdpa_grouped(q, k, v):
    s = jnp.einsum('bqhgd,bkhd->bhgqk', q, k).astype(jnp.float32) * (q.shape[-1] ** -0.5)
    p = jax.nn.softmax(s, axis=-1).astype(v.dtype)
    return jnp.einsum('bhgqk,bkhd->bqhgd', p, v)


def gqa_mixer(h, hc, w_qkv, q_g, k_g, w_out, cos, sin, ctx_out):
    B, S, _ = h.shape
    G = N_Q_HEADS // N_KV_HEADS
    q, k, v = gqa_project(h, w_qkv, q_g, k_g, True)
    q = apply_axial_rope(q, cos, sin)
    k = apply_axial_rope(k, cos, sin)
    qc, kc, vc = gqa_project(hc, w_qkv, q_g, k_g, ctx_out)
    keys = jnp.concatenate([k, kc], axis=1)
    vals = jnp.concatenate([v, vc], axis=1)
    qb = q.reshape(B, S // Q_BLOCK, Q_BLOCK, N_KV_HEADS, G, HEAD_DIM).swapaxes(0, 1)
    ob = lax.map(lambda q_blk: sdpa_grouped(q_blk, keys, vals), qb)
    y = ob.swapaxes(0, 1).reshape(B, S, D_MODEL) @ w_out
    yc = None
    if ctx_out:
        Lc = hc.shape[1]
        oc = sdpa_grouped(qc.reshape(B, Lc, N_KV_HEADS, G, HEAD_DIM), kc, vc)
        yc = oc.reshape(B, Lc, D_MODEL) @ w_out
    return y, yc


def ret_project(t, w_in, with_qg):
    B, L, _ = t.shape
    nqk = RET_HEADS * RET_HEAD_DIM
    nv = RET_HEADS * RET_V_DIM
    if with_qg:
        q, k, v, g = jnp.split(t @ w_in, [nqk, 2 * nqk, 2 * nqk + nv], axis=-1)
        q = q.reshape(B, L, RET_HEADS, RET_HEAD_DIM)
    else:
        k, v = jnp.split(t @ w_in[:, nqk:2 * nqk + nv], [nqk], axis=-1)
        q = g = None
    k = k.reshape(B, L, RET_HEADS, RET_HEAD_DIM) * (RET_HEAD_DIM ** -0.5)
    v = v.reshape(B, L, RET_HEADS, RET_V_DIM)
    return q, k, v, g


def retention_scan(q, k, v, log_g, state0, strict):
    B, L, H, dk = q.shape
    dv = v.shape[-1]
    C = RET_CHUNK
    n = L // C
    pos = jnp.arange(C, dtype=jnp.float32)
    diff = pos[:, None] - pos[None, :]
    mask = diff > 0 if strict else diff >= 0
    intra_decay = jnp.where(mask[None], jnp.exp(jnp.where(mask, diff, 0.0)[None] * log_g[:, None, None]), 0.0).astype(q.dtype)
    q_decay = jnp.exp((pos + 1)[:, None] * log_g[None]).astype(q.dtype)
    k_decay = jnp.exp((C - 1 - pos)[:, None] * log_g[None]).astype(q.dtype)
    chunk_decay = jnp.exp(C * log_g).astype(q.dtype)[None, :, None, None]
    qc = q.reshape(B, n, C, H, dk)
    kc = k.reshape(B, n, C, H, dk)
    vc = v.reshape(B, n, C, H, dv)
    scores = jnp.einsum('bnihd,bnjhd->bnhij', qc, kc) * intra_decay
    intra = jnp.einsum('bnhij,bnjhe->bnihe', scores, vc)
    xs = ((qc * q_decay[:, :, None]).swapaxes(0, 1),
          (kc * k_decay[:, :, None]).swapaxes(0, 1),
          vc.swapaxes(0, 1))

    def step(state, inp):
        qd, kd, vv = inp
        inter = jnp.einsum('bihd,bhde->bihe', qd, state)
        state = chunk_decay * state + jnp.einsum('bjhd,bjhe->bhde', kd, vv)
        return state, inter

    state, inter = lax.scan(step, state0, xs)
    out = intra + inter.swapaxes(0, 1)
    return out.reshape(B, L, H, dv), state


def retention_final_state(k, v, log_g):
    L = k.shape[1]
    w = jnp.exp((L - 1 - jnp.arange(L, dtype=jnp.float32))[:, None] * log_g[None]).astype(k.dtype)
    return jnp.einsum('blhd,blhe->bhde', k * w[:, :, None], v)


def ret_output(y, g, w_out):
    B, L = y.shape[:2]
    yn = rmsnorm(y).reshape(B, L, RET_HEADS * RET_V_DIM)
    return (jax.nn.silu(g) * yn) @ w_out


def retention_mixer(h, hc, w_in, decay_exp, w_out, cos, sin, ctx_out):
    log_g = jnp.log1p(-jnp.exp2(-decay_exp.astype(jnp.float32)))
    flip = lambda t: jnp.flip(t, axis=1)
    q, k, v, g = ret_project(h, w_in, True)
    q = apply_axial_rope(q, cos, sin)
    k = apply_axial_rope(k, cos, sin)
    qc, kc, vc, gc = ret_project(hc, w_in, ctx_out)
    yc = None
    if ctx_out:
        B = hc.shape[0]
        zero = jnp.zeros((B, RET_HEADS, RET_HEAD_DIM, RET_V_DIM), vc.dtype)
        yc_f, st_f = retention_scan(qc, kc, vc, log_g[0], zero, False)
        yc_b, st_b = retention_scan(flip(qc), flip(kc), flip(vc), log_g[1], zero, True)
        yc = ret_output(yc_f + flip(yc_b), gc, w_out)
    else:
        st_f = retention_final_state(kc, vc, log_g[0])
        st_b = retention_final_state(flip(kc), flip(vc), log_g[1])
    y_f, _ = retention_scan(q, k, v, log_g[0], st_f, False)
    y_b, _ = retention_scan(flip(q), flip(k), flip(v), log_g[1], st_b, True)
    return ret_output(y_f + flip(y_b), g, w_out), yc


def conv_ffn(h, w_up, conv_k, conv_b, w_down):
    u = dwconv3(h @ w_up, conv_k) + conv_b
    val, gate = jnp.split(u, 2, axis=-1)
    return (jax.nn.silu(gate) * val) @ w_down


def setup_inputs(seed: int = 0) -> dict:
    key = jax.random.key(seed)
    ks = iter(jax.random.split(key, 32))
    f32 = jnp.float32

    def normal(shape):
        return jax.random.normal(next(ks), shape, f32)

    def dense(shape, fan_in, gain=1.0):
        return normal(shape) * (gain * fan_in ** -0.5)

    qkv_w = (N_Q_HEADS + 2 * N_KV_HEADS) * HEAD_DIM
    ret_in_w = 2 * RET_HEADS * RET_HEAD_DIM + 2 * RET_HEADS * RET_V_DIM
    ret_v_w = RET_HEADS * RET_V_DIM
    return {
        "x": normal((BATCH, SEQ, D_MODEL)),
        "c": normal((BATCH, D_MODEL)),
        "ctx": normal((BATCH, CTX_LEN, D_MODEL)),
        "c_ctx": normal((D_MODEL,)),
        "ada_w": dense((DEPTH, D_MODEL, 6 * D_MODEL), D_MODEL, 0.5),
        "ada_b": 0.02 * normal((DEPTH, 6 * D_MODEL)),
        "norm_mix_g": 1.0 + 0.05 * normal((DEPTH, D_MODEL)),
        "norm_ffn_g": 1.0 + 0.05 * normal((DEPTH, D_MODEL)),
        "final_norm_g": 1.0 + 0.05 * normal((D_MODEL,)),
        "conv_w_in": dense((N_CONV_LAYERS, D_MODEL, 3 * D_MODEL), D_MODEL),
        "conv_k": dense((N_CONV_LAYERS, 3, D_MODEL), 3),
        "conv_w_out": dense((N_CONV_LAYERS, D_MODEL, D_MODEL), D_MODEL),
        "attn_w_qkv": dense((N_ATTN_LAYERS, D_MODEL, qkv_w), D_MODEL),
        "attn_q_norm_g": 1.0 + 0.05 * normal((N_ATTN_LAYERS, HEAD_DIM)),
        "attn_k_norm_g": 1.0 + 0.05 * normal((N_ATTN_LAYERS, HEAD_DIM)),
        "attn_w_out": dense((N_ATTN_LAYERS, D_MODEL, D_MODEL), D_MODEL),
        "ret_w_in": dense((N_RET_LAYERS, D_MODEL, ret_in_w), D_MODEL),
        "ret_decay": RET_DECAY_BASE + jnp.arange(RET_HEADS, dtype=f32) + 0.1 * normal((N_RET_LAYERS, 2, RET_HEADS)),
        "ret_w_out": dense((N_RET_LAYERS, ret_v_w, D_MODEL), ret_v_w),
        "ffn_w_up": dense((DEPTH, D_MODEL, 2 * D_FF), D_MODEL),
        "ffn_conv_k": dense((DEPTH, 3, 2 * D_FF), 3),
        "ffn_conv_b": 0.02 * normal((DEPTH, 2 * D_FF)),
        "ffn_w_down": dense((DEPTH, D_FF, D_MODEL), D_FF),
    }


def reference(x, c, ctx, c_ctx, ada_w, ada_b, norm_mix_g, norm_ffn_g, final_norm_g,
              conv_w_in, conv_k, conv_w_out, attn_w_qkv, attn_q_norm_g, attn_k_norm_g, attn_w_out,
              ret_w_in, ret_decay, ret_w_out, ffn_w_up, ffn_conv_k, ffn_conv_b, ffn_w_down):
    S = x.shape[1]
    ROWS = S // GRID_W
    rows = jnp.repeat(jnp.arange(ROWS, dtype=jnp.float32), GRID_W)
    cols = jnp.tile(jnp.arange(GRID_W, dtype=jnp.float32), ROWS)
    attn_cos, attn_sin = axial_angles(rows, cols, HEAD_DIM)
    ret_cos, ret_sin = axial_angles(rows, cols, RET_HEAD_DIM)

    kinds = [i % N_MIXERS for i in range(DEPTH)]
    reads_ctx = [kd in (1, 2) for kd in kinds]
    silu_c = jax.nn.silu(c)
    silu_cc = jax.nn.silu(c_ctx)
    cx = ctx
    for i in range(DEPTH):
        kind = kinds[i]
        j = i // N_MIXERS
        ctx_out = any(reads_ctx[i + 1:])
        ctx_in = reads_ctx[i] or ctx_out
        sh_m, sc_m, g_m, sh_f, sc_f, g_f = [m[:, None, :] for m in jnp.split(silu_c @ ada_w[i] + ada_b[i], 6, axis=-1)]
        h = modulate(x, norm_mix_g[i], sh_m, sc_m)
        hc = None
        if ctx_in:
            shc_m, scc_m, gc_m, shc_f, scc_f, gc_f = jnp.split(silu_cc @ ada_w[i] + ada_b[i], 6, axis=-1)
            hc = modulate(cx, norm_mix_g[i], shc_m, scc_m)
        if kind == 0:
            y = short_conv_mixer(h, conv_w_in[j], conv_k[j], conv_w_out[j])
            yc = short_conv_mixer(hc, conv_w_in[j], conv_k[j], conv_w_out[j]) if ctx_out else None
        elif kind == 1:
            y, yc = gqa_mixer(h, hc, attn_w_qkv[j], attn_q_norm_g[j], attn_k_norm_g[j], attn_w_out[j],
                              attn_cos, attn_sin, ctx_out)
        else:
            y, yc = retention_mixer(h, hc, ret_w_in[j], ret_decay[j], ret_w_out[j],
                                    ret_cos, ret_sin, ctx_out)
        x = x + g_m * y
        x = x + g_f * conv_ffn(modulate(x, norm_ffn_g[i], sh_f, sc_f),
                               ffn_w_up[i], ffn_conv_k[i], ffn_conv_b[i], ffn_w_down[i])
        if ctx_out:
            cx = cx + gc_m * yc
            cx = cx + gc_f * conv_ffn(modulate(cx, norm_ffn_g[i], shc_f, scc_f),
                                      ffn_w_up[i], ffn_conv_k[i], ffn_conv_b[i], ffn_w_down[i])
    return rmsnorm(x, final_norm_g)
```

```python
import functools

import jax
import jax.numpy as jnp
from jax import lax
from jax.experimental import pallas as pl
from jax.experimental.pallas import tpu as pltpu

F32 = jnp.float32
BF16 = jnp.bfloat16

D_MODEL = 1024
DEPTH = 4
GRID_W = 64
N_MIXERS = 3
HEAD_DIM = 64
N_Q_HEADS = D_MODEL // HEAD_DIM
N_KV_HEADS = N_Q_HEADS // 4
GQA_GROUP = N_Q_HEADS // N_KV_HEADS
ROPE_THETA = 10000.0
RET_HEAD_DIM = 256
RET_HEADS = D_MODEL // RET_HEAD_DIM
RET_V_DIM = 2 * RET_HEAD_DIM
RET_CHUNK = 128
D_FF = ((8 * D_MODEL // 3 + 127) // 128) * 128
NORM_EPS = 1e-6

LANES = 128
SUBLANES = 8
BF16_ROWS = 16
MXU_COLS = 256
VMEM_LIMIT = 56 * 1024 * 1024

MOD_ROWS = 16
CTX_ROW = 8


def _params(n_axes):
    return pltpu.CompilerParams(dimension_semantics=("arbitrary",) * n_axes,
                                vmem_limit_bytes=VMEM_LIMIT)


def _resident(shape):
    zeros = (0,) * len(shape)
    return pl.BlockSpec(shape, lambda *_: zeros, pipeline_mode=pl.Buffered(1))


def _silu(v):
    return v * (1.0 / (1.0 + jnp.exp(-v)))


def _modulate(xf, g, shift, scale):
    ms = jnp.mean(xf * xf, axis=-1, keepdims=True)
    y = xf * lax.rsqrt(ms + NORM_EPS) * g
    return y * (1.0 + scale) + shift


def _dot(a, b):
    return jnp.dot(a, b, preferred_element_type=F32)


def _dot_nt(a, b):
    return lax.dot_general(a, b, (((1,), (1,)), ((), ())), preferred_element_type=F32)


def _ada_kernel(c_ref, w_ref, b_ref, o_ref):
    s = _silu(c_ref[...]).astype(BF16)
    o_ref[...] = _dot(s, w_ref[...].astype(BF16)) + b_ref[...]


def _ada(cvec, ada_w, ada_b):
    depth, d, n = ada_w.shape
    tn = 1536
    return pl.pallas_call(
        _ada_kernel,
        grid=(depth, n // tn),
        in_specs=[
            pl.BlockSpec((MOD_ROWS, d), lambda l, j: (0, 0)),
            pl.BlockSpec((None, d, tn), lambda l, j: (l, 0, j)),
            pl.BlockSpec((None, 1, tn), lambda l, j: (l, 0, j)),
        ],
        out_specs=pl.BlockSpec((None, MOD_ROWS, tn), lambda l, j: (l, 0, j)),
        out_shape=jax.ShapeDtypeStruct((depth, MOD_ROWS, n), F32),
        compiler_params=_params(2),
        name="ada",
    )(cvec, ada_w, ada_b.reshape(depth, 1, n))


def _mod_spec(ctx):
    if ctx:
        return pl.BlockSpec((None, 6, 1, D_MODEL), lambda b, t: (CTX_ROW, 0, 0, 0))
    return pl.BlockSpec((None, 6, 1, D_MODEL), lambda b, t: (b, 0, 0, 0))


def _convmlp_kernel(x_ref, xp_ref, xn_ref, mod_ref, g_ref, w1_ref, ck_ref, cb_ref, w2_ref, fg_ref,
                    o_ref, h_ref, acc_ref, *, kind, bm, mod_base, final_norm):
    t = pl.program_id(1)
    nt = pl.num_programs(1)
    shift = mod_ref[mod_base]
    scale = mod_ref[mod_base + 1]
    gate = mod_ref[mod_base + 2]
    g = g_ref[...]
    x = x_ref[...]
    h_ref[0:bm, :] = _modulate(x, g, shift, scale).astype(BF16)
    keep_n = (t < nt - 1).astype(F32)
    keep_p = (t > 0).astype(F32)
    halo = jnp.concatenate([_modulate(xn_ref[...], g, shift, scale) * keep_n,
                            _modulate(xp_ref[...], g, shift, scale) * keep_p], axis=0)
    h_ref[bm:bm + BF16_ROWS, :] = halo.astype(BF16)
    h = h_ref[...]
    rows = bm + BF16_ROWS

    def conv3(u, c0):
        k0 = ck_ref[0:1, c0:c0 + MXU_COLS]
        k1 = ck_ref[1:2, c0:c0 + MXU_COLS]
        k2 = ck_ref[2:3, c0:c0 + MXU_COLS]
        down = pltpu.roll(u, 1, axis=0)
        up = pltpu.roll(u, rows - 1, axis=0)
        return (down * k0 + u * k1 + up * k2)[0:bm]

    if kind == "ffn":
        n_mid = D_FF
    else:
        n_mid = D_MODEL
    for ci in range(n_mid // MXU_COLS):
        c0 = ci * MXU_COLS
        if kind == "ffn":
            val = _dot(h, w1_ref[:, c0:c0 + MXU_COLS])
            gat = _dot(h, w1_ref[:, D_FF + c0:D_FF + c0 + MXU_COLS])
            cv = conv3(val, c0) + cb_ref[:, c0:c0 + MXU_COLS]
            cg = conv3(gat, D_FF + c0) + cb_ref[:, D_FF + c0:D_FF + c0 + MXU_COLS]
            z = _silu(cg) * cv
        else:
            bg = _dot(h, w1_ref[:, c0:c0 + MXU_COLS])
            cgt = _dot(h, w1_ref[:, D_MODEL + c0:D_MODEL + c0 + MXU_COLS])
            vv = _dot(h, w1_ref[:, 2 * D_MODEL + c0:2 * D_MODEL + c0 + MXU_COLS])
            z = bg[0:bm] * conv3(cgt * vv, c0)
        part = _dot(z.astype(BF16), w2_ref[c0:c0 + MXU_COLS, :])
        if ci == 0:
            acc_ref[...] = part
        else:
            acc_ref[...] += part
    out = x + gate * acc_ref[...]
    if final_norm:
        ms = jnp.mean(out * out, axis=-1, keepdims=True)
        out = out * lax.rsqrt(ms + NORM_EPS) * fg_ref[...]
    o_ref[...] = out


def _convmlp(x, mod, norm_g, w1, ck, cb, w2, final_g, *, kind, ctx, final_norm=False):
    b, l, d = x.shape
    bm = min(l, 512)
    nt = l // bm
    hb = bm // SUBLANES
    last_hb = l // SUBLANES - 1
    kern = functools.partial(_convmlp_kernel, kind=kind, bm=bm,
                             mod_base=3 if kind == "ffn" else 0, final_norm=final_norm)
    return pl.pallas_call(
        kern,
        grid=(b, nt),
        in_specs=[
            pl.BlockSpec((None, bm, d), lambda i, t: (i, t, 0)),
            pl.BlockSpec((None, SUBLANES, d), lambda i, t: (i, jnp.maximum(t * hb - 1, 0), 0)),
            pl.BlockSpec((None, SUBLANES, d), lambda i, t: (i, jnp.minimum((t + 1) * hb, last_hb), 0)),
            _mod_spec(ctx),
            _resident((1, d)),
            _resident(w1.shape),
            _resident(ck.shape),
            _resident(cb.shape),
            _resident(w2.shape),
            _resident((1, d)),
        ],
        out_specs=pl.BlockSpec((None, bm, d), lambda i, t: (i, t, 0)),
        out_shape=jax.ShapeDtypeStruct((b, l, d), F32),
        scratch_shapes=[pltpu.VMEM((bm + BF16_ROWS, d), BF16), pltpu.VMEM((bm, d), F32)],
        compiler_params=_params(2),
        name="convmlp_" + kind,
    )(x, x, x, mod, norm_g, w1, ck, cb, w2, final_g)


def _outproj_kernel(a_ref, x_ref, mod_ref, w_ref, o_ref):
    o_ref[...] = x_ref[...] + mod_ref[2] * _dot(a_ref[...], w_ref[...])


def _outproj(a, x, mod, w, *, ctx):
    b, l, d = x.shape
    k = a.shape[-1]
    bm = min(l, 512)
    return pl.pallas_call(
        _outproj_kernel,
        grid=(b, l // bm),
        in_specs=[
            pl.BlockSpec((None, bm, k), lambda i, t: (i, t, 0)),
            pl.BlockSpec((None, bm, d), lambda i, t: (i, t, 0)),
            _mod_spec(ctx),
            _resident(w.shape),
        ],
        out_specs=pl.BlockSpec((None, bm, d), lambda i, t: (i, t, 0)),
        out_shape=jax.ShapeDtypeStruct((b, l, d), F32),
        input_output_aliases={1: 0},
        compiler_params=_params(2),
        name="outproj",
    )(a, x, mod, w)


def _head_meansq(v, bd_ref):
    sq = v * v
    hi = sq.astype(BF16)
    lo = (sq - hi.astype(F32)).astype(BF16)
    return _dot(hi, bd_ref[...]) + _dot(lo, bd_ref[...])


def _rope64(v, cos, s_up, s_dn):
    return v * cos + pltpu.roll(v, LANES - 16, axis=1) * s_up + pltpu.roll(v, 16, axis=1) * s_dn


def _qkv_kernel(x_ref, mod_ref, g_ref, w_ref, qg_ref, kg_ref, bd_ref, cos_ref, sup_ref, sdn_ref,
                q_ref, k_ref, v_ref, *, rope):
    h = _modulate(x_ref[...], g_ref[...], mod_ref[0], mod_ref[1]).astype(BF16)
    nq = N_Q_HEADS * HEAD_DIM
    nkv = N_KV_HEADS * HEAD_DIM
    qscale = HEAD_DIM ** -0.5
    if rope:
        cos = cos_ref[...]
        s_up = sup_ref[...]
        s_dn = sdn_ref[...]

    def norm_rope(c0, gain):
        u = _dot(h, w_ref[:, c0:c0 + MXU_COLS])
        u = u * lax.rsqrt(_head_meansq(u, bd_ref) + NORM_EPS)
        halves = []
        for j in range(MXU_COLS // LANES):
            s = u[:, j * LANES:(j + 1) * LANES] * gain
            if rope:
                s = _rope64(s, cos, s_up, s_dn)
            halves.append(s)
        return halves

    for ci in range(nq // MXU_COLS):
        halves = norm_rope(ci * MXU_COLS, qg_ref[...])
        for j, s in enumerate(halves):
            lo = ci * MXU_COLS + j * LANES
            q_ref[:, lo:lo + LANES] = (s * qscale).astype(BF16)
    halves = norm_rope(nq, kg_ref[...])
    for j, s in enumerate(halves):
        for e in range(LANES // HEAD_DIM):
            k_ref[j * (LANES // HEAD_DIM) + e] = s[:, e * HEAD_DIM:(e + 1) * HEAD_DIM].astype(BF16)
    vv = _dot(h, w_ref[:, nq + nkv:nq + 2 * nkv])
    for e in range(N_KV_HEADS):
        v_ref[e] = vv[:, e * HEAD_DIM:(e + 1) * HEAD_DIM].astype(BF16)


def _qkv(x, mod, norm_g, w, qg, kg, bd, cos, s_up, s_dn, *, ctx):
    b, l, d = x.shape
    bm = min(l, 512)
    tab = pl.BlockSpec((bm, LANES), lambda i, t: (t, 0))
    kern = functools.partial(_qkv_kernel, rope=not ctx)
    return pl.pallas_call(
        kern,
        grid=(b, l // bm),
        in_specs=[
            pl.BlockSpec((None, bm, d), lambda i, t: (i, t, 0)),
            _mod_spec(ctx),
            _resident((1, d)),
            _resident(w.shape),
            _resident((1, LANES)),
            _resident((1, LANES)),
            _resident(bd.shape),
            tab, tab, tab,
        ],
        out_specs=[
            pl.BlockSpec((None, bm, d), lambda i, t: (i, t, 0)),
            pl.BlockSpec((None, N_KV_HEADS, bm, HEAD_DIM), lambda i, t: (i, 0, t, 0)),
            pl.BlockSpec((None, N_KV_HEADS, bm, HEAD_DIM), lambda i, t: (i, 0, t, 0)),
        ],
        out_shape=[
            jax.ShapeDtypeStruct((b, l, d), BF16),
            jax.ShapeDtypeStruct((b, N_KV_HEADS, l, HEAD_DIM), BF16),
            jax.ShapeDtypeStruct((b, N_KV_HEADS, l, HEAD_DIM), BF16),
        ],
        compiler_params=_params(2),
        name="qkv",
    )(x, mod, norm_g, w, qg, kg, bd, cos, s_up, s_dn)


def _attn_kernel(*refs, n_sets):
    q_ref = refs[0]
    kv = refs[1:1 + 2 * n_sets]
    o_ref = refs[1 + 2 * n_sets]
    q = q_ref[...]
    ks = [kv[2 * i][...] for i in range(n_sets)]
    vs = [kv[2 * i + 1][...] for i in range(n_sets)]
    outs = []
    for gi in range(GQA_GROUP):
        qg = q[:, gi * HEAD_DIM:(gi + 1) * HEAD_DIM]
        ss = [_dot_nt(qg, k) for k in ks]
        m = ss[0].max(axis=-1, keepdims=True)
        for s in ss[1:]:
            m = jnp.maximum(m, s.max(axis=-1, keepdims=True))
        den = None
        num = None
        for s, v in zip(ss, vs):
            p = jnp.exp(s - m)
            ps = p.sum(axis=-1, keepdims=True)
            pv = _dot(p.astype(BF16), v)
            den = ps if den is None else den + ps
            num = pv if num is None else num + pv
        outs.append(num / den)
    o_ref[...] = jnp.concatenate(outs, axis=-1).astype(BF16)


def _attention(q, kvs):
    b, l, d = q.shape
    bq = min(l, 256)
    width = GQA_GROUP * HEAD_DIM
    in_specs = [pl.BlockSpec((None, bq, width), lambda i, h, t: (i, t, h))]
    args = [q]
    for k, v in kvs:
        lk = k.shape[2]
        spec = pl.BlockSpec((None, None, lk, HEAD_DIM), lambda i, h, t: (i, h, 0, 0))
        in_specs += [spec, spec]
        args += [k, v]
    return pl.pallas_call(
        functools.partial(_attn_kernel, n_sets=len(kvs)),
        grid=(b, N_KV_HEADS, l // bq),
        in_specs=in_specs,
        out_specs=pl.BlockSpec((None, bq, width), lambda i, h, t: (i, t, h)),
        out_shape=jax.ShapeDtypeStruct((b, l, d), BF16),
        compiler_params=_params(3),
        name="attention",
    )(*args)


def _rope256(v, cos, sgn_sin):
    return v * cos + pltpu.roll(v, LANES // 2, axis=1) * sgn_sin


def _retproj_kernel(x_ref, mod_ref, g_ref, w_ref, cos_ref, sin_ref, *out_refs, ctx):
    h = _modulate(x_ref[...], g_ref[...], mod_ref[0], mod_ref[1]).astype(BF16)
    nqk = RET_HEADS * RET_HEAD_DIM
    nv = RET_HEADS * RET_V_DIM
    kscale = RET_HEAD_DIM ** -0.5
    if ctx:
        k_ref, v_ref = out_refs
        col = 0
    else:
        q_ref, k_ref, v_ref, gt_ref = out_refs
        col = nqk
        for ci in range(nqk // MXU_COLS):
            u = _dot(h, w_ref[:, ci * MXU_COLS:(ci + 1) * MXU_COLS])
            for j in range(MXU_COLS // LANES):
                sl = slice(j * LANES, (j + 1) * LANES)
                lo = ci * MXU_COLS + j * LANES
                q_ref[:, lo:lo + LANES] = _rope256(u[:, sl], cos_ref[:, sl], sin_ref[:, sl]).astype(BF16)
    for ci in range(nqk // MXU_COLS):
        u = _dot(h, w_ref[:, col + ci * MXU_COLS:col + (ci + 1) * MXU_COLS]) * kscale
        for j in range(MXU_COLS // LANES):
            sl = slice(j * LANES, (j + 1) * LANES)
            lo = ci * MXU_COLS + j * LANES
            s = u[:, sl]
            if not ctx:
                s = _rope256(s, cos_ref[:, sl], sin_ref[:, sl])
            k_ref[:, lo:lo + LANES] = s.astype(BF16)
    col += nqk
    for ci in range(nv // MXU_COLS):
        sl = slice(ci * MXU_COLS, (ci + 1) * MXU_COLS)
        v_ref[:, sl] = _dot(h, w_ref[:, col + ci * MXU_COLS:col + (ci + 1) * MXU_COLS]).astype(BF16)
    if not ctx:
        col += nv
        for ci in range(nv // MXU_COLS):
            sl = slice(ci * MXU_COLS, (ci + 1) * MXU_COLS)
            gt_ref[:, sl] = _dot(h, w_ref[:, col + ci * MXU_COLS:col + (ci + 1) * MXU_COLS]).astype(BF16)


def _retproj(x, mod, norm_g, w, cos, sin, *, ctx):
    b, l, d = x.shape
    bm = min(l, 512)
    nqk = RET_HEADS * RET_HEAD_DIM
    nv = RET_HEADS * RET_V_DIM
    row = lambda n: pl.BlockSpec((None, bm, n), lambda i, t: (i, t, 0))
    tab = pl.BlockSpec((bm, RET_HEAD_DIM), lambda i, t: (t, 0))
    widths = [nqk, nv] if ctx else [nqk, nqk, nv, nv]
    return pl.pallas_call(
        functools.partial(_retproj_kernel, ctx=ctx),
        grid=(b, l // bm),
        in_specs=[row(d), _mod_spec(ctx), _resident((1, d)), _resident(w.shape), tab, tab],
        out_specs=[row(n) for n in widths],
        out_shape=[jax.ShapeDtypeStruct((b, l, n), BF16) for n in widths],
        compiler_params=_params(2),
        name="retproj",
    )(x, mod, norm_g, w, cos, sin)


def _retention_kernel(dec_ref, q_ref, k_ref, v_ref, gt_ref, kc_ref, vc_ref, o_ref,
                      sf_ref, sb_ref, y_ref, *, seq, ctx_len):
    hd = pl.program_id(1)
    c = RET_CHUNK
    n_chunks = seq // c

    def log_gamma(i):
        dv = jnp.full((1, 1), dec_ref[i], F32)
        return jnp.log1p(-jnp.exp2(-dv))

    lf = log_gamma(hd)
    lb = log_gamma(RET_HEADS + hd)
    ii = lax.broadcasted_iota(jnp.int32, (c, c), 0)
    jj = lax.broadcasted_iota(jnp.int32, (c, c), 1)
    diff = (ii - jj).astype(F32)
    decay = jnp.where(diff >= 0, jnp.exp(diff * lf), jnp.exp(-diff * lb))
    pos = lax.broadcasted_iota(jnp.int32, (c, 1), 0).astype(F32)
    qd_f = jnp.exp((pos + 1.0) * lf)
    kd_f = jnp.exp((c - 1.0 - pos) * lf)
    qd_b = jnp.exp((c - pos) * lb)
    kd_b = jnp.exp(pos * lb)
    cd_f = jnp.exp(c * lf)
    cd_b = jnp.exp(c * lb)

    def outer(kd, v):
        return _dot(kd.T.astype(BF16), v)

    posc = lax.broadcasted_iota(jnp.int32, (ctx_len, 1), 0).astype(F32)
    kcf = kc_ref[...].astype(F32)
    vc = vc_ref[...]
    sf_ref[...] = outer(kcf * jnp.exp((ctx_len - 1.0 - posc) * lf), vc)
    sb_ref[...] = outer(kcf * jnp.exp(posc * lb), vc)

    def fwd(n, carry):
        r = pl.multiple_of(n * c, c)
        q = q_ref[pl.ds(r, c), :]
        k = k_ref[pl.ds(r, c), :]
        v = v_ref[pl.ds(r, c), :]
        a = (_dot_nt(q, k) * decay).astype(BF16)
        st = sf_ref[...]
        y_ref[pl.ds(r, c), :] = _dot(a, v) + _dot(q, st.astype(BF16)) * qd_f
        sf_ref[...] = cd_f * st + outer(k.astype(F32) * kd_f, v)
        return carry

    lax.fori_loop(0, n_chunks, fwd, 0)

    def bwd(m, carry):
        r = pl.multiple_of((n_chunks - 1 - m) * c, c)
        q = q_ref[pl.ds(r, c), :]
        k = k_ref[pl.ds(r, c), :]
        v = v_ref[pl.ds(r, c), :]
        st = sb_ref[...]
        y = y_ref[pl.ds(r, c), :] + _dot(q, st.astype(BF16)) * qd_b
        sb_ref[...] = cd_b * st + outer(k.astype(F32) * kd_b, v)
        ms = jnp.mean(y * y, axis=-1, keepdims=True)
        yn = y * lax.rsqrt(ms + NORM_EPS)
        gt = gt_ref[pl.ds(r, c), :].astype(F32)
        o_ref[pl.ds(r, c), :] = (_silu(gt) * yn).astype(BF16)
        return carry

    lax.fori_loop(0, n_chunks, bwd, 0)


def _retention(dec, q, k, v, gt, kc, vc):
    b, l, _ = q.shape
    lc = kc.shape[1]
    qk = lambda n: pl.BlockSpec((None, n, RET_HEAD_DIM), lambda i, h: (i, 0, h))
    vv = lambda n: pl.BlockSpec((None, n, RET_V_DIM), lambda i, h: (i, 0, h))
    return pl.pallas_call(
        functools.partial(_retention_kernel, seq=l, ctx_len=lc),
        grid=(b, RET_HEADS),
        in_specs=[pl.BlockSpec(memory_space=pltpu.SMEM),
                  qk(l), qk(l), vv(l), vv(l), qk(lc), vv(lc)],
        out_specs=vv(l),
        out_shape=jax.ShapeDtypeStruct((b, l, RET_HEADS * RET_V_DIM), BF16),
        scratch_shapes=[pltpu.VMEM((RET_HEAD_DIM, RET_V_DIM), F32),
                        pltpu.VMEM((RET_HEAD_DIM, RET_V_DIM), F32),
                        pltpu.VMEM((l, RET_V_DIM), F32)],
        compiler_params=_params(2),
        name="retention",
    )(dec, q, k, v, gt, kc, vc)


def _rope_tables(seq, head_dim):
    quarter = head_dim // 4
    rows = jnp.repeat(jnp.arange(seq // GRID_W, dtype=F32), GRID_W)
    cols = jnp.tile(jnp.arange(GRID_W, dtype=F32), seq // GRID_W)
    inv = ROPE_THETA ** (-jnp.arange(quarter, dtype=F32) / quarter)
    ang = jnp.stack([rows[:, None] * inv, cols[:, None] * inv], axis=1)
    cos = jnp.cos(ang)
    sin = jnp.sin(ang)
    cos_t = jnp.stack([cos, cos], axis=2).reshape(seq, head_dim)
    sin_first = jnp.stack([-sin, jnp.zeros_like(sin)], axis=2).reshape(seq, head_dim)
    sin_second = jnp.stack([jnp.zeros_like(sin), sin], axis=2).reshape(seq, head_dim)
    return cos_t, sin_first, sin_second


def kernel(x, c, ctx, c_ctx, ada_w, ada_b, norm_mix_g, norm_ffn_g, final_norm_g, conv_w_in, conv_k, conv_w_out, attn_w_qkv, attn_q_norm_g, attn_k_norm_g, attn_w_out, ret_w_in, ret_decay, ret_w_out, ffn_w_up, ffn_conv_k, ffn_conv_b, ffn_w_down):
    batch, seq, d = x.shape
    assert d == D_MODEL and seq % 512 == 0 and ctx.shape[1] % RET_CHUNK == 0

    cvec = jnp.zeros((MOD_ROWS, d), F32).at[:batch].set(c).at[CTX_ROW].set(c_ctx)
    mod_all = _ada(cvec, ada_w, ada_b).reshape(DEPTH, MOD_ROWS, 6, 1, d)

    a_cos, a_first, a_second = _rope_tables(seq, HEAD_DIM)
    rep = LANES // HEAD_DIM
    a_cos, a_first, a_second = (jnp.tile(t, (1, rep)) for t in (a_cos, a_first, a_second))
    r_cos, r_first, r_second = _rope_tables(seq, RET_HEAD_DIM)
    r_sin = r_first + r_second
    head_sum = jnp.kron(jnp.eye(MXU_COLS // HEAD_DIM, dtype=F32),
                        jnp.full((HEAD_DIM, HEAD_DIM), 1.0 / HEAD_DIM, F32)).astype(BF16)

    kinds = [i % N_MIXERS for i in range(DEPTH)]
    reads_ctx = [kd in (1, 2) for kd in kinds]
    row1 = lambda v: v.reshape(1, -1)
    cx = ctx
    for i in range(DEPTH):
        kind = kinds[i]
        j = i // N_MIXERS
        ctx_out = any(reads_ctx[i + 1:])
        mod = mod_all[i]
        ng = row1(norm_mix_g[i])
        last = i == DEPTH - 1
        if kind == 0:
            w1 = conv_w_in[j].astype(BF16)
            w2 = conv_w_out[j].astype(BF16)
            nb = jnp.zeros((1, d), F32)
            mix = functools.partial(_convmlp, norm_g=ng, w1=w1, ck=conv_k[j], cb=nb, w2=w2,
                                    final_g=row1(final_norm_g), kind="mix")
            x = mix(x, mod, ctx=False)
            if ctx_out:
                cx = mix(cx, mod, ctx=True)
        elif kind == 1:
            wq = attn_w_qkv[j].astype(BF16)
            wo = attn_w_out[j].astype(BF16)
            qg = jnp.tile(row1(attn_q_norm_g[j]), (1, rep))
            kg = jnp.tile(row1(attn_k_norm_g[j]), (1, rep))
            proj = functools.partial(_qkv, norm_g=ng, w=wq, qg=qg, kg=kg, bd=head_sum,
                                     cos=a_cos, s_up=a_first, s_dn=a_second)
            q, k, v = proj(x, mod, ctx=False)
            qc, kc, vc = proj(cx, mod, ctx=True)
            o = _attention(q, [(k, v), (kc, vc)])
            x = _outproj(o, x, mod, wo, ctx=False)
            if ctx_out:
                oc = _attention(qc, [(kc, vc)])
                cx = _outproj(oc, cx, mod, wo, ctx=True)
        else:
            wi = ret_w_in[j].astype(BF16)
            wo = ret_w_out[j].astype(BF16)
            nqk = RET_HEADS * RET_HEAD_DIM
            nv = RET_HEADS * RET_V_DIM
            q, k, v, gt = _retproj(x, mod, ng, wi, r_cos, r_sin, ctx=False)
            kc, vc = _retproj(cx, mod, ng, wi[:, nqk:2 * nqk + nv], r_cos, r_sin, ctx=True)
            a = _retention(ret_decay[j].reshape(-1), q, k, v, gt, kc, vc)
            x = _outproj(a, x, mod, wo, ctx=False)
            assert not ctx_out
        ffn = functools.partial(_convmlp, norm_g=row1(norm_ffn_g[i]), w1=ffn_w_up[i].astype(BF16),
                                ck=ffn_conv_k[i], cb=row1(ffn_conv_b[i]),
                                w2=ffn_w_down[i].astype(BF16), final_g=row1(final_norm_g), kind="ffn")
        x = ffn(x, mod, ctx=False, final_norm=last)
        if ctx_out:
            cx = ffn(cx, mod, ctx=True)
    return x
```

```python
import functools

import jax
import jax.numpy as jnp
from jax import lax
from jax.experimental import pallas as pl
from jax.experimental.pallas import tpu as pltpu

F32 = jnp.float32
BF16 = jnp.bfloat16

D_MODEL = 1024
DEPTH = 4
GRID_W = 64
N_MIXERS = 3
HEAD_DIM = 64
N_Q_HEADS = D_MODEL // HEAD_DIM
N_KV_HEADS = N_Q_HEADS // 4
GQA_GROUP = N_Q_HEADS // N_KV_HEADS
ROPE_THETA = 10000.0
RET_HEAD_DIM = 256
RET_HEADS = D_MODEL // RET_HEAD_DIM
RET_V_DIM = 2 * RET_HEAD_DIM
RET_BLOCK = 256
D_FF = ((8 * D_MODEL // 3 + 127) // 128) * 128
NORM_EPS = 1e-6
LOG2_E = 1.4426950408889634

LANES = 128
SUBLANES = 8
BF16_ROWS = 16
MXU_COLS = 256
VMEM_LIMIT = 56 * 1024 * 1024

MOD_ROWS = 16
CTX_ROW = 8


def _params(n_axes):
    return pltpu.CompilerParams(dimension_semantics=("arbitrary",) * n_axes,
                                vmem_limit_bytes=VMEM_LIMIT)


def _resident(shape):
    zeros = (0,) * len(shape)
    return pl.BlockSpec(shape, lambda *_: zeros, pipeline_mode=pl.Buffered(1))


def _silu(v):
    return v * (1.0 / (1.0 + jnp.exp(-v)))


def _modulate(xf, g, shift, scale):
    ms = jnp.mean(xf * xf, axis=-1, keepdims=True)
    y = xf * lax.rsqrt(ms + NORM_EPS) * g
    return y * (1.0 + scale) + shift


def _dot(a, b):
    return jnp.dot(a, b, preferred_element_type=F32)


def _dot_nt(a, b):
    return lax.dot_general(a, b, (((1,), (1,)), ((), ())), preferred_element_type=F32)


def _ada_kernel(c_ref, w_ref, b_ref, o_ref):
    s = _silu(c_ref[...]).astype(BF16)
    o_ref[...] = _dot(s, w_ref[...].astype(BF16)) + b_ref[...]


def _ada(cvec, ada_w, ada_b):
    depth, d, n = ada_w.shape
    tn = 1536
    return pl.pallas_call(
        _ada_kernel,
        grid=(depth, n // tn),
        in_specs=[
            pl.BlockSpec((MOD_ROWS, d), lambda l, j: (0, 0)),
            pl.BlockSpec((None, d, tn), lambda l, j: (l, 0, j)),
            pl.BlockSpec((None, 1, tn), lambda l, j: (l, 0, j)),
        ],
        out_specs=pl.BlockSpec((None, MOD_ROWS, tn), lambda l, j: (l, 0, j)),
        out_shape=jax.ShapeDtypeStruct((depth, MOD_ROWS, n), F32),
        compiler_params=_params(2),
        name="ada",
    )(cvec, ada_w, ada_b.reshape(depth, 1, n))


def _mod_spec(ctx):
    if ctx:
        return pl.BlockSpec((None, 6, 1, D_MODEL), lambda b, t: (CTX_ROW, 0, 0, 0))
    return pl.BlockSpec((None, 6, 1, D_MODEL), lambda b, t: (b, 0, 0, 0))


def _convmlp_kernel(x_ref, xp_ref, xn_ref, mod_ref, g_ref, w1_ref, ck_ref, cb_ref, w2_ref, fg_ref,
                    o_ref, h_ref, u_ref, z_ref, *, kind, bm, mod_base, final_norm):
    t = pl.program_id(1)
    nt = pl.num_programs(1)
    shift = mod_ref[mod_base]
    scale = mod_ref[mod_base + 1]
    gate = mod_ref[mod_base + 2]
    g = g_ref[...]
    x = x_ref[...]
    h_ref[0:bm, :] = _modulate(x, g, shift, scale).astype(BF16)
    keep_n = (t < nt - 1).astype(F32)
    keep_p = (t > 0).astype(F32)
    halo = jnp.concatenate([_modulate(xn_ref[...], g, shift, scale) * keep_n,
                            _modulate(xp_ref[...], g, shift, scale) * keep_p], axis=0)
    h_ref[bm:bm + BF16_ROWS, :] = halo.astype(BF16)
    h = h_ref[...]
    n_mid = D_FF if kind == "ffn" else D_MODEL
    n_chunks = n_mid // MXU_COLS
    top = SUBLANES

    def rows_of(s, pos, n):
        return pl.ds(s + 2 * pos, n, stride=2)

    def put(slot, s, u):
        for j in range(MXU_COLS // LANES):
            uj = u[:, j * LANES:(j + 1) * LANES]
            u_ref[slot, j, rows_of(s, 0, top), :] = uj[bm + top:bm + 2 * top]
            u_ref[slot, j, rows_of(s, top, bm), :] = uj[0:bm]
            u_ref[slot, j, rows_of(s, top + bm, top), :] = uj[bm:bm + top]

    def conv3(slot, j, s, c0):
        k0 = ck_ref[0:1, c0:c0 + LANES]
        k1 = ck_ref[1:2, c0:c0 + LANES]
        k2 = ck_ref[2:3, c0:c0 + LANES]
        return (u_ref[slot, j, rows_of(s, top - 1, bm), :] * k0
                + u_ref[slot, j, rows_of(s, top, bm), :] * k1
                + u_ref[slot, j, rows_of(s, top + 1, bm), :] * k2)

    def up(ci):
        slot = ci % 2
        cols = lambda s: slice(s * n_mid + ci * MXU_COLS, s * n_mid + (ci + 1) * MXU_COLS)
        if kind == "ffn":
            put(slot, 0, _dot(h, w1_ref[:, cols(0)]))
            put(slot, 1, _dot(h, w1_ref[:, cols(1)]))
        else:
            put(slot, 0, _dot(h, w1_ref[:, cols(0)]))
            put(slot, 1, _dot(h, w1_ref[:, cols(1)]) * _dot(h, w1_ref[:, cols(2)]))

    def mid(ci):
        slot = ci % 2
        for j in range(MXU_COLS // LANES):
            c0 = ci * MXU_COLS + j * LANES
            if kind == "ffn":
                cv = conv3(slot, j, 0, c0) + cb_ref[:, c0:c0 + LANES]
                cg = conv3(slot, j, 1, D_FF + c0) + cb_ref[:, D_FF + c0:D_FF + c0 + LANES]
                z = _silu(cg) * cv
            else:
                z = u_ref[slot, j, rows_of(0, top, bm), :] * conv3(slot, j, 1, c0)
            z_ref[:, c0:c0 + LANES] = z.astype(BF16)

    up(0)
    for ci in range(n_chunks):
        if ci + 1 < n_chunks:
            up(ci + 1)
        mid(ci)
    out = x + gate * _dot(z_ref[...], w2_ref[...])
    if final_norm:
        ms = jnp.mean(out * out, axis=-1, keepdims=True)
        out = out * lax.rsqrt(ms + NORM_EPS) * fg_ref[...]
    o_ref[...] = out


def _convmlp(x, mod, norm_g, w1, ck, cb, w2, final_g, *, kind, ctx, final_norm=False):
    b, l, d = x.shape
    bm = min(l, 512)
    nt = l // bm
    hb = bm // SUBLANES
    last_hb = l // SUBLANES - 1
    kern = functools.partial(_convmlp_kernel, kind=kind, bm=bm,
                             mod_base=3 if kind == "ffn" else 0, final_norm=final_norm)
    return pl.pallas_call(
        kern,
        grid=(b, nt),
        in_specs=[
            pl.BlockSpec((None, bm, d), lambda i, t: (i, t, 0)),
            pl.BlockSpec((None, SUBLANES, d), lambda i, t: (i, jnp.maximum(t * hb - 1, 0), 0)),
            pl.BlockSpec((None, SUBLANES, d), lambda i, t: (i, jnp.minimum((t + 1) * hb, last_hb), 0)),
            _mod_spec(ctx),
            _resident((1, d)),
            _resident(w1.shape),
            _resident(ck.shape),
            _resident(cb.shape),
            _resident(w2.shape),
            _resident((1, d)),
        ],
        out_specs=pl.BlockSpec((None, bm, d), lambda i, t: (i, t, 0)),
        out_shape=jax.ShapeDtypeStruct((b, l, d), F32),
        scratch_shapes=[pltpu.VMEM((bm + BF16_ROWS, d), BF16),
                        pltpu.VMEM((2, MXU_COLS // LANES, 2 * (bm + 2 * SUBLANES), LANES), F32),
                        pltpu.VMEM((bm, w2.shape[0]), BF16)],
        compiler_params=_params(2),
        name="convmlp_" + kind,
    )(x, x, x, mod, norm_g, w1, ck, cb, w2, final_g)


def _outproj_kernel(a_ref, x_ref, mod_ref, w_ref, o_ref):
    o_ref[...] = x_ref[...] + mod_ref[2] * _dot(a_ref[...], w_ref[...])


def _outproj(a, x, mod, w, *, ctx):
    b, l, d = x.shape
    k = a.shape[-1]
    bm = min(l, 512)
    return pl.pallas_call(
        _outproj_kernel,
        grid=(b, l // bm),
        in_specs=[
            pl.BlockSpec((None, bm, k), lambda i, t: (i, t, 0)),
            pl.BlockSpec((None, bm, d), lambda i, t: (i, t, 0)),
            _mod_spec(ctx),
            _resident(w.shape),
        ],
        out_specs=pl.BlockSpec((None, bm, d), lambda i, t: (i, t, 0)),
        out_shape=jax.ShapeDtypeStruct((b, l, d), F32),
        input_output_aliases={1: 0},
        compiler_params=_params(2),
        name="outproj",
    )(a, x, mod, w)


def _head_meansq(v, bd_ref):
    sq = v * v
    hi = sq.astype(BF16)
    lo = (sq - hi.astype(F32)).astype(BF16)
    return _dot(hi, bd_ref[...]) + _dot(lo, bd_ref[...])


def _rope64(v, cos, s_up, s_dn):
    return v * cos + pltpu.roll(v, LANES - 16, axis=1) * s_up + pltpu.roll(v, 16, axis=1) * s_dn


def _qkv_kernel(x_ref, mod_ref, g_ref, w_ref, qg_ref, kg_ref, bd_ref, cos_ref, sup_ref, sdn_ref,
                q_ref, k_ref, v_ref, *, rope):
    h = _modulate(x_ref[...], g_ref[...], mod_ref[0], mod_ref[1]).astype(BF16)
    nq = N_Q_HEADS * HEAD_DIM
    nkv = N_KV_HEADS * HEAD_DIM
    qscale = HEAD_DIM ** -0.5 * LOG2_E
    if rope:
        cos = cos_ref[...]
        s_up = sup_ref[...]
        s_dn = sdn_ref[...]

    def norm_rope(c0, gain):
        u = _dot(h, w_ref[:, c0:c0 + MXU_COLS])
        u = u * lax.rsqrt(_head_meansq(u, bd_ref) + NORM_EPS)
        halves = []
        for j in range(MXU_COLS // LANES):
            s = u[:, j * LANES:(j + 1) * LANES] * gain
            if rope:
                s = _rope64(s, cos, s_up, s_dn)
            halves.append(s)
        return halves

    for ci in range(nq // MXU_COLS):
        halves = norm_rope(ci * MXU_COLS, qg_ref[...])
        for j, s in enumerate(halves):
            lo = ci * MXU_COLS + j * LANES
            q_ref[:, lo:lo + LANES] = (s * qscale).astype(BF16)
    halves = norm_rope(nq, kg_ref[...])
    for j, s in enumerate(halves):
        for e in range(LANES // HEAD_DIM):
            k_ref[j * (LANES // HEAD_DIM) + e] = s[:, e * HEAD_DIM:(e + 1) * HEAD_DIM].astype(BF16)
    vv = _dot(h, w_ref[:, nq + nkv:nq + 2 * nkv])
    ones = jnp.ones((vv.shape[0], HEAD_DIM), F32)
    for e in range(N_KV_HEADS):
        v_ref[e] = jnp.concatenate([vv[:, e * HEAD_DIM:(e + 1) * HEAD_DIM], ones], axis=-1).astype(BF16)


def _qkv(x, mod, norm_g, w, qg, kg, bd, cos, s_up, s_dn, *, ctx):
    b, l, d = x.shape
    bm = min(l, 512)
    tab = pl.BlockSpec((bm, LANES), lambda i, t: (t, 0))
    kern = functools.partial(_qkv_kernel, rope=not ctx)
    return pl.pallas_call(
        kern,
        grid=(b, l // bm),
        in_specs=[
            pl.BlockSpec((None, bm, d), lambda i, t: (i, t, 0)),
            _mod_spec(ctx),
            _resident((1, d)),
            _resident(w.shape),
            _resident((1, LANES)),
            _resident((1, LANES)),
            _resident(bd.shape),
            tab, tab, tab,
        ],
        out_specs=[
            pl.BlockSpec((None, bm, d), lambda i, t: (i, t, 0)),
            pl.BlockSpec((None, N_KV_HEADS, bm, HEAD_DIM), lambda i, t: (i, 0, t, 0)),
            pl.BlockSpec((None, N_KV_HEADS, bm, 2 * HEAD_DIM), lambda i, t: (i, 0, t, 0)),
        ],
        out_shape=[
            jax.ShapeDtypeStruct((b, l, d), BF16),
            jax.ShapeDtypeStruct((b, N_KV_HEADS, l, HEAD_DIM), BF16),
            jax.ShapeDtypeStruct((b, N_KV_HEADS, l, 2 * HEAD_DIM), BF16),
        ],
        compiler_params=_params(2),
        name="qkv",
    )(x, mod, norm_g, w, qg, kg, bd, cos, s_up, s_dn)


def _attn_kernel(*refs, n_sets):
    q_ref = refs[0]
    kv = refs[1:1 + 2 * n_sets]
    o_ref = refs[1 + 2 * n_sets]
    q = q_ref[...]
    ks = [kv[2 * i][...] for i in range(n_sets)]
    vs = [kv[2 * i + 1][...] for i in range(n_sets)]
    outs = []

    def scores(gi):
        qg = q[:, gi * HEAD_DIM:(gi + 1) * HEAD_DIM]
        return [_dot_nt(qg, k) for k in ks]

    ss_next = scores(0)
    for gi in range(GQA_GROUP):
        ss = ss_next
        if gi + 1 < GQA_GROUP:
            ss_next = scores(gi + 1)
        m = ss[0].max(axis=-1, keepdims=True)
        for s in ss[1:]:
            m = jnp.maximum(m, s.max(axis=-1, keepdims=True))
        acc = None
        for s, v in zip(ss, vs):
            pv = _dot(jnp.exp2(s - m).astype(BF16), v)
            acc = pv if acc is None else acc + pv
        o = acc * (1.0 / pltpu.roll(acc, HEAD_DIM, axis=1))
        outs.append(o[:, 0:HEAD_DIM])
    o_ref[...] = jnp.concatenate(outs, axis=-1).astype(BF16)


def _attention(q, kvs):
    b, l, d = q.shape
    bq = min(l, 512)
    width = GQA_GROUP * HEAD_DIM
    in_specs = [pl.BlockSpec((None, bq, width), lambda i, h, t: (i, t, h))]
    args = [q]
    for k, v in kvs:
        lk = k.shape[2]
        in_specs += [pl.BlockSpec((None, None, lk, HEAD_DIM), lambda i, h, t: (i, h, 0, 0)),
                     pl.BlockSpec((None, None, lk, 2 * HEAD_DIM), lambda i, h, t: (i, h, 0, 0))]
        args += [k, v]
    return pl.pallas_call(
        functools.partial(_attn_kernel, n_sets=len(kvs)),
        grid=(b, N_KV_HEADS, l // bq),
        in_specs=in_specs,
        out_specs=pl.BlockSpec((None, bq, width), lambda i, h, t: (i, t, h)),
        out_shape=jax.ShapeDtypeStruct((b, l, d), BF16),
        compiler_params=_params(3),
        name="attention",
    )(*args)


def _rope256(v, cos, sgn_sin):
    return v * cos + pltpu.roll(v, LANES // 2, axis=1) * sgn_sin


def _retproj_kernel(x_ref, mod_ref, g_ref, w_ref, cos_ref, sin_ref, *out_refs, ctx):
    h = _modulate(x_ref[...], g_ref[...], mod_ref[0], mod_ref[1]).astype(BF16)
    nqk = RET_HEADS * RET_HEAD_DIM
    nv = RET_HEADS * RET_V_DIM
    kscale = RET_HEAD_DIM ** -0.5
    if ctx:
        k_ref, v_ref = out_refs
        col = 0
    else:
        q_ref, k_ref, v_ref, gt_ref = out_refs
        col = nqk
        for ci in range(nqk // MXU_COLS):
            u = _dot(h, w_ref[:, ci * MXU_COLS:(ci + 1) * MXU_COLS])
            for j in range(MXU_COLS // LANES):
                sl = slice(j * LANES, (j + 1) * LANES)
                lo = ci * MXU_COLS + j * LANES
                q_ref[:, lo:lo + LANES] = _rope256(u[:, sl], cos_ref[:, sl], sin_ref[:, sl]).astype(BF16)
    for ci in range(nqk // MXU_COLS):
        u = _dot(h, w_ref[:, col + ci * MXU_COLS:col + (ci + 1) * MXU_COLS]) * kscale
        for j in range(MXU_COLS // LANES):
            sl = slice(j * LANES, (j + 1) * LANES)
            lo = ci * MXU_COLS + j * LANES
            s = u[:, sl]
            if not ctx:
                s = _rope256(s, cos_ref[:, sl], sin_ref[:, sl])
            k_ref[:, lo:lo + LANES] = s.astype(BF16)
    col += nqk
    for ci in range(nv // MXU_COLS):
        sl = slice(ci * MXU_COLS, (ci + 1) * MXU_COLS)
        v_ref[:, sl] = _dot(h, w_ref[:, col + ci * MXU_COLS:col + (ci + 1) * MXU_COLS]).astype(BF16)
    if not ctx:
        col += nv
        for ci in range(nv // MXU_COLS):
            sl = slice(ci * MXU_COLS, (ci + 1) * MXU_COLS)
            gt_ref[:, sl] = _dot(h, w_ref[:, col + ci * MXU_COLS:col + (ci + 1) * MXU_COLS]).astype(BF16)


def _retproj(x, mod, norm_g, w, cos, sin, *, ctx):
    b, l, d = x.shape
    bm = min(l, 512)
    nqk = RET_HEADS * RET_HEAD_DIM
    nv = RET_HEADS * RET_V_DIM
    row = lambda n: pl.BlockSpec((None, bm, n), lambda i, t: (i, t, 0))
    tab = pl.BlockSpec((bm, RET_HEAD_DIM), lambda i, t: (t, 0))
    widths = [nqk, nv] if ctx else [nqk, nqk, nv, nv]
    return pl.pallas_call(
        functools.partial(_retproj_kernel, ctx=ctx),
        grid=(b, l // bm),
        in_specs=[row(d), _mod_spec(ctx), _resident((1, d)), _resident(w.shape), tab, tab],
        out_specs=[row(n) for n in widths],
        out_shape=[jax.ShapeDtypeStruct((b, l, n), BF16) for n in widths],
        compiler_params=_params(2),
        name="retproj",
    )(x, mod, norm_g, w, cos, sin)


def _retention_kernel(dec_ref, q_ref, k_ref, v_ref, gt_ref, kc_ref, vc_ref, o_ref,
                      y_ref, *, seq, ctx_len):
    hd = pl.program_id(1)
    c = RET_BLOCK
    n_chunks = seq // c

    def log_gamma(i):
        dv = jnp.full((1, 1), dec_ref[i], F32)
        return jnp.log1p(-jnp.exp2(-dv))

    lf = log_gamma(hd)
    lb = log_gamma(RET_HEADS + hd)
    ii = lax.broadcasted_iota(jnp.int32, (c, c), 0)
    jj = lax.broadcasted_iota(jnp.int32, (c, c), 1)
    diff = (ii - jj).astype(F32)
    decay = jnp.where(diff >= 0, jnp.exp(diff * lf), jnp.exp(-diff * lb))
    pos = lax.broadcasted_iota(jnp.int32, (c, 1), 0).astype(F32)
    qd_f = jnp.exp((pos + 1.0) * lf)
    kd_f = jnp.exp((c - 1.0 - pos) * lf)
    qd_b = jnp.exp((c - pos) * lb)
    kd_b = jnp.exp(pos * lb)
    cd_f = jnp.exp(c * lf)
    cd_b = jnp.exp(c * lb)

    def outer(kd, v):
        return _dot(kd.T.astype(BF16), v)

    posc = lax.broadcasted_iota(jnp.int32, (ctx_len, 1), 0).astype(F32)
    kcf = kc_ref[...].astype(F32)
    vc = vc_ref[...]
    sf = outer(kcf * jnp.exp((ctx_len - 1.0 - posc) * lf), vc)
    sb = outer(kcf * jnp.exp(posc * lb), vc)

    def rows(i):
        return slice(i * c, (i + 1) * c)

    def intra(i):
        a = (_dot_nt(q_ref[rows(i), :], k_ref[rows(i), :]) * decay).astype(BF16)
        return _dot(a, v_ref[rows(i), :])

    def finish(i, y):
        ms = jnp.mean(y * y, axis=-1, keepdims=True)
        yn = y * lax.rsqrt(ms + NORM_EPS)
        o_ref[rows(i), :] = (_silu(gt_ref[rows(i), :].astype(F32)) * yn).astype(BF16)

    for step in range(n_chunks):
        i = step
        j = n_chunks - 1 - step
        inter_f = _dot(q_ref[rows(i), :], sf.astype(BF16)) * qd_f
        inter_b = _dot(q_ref[rows(j), :], sb.astype(BF16)) * qd_b
        if i < j:
            y_ref[rows(i), :] = intra(i) + inter_f
            y_ref[rows(j), :] = intra(j) + inter_b
        else:
            finish(i, y_ref[rows(i), :] + inter_f)
            finish(j, y_ref[rows(j), :] + inter_b)
        if step < n_chunks - 1:
            sf = cd_f * sf + outer(k_ref[rows(i), :].astype(F32) * kd_f, v_ref[rows(i), :])
            sb = cd_b * sb + outer(k_ref[rows(j), :].astype(F32) * kd_b, v_ref[rows(j), :])


def _retention(dec, q, k, v, gt, kc, vc):
    b, l, _ = q.shape
    lc = kc.shape[1]
    qk = lambda n: pl.BlockSpec((None, n, RET_HEAD_DIM), lambda i, h: (i, 0, h))
    vv = lambda n: pl.BlockSpec((None, n, RET_V_DIM), lambda i, h: (i, 0, h))
    return pl.pallas_call(
        functools.partial(_retention_kernel, seq=l, ctx_len=lc),
        grid=(b, RET_HEADS),
        in_specs=[pl.BlockSpec(memory_space=pltpu.SMEM),
                  qk(l), qk(l), vv(l), vv(l), qk(lc), vv(lc)],
        out_specs=vv(l),
        out_shape=jax.ShapeDtypeStruct((b, l, RET_HEADS * RET_V_DIM), BF16),
        scratch_shapes=[pltpu.VMEM((l, RET_V_DIM), F32)],
        compiler_params=_params(2),
        name="retention",
    )(dec, q, k, v, gt, kc, vc)


def _rope_tables(seq, head_dim):
    quarter = head_dim // 4
    rows = jnp.repeat(jnp.arange(seq // GRID_W, dtype=F32), GRID_W)
    cols = jnp.tile(jnp.arange(GRID_W, dtype=F32), seq // GRID_W)
    inv = ROPE_THETA ** (-jnp.arange(quarter, dtype=F32) / quarter)
    ang = jnp.stack([rows[:, None] * inv, cols[:, None] * inv], axis=1)
    cos = jnp.cos(ang)
    sin = jnp.sin(ang)
    cos_t = jnp.stack([cos, cos], axis=2).reshape(seq, head_dim)
    sin_first = jnp.stack([-sin, jnp.zeros_like(sin)], axis=2).reshape(seq, head_dim)
    sin_second = jnp.stack([jnp.zeros_like(sin), sin], axis=2).reshape(seq, head_dim)
    return cos_t, sin_first, sin_second


def kernel(x, c, ctx, c_ctx, ada_w, ada_b, norm_mix_g, norm_ffn_g, final_norm_g, conv_w_in, conv_k, conv_w_out, attn_w_qkv, attn_q_norm_g, attn_k_norm_g, attn_w_out, ret_w_in, ret_decay, ret_w_out, ffn_w_up, ffn_conv_k, ffn_conv_b, ffn_w_down):
    batch, seq, d = x.shape
    assert d == D_MODEL and seq % (2 * RET_BLOCK) == 0 and ctx.shape[1] % SUBLANES == 0

    cvec = jnp.zeros((MOD_ROWS, d), F32).at[:batch].set(c).at[CTX_ROW].set(c_ctx)
    mod_all = _ada(cvec, ada_w, ada_b).reshape(DEPTH, MOD_ROWS, 6, 1, d)

    a_cos, a_first, a_second = _rope_tables(seq, HEAD_DIM)
    rep = LANES // HEAD_DIM
    a_cos, a_first, a_second = (jnp.tile(t, (1, rep)) for t in (a_cos, a_first, a_second))
    r_cos, r_first, r_second = _rope_tables(seq, RET_HEAD_DIM)
    r_sin = r_first + r_second
    head_sum = jnp.kron(jnp.eye(MXU_COLS // HEAD_DIM, dtype=F32),
                        jnp.full((HEAD_DIM, HEAD_DIM), 1.0 / HEAD_DIM, F32)).astype(BF16)

    kinds = [i % N_MIXERS for i in range(DEPTH)]
    reads_ctx = [kd in (1, 2) for kd in kinds]
    row1 = lambda v: v.reshape(1, -1)
    cx = ctx
    for i in range(DEPTH):
        kind = kinds[i]
        j = i // N_MIXERS
        ctx_out = any(reads_ctx[i + 1:])
        mod = mod_all[i]
        ng = row1(norm_mix_g[i])
        last = i == DEPTH - 1
        if kind == 0:
            w1 = conv_w_in[j].astype(BF16)
            w2 = conv_w_out[j].astype(BF16)
            nb = jnp.zeros((1, d), F32)
            mix = functools.partial(_convmlp, norm_g=ng, w1=w1, ck=conv_k[j], cb=nb, w2=w2,
                                    final_g=row1(final_norm_g), kind="mix")
            x = mix(x, mod, ctx=False)
            if ctx_out:
                cx = mix(cx, mod, ctx=True)
        elif kind == 1:
            wq = attn_w_qkv[j].astype(BF16)
            wo = attn_w_out[j].astype(BF16)
            qg = jnp.tile(row1(attn_q_norm_g[j]), (1, rep))
            kg = jnp.tile(row1(attn_k_norm_g[j]), (1, rep))
            proj = functools.partial(_qkv, norm_g=ng, w=wq, qg=qg, kg=kg, bd=head_sum,
                                     cos=a_cos, s_up=a_first, s_dn=a_second)
            q, k, v = proj(x, mod, ctx=False)
            qc, kc, vc = proj(cx, mod, ctx=True)
            o = _attention(q, [(k, v), (kc, vc)])
            x = _outproj(o, x, mod, wo, ctx=False)
            if ctx_out:
                oc = _attention(qc, [(kc, vc)])
                cx = _outproj(oc, cx, mod, wo, ctx=True)
        else:
            wi = ret_w_in[j].astype(BF16)
            wo = ret_w_out[j].astype(BF16)
            nqk = RET_HEADS * RET_HEAD_DIM
            nv = RET_HEADS * RET_V_DIM
            q, k, v, gt = _retproj(x, mod, ng, wi, r_cos, r_sin, ctx=False)
            kc, vc = _retproj(cx, mod, ng, wi[:, nqk:2 * nqk + nv], r_cos, r_sin, ctx=True)
            a = _retention(ret_decay[j].reshape(-1), q, k, v, gt, kc, vc)
            x = _outproj(a, x, mod, wo, ctx=False)
            assert not ctx_out
        ffn = functools.partial(_convmlp, norm_g=row1(norm_ffn_g[i]), w1=ffn_w_up[i].astype(BF16),
                                ck=ffn_conv_k[i], cb=row1(ffn_conv_b[i]),
                                w2=ffn_w_down[i].astype(BF16), final_g=row1(final_norm_g), kind="ffn")
        x = ffn(x, mod, ctx=False, final_norm=last)
        if ctx_out:
            cx = ffn(cx, mod, ctx=True)
    return x
```

```python
import functools
from typing import NamedTuple

import jax
import jax.numpy as jnp
from jax import lax
from jax.experimental import pallas as pl
from jax.experimental.pallas import tpu as pltpu

F32 = jnp.float32
BF16 = jnp.bfloat16

D_MODEL = 1024
DEPTH = 4
GRID_W = 64
N_MIXERS = 3
HEAD_DIM = 64
N_Q_HEADS = D_MODEL // HEAD_DIM
N_KV_HEADS = N_Q_HEADS // 4
GQA_GROUP = N_Q_HEADS // N_KV_HEADS
ROPE_THETA = 10000.0
RET_HEAD_DIM = 256
RET_HEADS = D_MODEL // RET_HEAD_DIM
RET_V_DIM = 2 * RET_HEAD_DIM
RET_BLOCK = 256
D_FF = ((8 * D_MODEL // 3 + 127) // 128) * 128
NORM_EPS = 1e-6
LOG2_E = 1.4426950408889634

LANES = 128
SUBLANES = 8
BF16_ROWS = 16
MXU_COLS = 256
VMEM_LIMIT = 56 * 1024 * 1024

N_U_SLOTS = 11
MOD_ROWS = 16
CTX_ROW = 8


def _params(n_axes):
    return pltpu.CompilerParams(dimension_semantics=("arbitrary",) * n_axes,
                                vmem_limit_bytes=VMEM_LIMIT)


class _Layer(NamedTuple):
    stack: jax.Array
    index: int

    @property
    def shape(self):
        return self.stack.shape[1:]


def _resident(p):
    zeros = (0,) * len(p.shape)
    return pl.BlockSpec((None,) + p.shape, lambda *_: (p.index,) + zeros,
                        pipeline_mode=pl.Buffered(1))


def _silu(v):
    return v * (1.0 / (1.0 + jnp.exp(-v)))


def _modulate(xf, g, shift, scale):
    ms = jnp.mean(xf * xf, axis=-1, keepdims=True)
    y = xf * lax.rsqrt(ms + NORM_EPS) * g
    return y * (1.0 + scale) + shift


def _dot(a, b):
    return jnp.dot(a, b, preferred_element_type=F32)


def _dot_nt(a, b):
    return lax.dot_general(a, b, (((1,), (1,)), ((), ())), preferred_element_type=F32)


def _ada_kernel(c_ref, w_ref, b_ref, o_ref):
    s = _silu(c_ref[...]).astype(BF16)
    o_ref[...] = _dot(s, w_ref[...].astype(BF16)) + b_ref[...]


def _ada(cvec, ada_w, ada_b):
    depth, d, n = ada_w.shape
    tn = 1536
    return pl.pallas_call(
        _ada_kernel,
        grid=(depth, n // tn),
        in_specs=[
            pl.BlockSpec((MOD_ROWS, d), lambda l, j: (0, 0)),
            pl.BlockSpec((None, d, tn), lambda l, j: (l, 0, j)),
            pl.BlockSpec((None, 1, tn), lambda l, j: (l, 0, j)),
        ],
        out_specs=pl.BlockSpec((None, MOD_ROWS, tn), lambda l, j: (l, 0, j)),
        out_shape=jax.ShapeDtypeStruct((depth, MOD_ROWS, n), F32),
        compiler_params=_params(2),
        name="ada",
    )(cvec, ada_w, ada_b.reshape(depth, 1, n))


def _mod_spec(mod, ctx):
    blk = (None, None, 6, 1, D_MODEL)
    if ctx:
        return pl.BlockSpec(blk, lambda b, t: (mod.index, CTX_ROW, 0, 0, 0))
    return pl.BlockSpec(blk, lambda b, t: (mod.index, b, 0, 0, 0))


def _convmlp_kernel(x_ref, xp_ref, xn_ref, mod_ref, g_ref, w1_ref, ck_ref, cb_ref, w2_ref, fg_ref,
                    o_ref, h_ref, u_ref, z_ref, *, kind, bm, mod_base, final_norm):
    t = pl.program_id(1)
    nt = pl.num_programs(1)
    shift = mod_ref[mod_base]
    scale = mod_ref[mod_base + 1]
    gate = mod_ref[mod_base + 2]
    g = g_ref[...]
    x = x_ref[...]
    h_ref[0:bm, :] = _modulate(x, g, shift, scale).astype(BF16)
    keep_n = (t < nt - 1).astype(F32)
    keep_p = (t > 0).astype(F32)
    halo = jnp.concatenate([_modulate(xn_ref[...], g, shift, scale) * keep_n,
                            _modulate(xp_ref[...], g, shift, scale) * keep_p], axis=0)
    h_ref[bm:bm + BF16_ROWS, :] = halo.astype(BF16)
    h = h_ref[...]
    n_mid = D_FF if kind == "ffn" else D_MODEL
    n_chunks = n_mid // MXU_COLS
    top = SUBLANES

    def rows_of(s, pos, n):
        return pl.ds(s + 2 * pos, n, stride=2)

    def put(slot, s, u):
        for j in range(MXU_COLS // LANES):
            uj = u[:, j * LANES:(j + 1) * LANES]
            u_ref[slot, j, rows_of(s, 0, top), :] = uj[bm + top:bm + 2 * top]
            u_ref[slot, j, rows_of(s, top, bm), :] = uj[0:bm]
            u_ref[slot, j, rows_of(s, top + bm, top), :] = uj[bm:bm + top]

    def conv3(slot, j, s, c0):
        k0 = ck_ref[0:1, c0:c0 + LANES]
        k1 = ck_ref[1:2, c0:c0 + LANES]
        k2 = ck_ref[2:3, c0:c0 + LANES]
        return (u_ref[slot, j, rows_of(s, top - 1, bm), :] * k0
                + u_ref[slot, j, rows_of(s, top, bm), :] * k1
                + u_ref[slot, j, rows_of(s, top + 1, bm), :] * k2)

    def up(ci):
        slot = ci % N_U_SLOTS
        cols = lambda s: slice(s * n_mid + ci * MXU_COLS, s * n_mid + (ci + 1) * MXU_COLS)
        if kind == "ffn":
            put(slot, 0, _dot(h, w1_ref[:, cols(0)]))
            put(slot, 1, _dot(h, w1_ref[:, cols(1)]))
        else:
            put(slot, 0, _dot(h, w1_ref[:, cols(0)]))
            put(slot, 1, _dot(h, w1_ref[:, cols(1)]) * _dot(h, w1_ref[:, cols(2)]))

    def mid(ci):
        slot = ci % N_U_SLOTS
        for j in range(MXU_COLS // LANES):
            c0 = ci * MXU_COLS + j * LANES
            if kind == "ffn":
                cv = conv3(slot, j, 0, c0) + cb_ref[:, c0:c0 + LANES]
                cg = conv3(slot, j, 1, D_FF + c0) + cb_ref[:, D_FF + c0:D_FF + c0 + LANES]
                z = _silu(cg) * cv
            else:
                z = u_ref[slot, j, rows_of(0, top, bm), :] * conv3(slot, j, 1, c0)
            z_ref[:, c0:c0 + LANES] = z.astype(BF16)

    up(0)
    for ci in range(n_chunks):
        if ci + 1 < n_chunks:
            up(ci + 1)
        mid(ci)
    out = x + gate * _dot(z_ref[...], w2_ref[...])
    if final_norm:
        ms = jnp.mean(out * out, axis=-1, keepdims=True)
        out = out * lax.rsqrt(ms + NORM_EPS) * fg_ref[...]
    o_ref[...] = out


def _convmlp(x, mod, norm_g, w1, ck, cb, w2, final_g, *, kind, ctx, final_norm=False):
    b, l, d = x.shape
    bm = min(l, 512)
    nt = l // bm
    hb = bm // SUBLANES
    last_hb = l // SUBLANES - 1
    kern = functools.partial(_convmlp_kernel, kind=kind, bm=bm,
                             mod_base=3 if kind == "ffn" else 0, final_norm=final_norm)
    return pl.pallas_call(
        kern,
        grid=(b, nt),
        in_specs=[
            pl.BlockSpec((None, bm, d), lambda i, t: (i, t, 0)),
            pl.BlockSpec((None, SUBLANES, d), lambda i, t: (i, jnp.maximum(t * hb - 1, 0), 0)),
            pl.BlockSpec((None, SUBLANES, d), lambda i, t: (i, jnp.minimum((t + 1) * hb, last_hb), 0)),
            _mod_spec(mod, ctx),
            _resident(norm_g),
            _resident(w1),
            _resident(ck),
            _resident(cb),
            _resident(w2),
            _resident(final_g),
        ],
        out_specs=pl.BlockSpec((None, bm, d), lambda i, t: (i, t, 0)),
        out_shape=jax.ShapeDtypeStruct((b, l, d), F32),
        scratch_shapes=[pltpu.VMEM((bm + BF16_ROWS, d), BF16),
                        pltpu.VMEM((N_U_SLOTS, MXU_COLS // LANES, 2 * (bm + 2 * SUBLANES), LANES), F32),
                        pltpu.VMEM((bm, w2.shape[0]), BF16)],
        compiler_params=_params(2),
        name="convmlp_" + kind,
    )(x, x, x, mod.stack, norm_g.stack, w1.stack, ck.stack, cb.stack, w2.stack, final_g.stack)


def _outproj_kernel(a_ref, x_ref, mod_ref, w_ref, o_ref):
    o_ref[...] = x_ref[...] + mod_ref[2] * _dot(a_ref[...], w_ref[...])


def _outproj(a, x, mod, w, *, ctx):
    b, l, d = x.shape
    k = a.shape[-1]
    bm = min(l, 512)
    return pl.pallas_call(
        _outproj_kernel,
        grid=(b, l // bm),
        in_specs=[
            pl.BlockSpec((None, bm, k), lambda i, t: (i, t, 0)),
            pl.BlockSpec((None, bm, d), lambda i, t: (i, t, 0)),
            _mod_spec(mod, ctx),
            _resident(w),
        ],
        out_specs=pl.BlockSpec((None, bm, d), lambda i, t: (i, t, 0)),
        out_shape=jax.ShapeDtypeStruct((b, l, d), F32),
        input_output_aliases={1: 0},
        compiler_params=_params(2),
        name="outproj",
    )(a, x, mod.stack, w.stack)


def _head_meansq(v, bd_ref):
    return _dot((v * v).astype(BF16), bd_ref[...])


def _rope64(v, cos, s_up, s_dn):
    return v * cos + pltpu.roll(v, LANES - 16, axis=1) * s_up + pltpu.roll(v, 16, axis=1) * s_dn


def _qkv_kernel(x_ref, mod_ref, g_ref, w_ref, qg_ref, kg_ref, bd_ref, cos_ref, sup_ref, sdn_ref,
                q_ref, k_ref, v_ref, *, rope):
    h = _modulate(x_ref[...], g_ref[...], mod_ref[0], mod_ref[1]).astype(BF16)
    nq = N_Q_HEADS * HEAD_DIM
    nkv = N_KV_HEADS * HEAD_DIM
    n_norm = (nq + nkv) // MXU_COLS
    q_gain = qg_ref[...] * (HEAD_DIM ** -0.5 * LOG2_E)
    k_gain = kg_ref[...]
    heads_per_slab = LANES // HEAD_DIM

    def project(ci):
        return _dot(h, w_ref[:, ci * MXU_COLS:(ci + 1) * MXU_COLS])

    def finish(ci, u, ms):
        u = u * lax.rsqrt(ms + NORM_EPS)
        for j in range(MXU_COLS // LANES):
            s = u[:, j * LANES:(j + 1) * LANES] * (q_gain if ci < n_norm - 1 else k_gain)
            if rope:
                s = _rope64(s, cos_ref[...], sup_ref[...], sdn_ref[...])
            if ci < n_norm - 1:
                lo = ci * MXU_COLS + j * LANES
                q_ref[:, lo:lo + LANES] = s.astype(BF16)
            else:
                for e in range(heads_per_slab):
                    k_ref[j * heads_per_slab + e] = s[:, e * HEAD_DIM:(e + 1) * HEAD_DIM].astype(BF16)

    us = {0: project(0), 1: project(1)}
    mss = {0: _head_meansq(us[0], bd_ref)}
    for ci in range(n_norm):
        if ci + 2 <= n_norm:
            us[ci + 2] = project(ci + 2)
        if ci + 1 < n_norm:
            mss[ci + 1] = _head_meansq(us[ci + 1], bd_ref)
        finish(ci, us.pop(ci), mss.pop(ci))
    vv = us.pop(n_norm)
    ones = jnp.ones((vv.shape[0], HEAD_DIM), F32)
    for e in range(N_KV_HEADS):
        v_ref[e] = jnp.concatenate([vv[:, e * HEAD_DIM:(e + 1) * HEAD_DIM], ones], axis=-1).astype(BF16)


def _qkv(x, mod, norm_g, w, qg, kg, bd, cos, s_up, s_dn, *, ctx):
    b, l, d = x.shape
    bm = min(l, 512)
    tab = pl.BlockSpec((bm, LANES), lambda i, t: (t, 0))
    kern = functools.partial(_qkv_kernel, rope=not ctx)
    return pl.pallas_call(
        kern,
        grid=(b, l // bm),
        in_specs=[
            pl.BlockSpec((None, bm, d), lambda i, t: (i, t, 0)),
            _mod_spec(mod, ctx),
            _resident(norm_g),
            _resident(w),
            _resident(qg),
            _resident(kg),
            _resident(bd),
            tab, tab, tab,
        ],
        out_specs=[
            pl.BlockSpec((None, bm, d), lambda i, t: (i, t, 0)),
            pl.BlockSpec((None, N_KV_HEADS, bm, HEAD_DIM), lambda i, t: (i, 0, t, 0)),
            pl.BlockSpec((None, N_KV_HEADS, bm, 2 * HEAD_DIM), lambda i, t: (i, 0, t, 0)),
        ],
        out_shape=[
            jax.ShapeDtypeStruct((b, l, d), BF16),
            jax.ShapeDtypeStruct((b, N_KV_HEADS, l, HEAD_DIM), BF16),
            jax.ShapeDtypeStruct((b, N_KV_HEADS, l, 2 * HEAD_DIM), BF16),
        ],
        compiler_params=_params(2),
        name="qkv",
    )(x, mod.stack, norm_g.stack, w.stack, qg.stack, kg.stack, bd.stack, cos, s_up, s_dn)


def _attn_kernel(*refs, n_sets):
    q_ref = refs[0]
    kv = refs[1:1 + 2 * n_sets]
    o_ref = refs[1 + 2 * n_sets]
    q = q_ref[...]
    ks = [kv[2 * i][...] for i in range(n_sets)]
    vs = [kv[2 * i + 1][...] for i in range(n_sets)]
    outs = []

    def scores(gi):
        qg = q[:, gi * HEAD_DIM:(gi + 1) * HEAD_DIM]
        return [_dot_nt(qg, k) for k in ks]

    ss_next = scores(0)
    for gi in range(GQA_GROUP):
        ss = ss_next
        if gi + 1 < GQA_GROUP:
            ss_next = scores(gi + 1)
        m = ss[0].max(axis=-1, keepdims=True)
        for s in ss[1:]:
            m = jnp.maximum(m, s.max(axis=-1, keepdims=True))
        acc = None
        for s, v in zip(ss, vs):
            pv = _dot(jnp.exp2(s - m).astype(BF16), v)
            acc = pv if acc is None else acc + pv
        o = acc * (1.0 / pltpu.roll(acc, HEAD_DIM, axis=1))
        outs.append(o[:, 0:HEAD_DIM])
    o_ref[...] = jnp.concatenate(outs, axis=-1).astype(BF16)


def _attention(q, kvs):
    b, l, d = q.shape
    bq = min(l, 512)
    width = GQA_GROUP * HEAD_DIM
    in_specs = [pl.BlockSpec((None, bq, width), lambda i, h, t: (i, t, h))]
    args = [q]
    for k, v in kvs:
        lk = k.shape[2]
        in_specs += [pl.BlockSpec((None, None, lk, HEAD_DIM), lambda i, h, t: (i, h, 0, 0)),
                     pl.BlockSpec((None, None, lk, 2 * HEAD_DIM), lambda i, h, t: (i, h, 0, 0))]
        args += [k, v]
    return pl.pallas_call(
        functools.partial(_attn_kernel, n_sets=len(kvs)),
        grid=(b, N_KV_HEADS, l // bq),
        in_specs=in_specs,
        out_specs=pl.BlockSpec((None, bq, width), lambda i, h, t: (i, t, h)),
        out_shape=jax.ShapeDtypeStruct((b, l, d), BF16),
        compiler_params=_params(3),
        name="attention",
    )(*args)


def _rope256(v, cos, sgn_sin):
    return v * cos + pltpu.roll(v, LANES // 2, axis=1) * sgn_sin


def _retproj_kernel(x_ref, mod_ref, g_ref, w_ref, cos_ref, sin_ref, *out_refs, ctx):
    h = _modulate(x_ref[...], g_ref[...], mod_ref[0], mod_ref[1]).astype(BF16)
    nqk = RET_HEADS * RET_HEAD_DIM
    nv = RET_HEADS * RET_V_DIM
    kscale = RET_HEAD_DIM ** -0.5
    col = nqk
    if ctx:
        k_ref, v_ref = out_refs
    else:
        q_ref, k_ref, v_ref, gt_ref = out_refs
        for ci in range(nqk // MXU_COLS):
            u = _dot(h, w_ref[:, ci * MXU_COLS:(ci + 1) * MXU_COLS])
            for j in range(MXU_COLS // LANES):
                sl = slice(j * LANES, (j + 1) * LANES)
                lo = ci * MXU_COLS + j * LANES
                q_ref[:, lo:lo + LANES] = _rope256(u[:, sl], cos_ref[:, sl], sin_ref[:, sl]).astype(BF16)
    for ci in range(nqk // MXU_COLS):
        u = _dot(h, w_ref[:, col + ci * MXU_COLS:col + (ci + 1) * MXU_COLS]) * kscale
        for j in range(MXU_COLS // LANES):
            sl = slice(j * LANES, (j + 1) * LANES)
            lo = ci * MXU_COLS + j * LANES
            s = u[:, sl]
            if not ctx:
                s = _rope256(s, cos_ref[:, sl], sin_ref[:, sl])
            k_ref[:, lo:lo + LANES] = s.astype(BF16)
    col += nqk
    for ci in range(nv // MXU_COLS):
        sl = slice(ci * MXU_COLS, (ci + 1) * MXU_COLS)
        v_ref[:, sl] = _dot(h, w_ref[:, col + ci * MXU_COLS:col + (ci + 1) * MXU_COLS]).astype(BF16)
    if not ctx:
        col += nv
        for ci in range(nv // MXU_COLS):
            sl = slice(ci * MXU_COLS, (ci + 1) * MXU_COLS)
            gt_ref[:, sl] = _dot(h, w_ref[:, col + ci * MXU_COLS:col + (ci + 1) * MXU_COLS]).astype(BF16)


def _retproj(x, mod, norm_g, w, cos, sin, *, ctx):
    b, l, d = x.shape
    bm = min(l, 512)
    nqk = RET_HEADS * RET_HEAD_DIM
    nv = RET_HEADS * RET_V_DIM
    row = lambda n: pl.BlockSpec((None, bm, n), lambda i, t: (i, t, 0))
    tab = pl.BlockSpec((bm, RET_HEAD_DIM), lambda i, t: (t, 0))
    widths = [nqk, nv] if ctx else [nqk, nqk, nv, nv]
    return pl.pallas_call(
        functools.partial(_retproj_kernel, ctx=ctx),
        grid=(b, l // bm),
        in_specs=[row(d), _mod_spec(mod, ctx), _resident(norm_g), _resident(w), tab, tab],
        out_specs=[row(n) for n in widths],
        out_shape=[jax.ShapeDtypeStruct((b, l, n), BF16) for n in widths],
        compiler_params=_params(2),
        name="retproj",
    )(x, mod.stack, norm_g.stack, w.stack, cos, sin)


def _retention_kernel(dec_ref, q_ref, k_ref, v_ref, gt_ref, kc_ref, vc_ref, o_ref,
                      y_ref, *, seq, ctx_len):
    hd = pl.program_id(1)
    c = RET_BLOCK
    n_chunks = seq // c

    def log_gamma(i):
        dv = jnp.full((1, 1), dec_ref[i], F32)
        return jnp.log1p(-jnp.exp2(-dv))

    lf = log_gamma(hd)
    lb = log_gamma(RET_HEADS + hd)
    ii = lax.broadcasted_iota(jnp.int32, (c, c), 0)
    jj = lax.broadcasted_iota(jnp.int32, (c, c), 1)
    diff = (ii - jj).astype(F32)
    decay = jnp.where(diff >= 0, jnp.exp(diff * lf), jnp.exp(-diff * lb))
    pos = lax.broadcasted_iota(jnp.int32, (c, 1), 0).astype(F32)
    qd_f = jnp.exp((pos + 1.0) * lf)
    kd_f = jnp.exp((c - 1.0 - pos) * lf)
    qd_b = jnp.exp((c - pos) * lb)
    kd_b = jnp.exp(pos * lb)
    cd_f = jnp.exp(c * lf)
    cd_b = jnp.exp(c * lb)

    def outer(kd, v):
        return _dot(kd.T.astype(BF16), v)

    posc = lax.broadcasted_iota(jnp.int32, (ctx_len, 1), 0).astype(F32)
    kcf = kc_ref[...].astype(F32)
    vc = vc_ref[...]
    sf = outer(kcf * jnp.exp((ctx_len - 1.0 - posc) * lf), vc)
    sb = outer(kcf * jnp.exp(posc * lb), vc)

    def rows(i):
        return slice(i * c, (i + 1) * c)

    def intra(i):
        a = (_dot_nt(q_ref[rows(i), :], k_ref[rows(i), :]) * decay).astype(BF16)
        return _dot(a, v_ref[rows(i), :])

    def finish(i, y):
        ms = jnp.mean(y * y, axis=-1, keepdims=True)
        yn = y * lax.rsqrt(ms + NORM_EPS)
        o_ref[rows(i), :] = (_silu(gt_ref[rows(i), :].astype(F32)) * yn).astype(BF16)

    for step in range(n_chunks):
        i = step
        j = n_chunks - 1 - step
        inter_f = _dot(q_ref[rows(i), :], sf.astype(BF16)) * qd_f
        inter_b = _dot(q_ref[rows(j), :], sb.astype(BF16)) * qd_b
        if i < j:
            y_ref[rows(i), :] = intra(i) + inter_f
            y_ref[rows(j), :] = intra(j) + inter_b
        else:
            finish(i, y_ref[rows(i), :] + inter_f)
            finish(j, y_ref[rows(j), :] + inter_b)
        if step < n_chunks - 1:
            sf = cd_f * sf + outer(k_ref[rows(i), :].astype(F32) * kd_f, v_ref[rows(i), :])
            sb = cd_b * sb + outer(k_ref[rows(j), :].astype(F32) * kd_b, v_ref[rows(j), :])


def _retention(dec, q, k, v, gt, kc, vc):
    b, l, _ = q.shape
    lc = kc.shape[1]
    qk = lambda n: pl.BlockSpec((None, n, RET_HEAD_DIM), lambda i, h: (i, 0, h))
    vv = lambda n: pl.BlockSpec((None, n, RET_V_DIM), lambda i, h: (i, 0, h))
    return pl.pallas_call(
        functools.partial(_retention_kernel, seq=l, ctx_len=lc),
        grid=(b, RET_HEADS),
        in_specs=[pl.BlockSpec(memory_space=pltpu.SMEM),
                  qk(l), qk(l), vv(l), vv(l), qk(lc), vv(lc)],
        out_specs=vv(l),
        out_shape=jax.ShapeDtypeStruct((b, l, RET_HEADS * RET_V_DIM), BF16),
        scratch_shapes=[pltpu.VMEM((l, RET_V_DIM), F32)],
        compiler_params=_params(2),
        name="retention",
    )(dec, q, k, v, gt, kc, vc)


def _rope_tables(seq, head_dim):
    quarter = head_dim // 4
    rows = jnp.repeat(jnp.arange(seq // GRID_W, dtype=F32), GRID_W)
    cols = jnp.tile(jnp.arange(GRID_W, dtype=F32), seq // GRID_W)
    inv = ROPE_THETA ** (-jnp.arange(quarter, dtype=F32) / quarter)
    ang = jnp.stack([rows[:, None] * inv, cols[:, None] * inv], axis=1)
    cos = jnp.cos(ang)
    sin = jnp.sin(ang)
    cos_t = jnp.stack([cos, cos], axis=2).reshape(seq, head_dim)
    sin_first = jnp.stack([-sin, jnp.zeros_like(sin)], axis=2).reshape(seq, head_dim)
    sin_second = jnp.stack([jnp.zeros_like(sin), sin], axis=2).reshape(seq, head_dim)
    return cos_t, sin_first, sin_second


def kernel(x, c, ctx, c_ctx, ada_w, ada_b, norm_mix_g, norm_ffn_g, final_norm_g, conv_w_in, conv_k, conv_w_out, attn_w_qkv, attn_q_norm_g, attn_k_norm_g, attn_w_out, ret_w_in, ret_decay, ret_w_out, ffn_w_up, ffn_conv_k, ffn_conv_b, ffn_w_down):
    batch, seq, d = x.shape
    assert d == D_MODEL and seq % (2 * RET_BLOCK) == 0 and ctx.shape[1] % SUBLANES == 0

    cvec = jnp.zeros((MOD_ROWS, d), F32).at[:batch].set(c).at[CTX_ROW].set(c_ctx)
    mod_all = _ada(cvec, ada_w, ada_b).reshape(DEPTH, MOD_ROWS, 6, 1, d)

    a_cos, a_first, a_second = _rope_tables(seq, HEAD_DIM)
    rep = LANES // HEAD_DIM
    a_cos, a_first, a_second = (jnp.tile(t, (1, rep)) for t in (a_cos, a_first, a_second))
    r_cos, r_first, r_second = _rope_tables(seq, RET_HEAD_DIM)
    r_sin = r_first + r_second
    head_sum = jnp.kron(jnp.eye(MXU_COLS // HEAD_DIM, dtype=F32),
                        jnp.full((HEAD_DIM, HEAD_DIM), 1.0 / HEAD_DIM, F32)).astype(BF16)[None]

    rows = lambda v: v.reshape(v.shape[0], 1, -1)
    norm_mix, norm_ffn = rows(norm_mix_g), rows(norm_ffn_g)
    final_g = _Layer(final_norm_g.reshape(1, 1, d), 0)
    mix_w1, mix_w2 = conv_w_in.astype(BF16), conv_w_out.astype(BF16)
    no_bias = _Layer(jnp.zeros((1, 1, d), F32), 0)
    att_wq, att_wo = attn_w_qkv.astype(BF16), attn_w_out.astype(BF16)
    att_qg = rows(jnp.tile(attn_q_norm_g, (1, rep)))
    att_kg = rows(jnp.tile(attn_k_norm_g, (1, rep)))
    ret_wi, ret_wo = ret_w_in.astype(BF16), ret_w_out.astype(BF16)
    ffn_w1, ffn_w2, ffn_cb = ffn_w_up.astype(BF16), ffn_w_down.astype(BF16), rows(ffn_conv_b)

    kinds = [i % N_MIXERS for i in range(DEPTH)]
    reads_ctx = [kd in (1, 2) for kd in kinds]
    cx = ctx
    for i in range(DEPTH):
        kind = kinds[i]
        j = i // N_MIXERS
        ctx_out = any(reads_ctx[i + 1:])
        mod = _Layer(mod_all, i)
        ng = _Layer(norm_mix, i)
        last = i == DEPTH - 1
        if kind == 0:
            mix = functools.partial(_convmlp, mod=mod, norm_g=ng, w1=_Layer(mix_w1, j), ck=_Layer(conv_k, j),
                                    cb=no_bias, w2=_Layer(mix_w2, j), final_g=final_g, kind="mix")
            x = mix(x, ctx=False)
            if ctx_out:
                cx = mix(cx, ctx=True)
        elif kind == 1:
            wo = _Layer(att_wo, j)
            proj = functools.partial(_qkv, mod=mod, norm_g=ng, w=_Layer(att_wq, j), qg=_Layer(att_qg, j),
                                     kg=_Layer(att_kg, j), bd=_Layer(head_sum, 0),
                                     cos=a_cos, s_up=a_first, s_dn=a_second)
            q, k, v = proj(x, ctx=False)
            qc, kc, vc = proj(cx, ctx=True)
            o = _attention(q, [(k, v), (kc, vc)])
            x = _outproj(o, x, mod, wo, ctx=False)
            if ctx_out:
                oc = _attention(qc, [(kc, vc)])
                cx = _outproj(oc, cx, mod, wo, ctx=True)
        else:
            wi = _Layer(ret_wi, j)
            q, k, v, gt = _retproj(x, mod, ng, wi, r_cos, r_sin, ctx=False)
            kc, vc = _retproj(cx, mod, ng, wi, r_cos, r_sin, ctx=True)
            a = _retention(ret_decay[j].reshape(-1), q, k, v, gt, kc, vc)
            x = _outproj(a, x, mod, _Layer(ret_wo, j), ctx=False)
            assert not ctx_out
        ffn = functools.partial(_convmlp, mod=mod, norm_g=_Layer(norm_ffn, i), w1=_Layer(ffn_w1, i),
                                ck=_Layer(ffn_conv_k, i), cb=_Layer(ffn_cb, i), w2=_Layer(ffn_w2, i),
                                final_g=final_g, kind="ffn")
        x = ffn(x, ctx=False, final_norm=last)
        if ctx_out:
            cx = ffn(cx, ctx=True)
    return x
```

```python
import functools
from typing import NamedTuple

import jax
import jax.numpy as jnp
from jax import lax
from jax.experimental import pallas as pl
from jax.experimental.pallas import tpu as pltpu

F32 = jnp.float32
BF16 = jnp.bfloat16

D_MODEL = 1024
DEPTH = 4
GRID_W = 64
N_MIXERS = 3
HEAD_DIM = 64
N_Q_HEADS = D_MODEL // HEAD_DIM
N_KV_HEADS = N_Q_HEADS // 4
GQA_GROUP = N_Q_HEADS // N_KV_HEADS
ROPE_THETA = 10000.0
RET_HEAD_DIM = 256
RET_HEADS = D_MODEL // RET_HEAD_DIM
RET_V_DIM = 2 * RET_HEAD_DIM
RET_BLOCK = 256
D_FF = ((8 * D_MODEL // 3 + 127) // 128) * 128
NORM_EPS = 1e-6
LOG2_E = 1.4426950408889634

LANES = 128
SUBLANES = 8
BF16_ROWS = 16
MXU_COLS = 256
VMEM_LIMIT = 56 * 1024 * 1024

N_U_SLOTS = 11
MOD_ROWS = 16
CTX_ROW = 8


def _params(n_axes):
    return pltpu.CompilerParams(dimension_semantics=("arbitrary",) * n_axes,
                                vmem_limit_bytes=VMEM_LIMIT)


class _Layer(NamedTuple):
    stack: jax.Array
    index: int

    @property
    def shape(self):
        return self.stack.shape[1:]


def _resident(p):
    zeros = (0,) * len(p.shape)
    return pl.BlockSpec((None,) + p.shape, lambda *_: (p.index,) + zeros,
                        pipeline_mode=pl.Buffered(1))


def _silu(v):
    return v * (1.0 / (1.0 + jnp.exp(-v)))


def _modulate(xf, g, shift, scale):
    ms = jnp.mean(xf * xf, axis=-1, keepdims=True)
    y = xf * lax.rsqrt(ms + NORM_EPS) * g
    return y * (1.0 + scale) + shift


def _dot(a, b):
    return jnp.dot(a, b, preferred_element_type=F32)


def _dot_nt(a, b):
    return lax.dot_general(a, b, (((1,), (1,)), ((), ())), preferred_element_type=F32)


def _ada_kernel(c_ref, w_ref, b_ref, o_ref):
    s = _silu(c_ref[...]).astype(BF16)
    o_ref[...] = _dot(s, w_ref[...].astype(BF16)) + b_ref[...]


def _ada(cvec, ada_w, ada_b):
    depth, d, n = ada_w.shape
    tn = 1536
    return pl.pallas_call(
        _ada_kernel,
        grid=(depth, n // tn),
        in_specs=[
            pl.BlockSpec((MOD_ROWS, d), lambda l, j: (0, 0)),
            pl.BlockSpec((None, d, tn), lambda l, j: (l, 0, j)),
            pl.BlockSpec((None, 1, tn), lambda l, j: (l, 0, j)),
        ],
        out_specs=pl.BlockSpec((None, MOD_ROWS, tn), lambda l, j: (l, 0, j)),
        out_shape=jax.ShapeDtypeStruct((depth, MOD_ROWS, n), F32),
        compiler_params=_params(2),
        name="ada",
    )(cvec, ada_w, ada_b.reshape(depth, 1, n))


def _mod_spec(mod, ctx):
    blk = (None, None, 6, 1, D_MODEL)
    if ctx:
        return pl.BlockSpec(blk, lambda b, t: (mod.index, CTX_ROW, 0, 0, 0))
    return pl.BlockSpec(blk, lambda b, t: (mod.index, b, 0, 0, 0))


def _convmlp_kernel(x_ref, xn_ref, xnn_ref, mod_ref, modn_ref, g_ref, w1_ref, ck_ref, cb_ref, w2_ref, fg_ref,
                    o_ref, h_ref, u_ref, z_ref, *, kind, bm, nt, mod_base, final_norm):
    step = pl.program_id(0)
    last_step = pl.num_programs(0) - 1
    g = g_ref[...]

    def build_h(tile, prev8, next8, mod, keep_p, keep_n):
        shift = mod[mod_base]
        scale = mod[mod_base + 1]
        h_ref[0:bm, :] = _modulate(tile, g, shift, scale).astype(BF16)
        halo = jnp.concatenate([_modulate(next8, g, shift, scale) * keep_n,
                                _modulate(prev8, g, shift, scale) * keep_p], axis=0)
        h_ref[bm:bm + BF16_ROWS, :] = halo.astype(BF16)

    n_mid = D_FF if kind == "ffn" else D_MODEL
    n_chunks = n_mid // MXU_COLS
    top = SUBLANES

    def rows_of(s, pos, n):
        return pl.ds(s + 2 * pos, n, stride=2)

    def put(slot, s, u):
        for j in range(MXU_COLS // LANES):
            uj = u[:, j * LANES:(j + 1) * LANES]
            u_ref[slot, j, rows_of(s, 0, top), :] = uj[bm + top:bm + 2 * top]
            u_ref[slot, j, rows_of(s, top, bm), :] = uj[0:bm]
            u_ref[slot, j, rows_of(s, top + bm, top), :] = uj[bm:bm + top]

    def conv3(slot, j, s, c0):
        k0 = ck_ref[0:1, c0:c0 + LANES]
        k1 = ck_ref[1:2, c0:c0 + LANES]
        k2 = ck_ref[2:3, c0:c0 + LANES]
        return (u_ref[slot, j, rows_of(s, top - 1, bm), :] * k0
                + u_ref[slot, j, rows_of(s, top, bm), :] * k1
                + u_ref[slot, j, rows_of(s, top + 1, bm), :] * k2)

    def up(ci):
        slot = ci % N_U_SLOTS
        cols = lambda s: slice(s * n_mid + ci * MXU_COLS, s * n_mid + (ci + 1) * MXU_COLS)
        h = h_ref[...]
        if kind == "ffn":
            put(slot, 0, _dot(h, w1_ref[:, cols(0)]))
            put(slot, 1, _dot(h, w1_ref[:, cols(1)]))
        else:
            put(slot, 0, _dot(h, w1_ref[:, cols(0)]))
            put(slot, 1, _dot(h, w1_ref[:, cols(1)]) * _dot(h, w1_ref[:, cols(2)]))

    def mid(ci):
        slot = ci % N_U_SLOTS
        for j in range(MXU_COLS // LANES):
            c0 = ci * MXU_COLS + j * LANES
            if kind == "ffn":
                cv = conv3(slot, j, 0, c0) + cb_ref[:, c0:c0 + LANES]
                cg = conv3(slot, j, 1, D_FF + c0) + cb_ref[:, D_FF + c0:D_FF + c0 + LANES]
                z = _silu(cg) * cv
            else:
                z = u_ref[slot, j, rows_of(0, top, bm), :] * conv3(slot, j, 1, c0)
            z_ref[:, c0:c0 + LANES] = z.astype(BF16)

    @pl.when(step == 0)
    def _():
        build_h(x_ref[...], x_ref[0:top, :], xn_ref[0:top, :], mod_ref, 0.0, 1.0 if nt > 1 else 0.0)
        up(0)

    for ci in range(n_chunks):
        if ci + 1 < n_chunks:
            up(ci + 1)
        mid(ci)
    acc = _dot(z_ref[...], w2_ref[...])

    tn = jnp.minimum(step + 1, last_step) % nt
    build_h(xn_ref[...], x_ref[bm - top:bm, :], xnn_ref[...], modn_ref,
            (tn > 0).astype(F32), (tn < nt - 1).astype(F32))
    up(0)

    out = x_ref[...] + mod_ref[mod_base + 2] * acc
    if final_norm:
        ms = jnp.mean(out * out, axis=-1, keepdims=True)
        out = out * lax.rsqrt(ms + NORM_EPS) * fg_ref[...]
    o_ref[...] = out


def _convmlp(x, mod, norm_g, w1, ck, cb, w2, final_g, *, kind, ctx, final_norm=False):
    b, l, d = x.shape
    bm = min(l, 512)
    nt = l // bm
    hb = bm // SUBLANES
    last = b * nt - 1
    kern = functools.partial(_convmlp_kernel, kind=kind, bm=bm, nt=nt,
                             mod_base=3 if kind == "ffn" else 0, final_norm=final_norm)
    ahead = lambda s, k: jnp.minimum(s + k, last)
    mod_spec = _mod_spec(mod, ctx)
    mod_at = lambda k: pl.BlockSpec(mod_spec.block_shape,
                                    lambda s: mod_spec.index_map(ahead(s, k) // nt, 0))
    return pl.pallas_call(
        kern,
        grid=(b * nt,),
        in_specs=[
            pl.BlockSpec((None, bm, d), lambda s: (s // nt, s % nt, 0)),
            pl.BlockSpec((None, bm, d), lambda s: (ahead(s, 1) // nt, ahead(s, 1) % nt, 0)),
            pl.BlockSpec((None, SUBLANES, d), lambda s: (ahead(s, 2) // nt, (ahead(s, 2) % nt) * hb, 0)),
            mod_at(0),
            mod_at(1),
            _resident(norm_g),
            _resident(w1),
            _resident(ck),
            _resident(cb),
            _resident(w2),
            _resident(final_g),
        ],
        out_specs=pl.BlockSpec((None, bm, d), lambda s: (s // nt, s % nt, 0)),
        out_shape=jax.ShapeDtypeStruct((b, l, d), F32),
        scratch_shapes=[pltpu.VMEM((bm + BF16_ROWS, d), BF16),
                        pltpu.VMEM((N_U_SLOTS, MXU_COLS // LANES, 2 * (bm + 2 * SUBLANES), LANES), F32),
                        pltpu.VMEM((bm, w2.shape[0]), BF16)],
        compiler_params=_params(1),
        name="convmlp_" + kind,
    )(x, x, x, mod.stack, mod.stack, norm_g.stack, w1.stack, ck.stack, cb.stack, w2.stack, final_g.stack)


def _outproj_kernel(*refs, gated):
    if not gated:
        a_ref, x_ref, mod_ref, w_ref, o_ref = refs
        acc = _dot(a_ref[...], w_ref[...])
    else:
        y_ref, gt_ref, x_ref, mod_ref, w_ref, o_ref = refs

        def gate(hd):
            cols = slice(hd * RET_V_DIM, (hd + 1) * RET_V_DIM)
            y = y_ref[:, cols].astype(F32)
            ms = jnp.mean(y * y, axis=-1, keepdims=True)
            return (_silu(gt_ref[:, cols].astype(F32)) * (y * lax.rsqrt(ms + NORM_EPS))).astype(BF16)

        a_next = gate(0)
        acc = None
        for hd in range(RET_HEADS):
            a = a_next
            if hd + 1 < RET_HEADS:
                a_next = gate(hd + 1)
            part = _dot(a, w_ref[hd * RET_V_DIM:(hd + 1) * RET_V_DIM, :])
            acc = part if acc is None else acc + part
    o_ref[...] = x_ref[...] + mod_ref[2] * acc


def _outproj(acts, x, mod, w, *, ctx):
    b, l, d = x.shape
    bm = min(l, 512)
    act_specs = [pl.BlockSpec((None, bm, a.shape[-1]), lambda i, t: (i, t, 0)) for a in acts]
    return pl.pallas_call(
        functools.partial(_outproj_kernel, gated=len(acts) == 2),
        grid=(b, l // bm),
        in_specs=act_specs + [
            pl.BlockSpec((None, bm, d), lambda i, t: (i, t, 0)),
            _mod_spec(mod, ctx),
            _resident(w),
        ],
        out_specs=pl.BlockSpec((None, bm, d), lambda i, t: (i, t, 0)),
        out_shape=jax.ShapeDtypeStruct((b, l, d), F32),
        input_output_aliases={len(acts): 0},
        compiler_params=_params(2),
        name="outproj",
    )(*acts, x, mod.stack, w.stack)


def _head_meansq(v, bd_ref):
    return _dot((v * v).astype(BF16), bd_ref[...])


def _rope64(v, cos, s_up, s_dn):
    return v * cos + pltpu.roll(v, LANES - 16, axis=1) * s_up + pltpu.roll(v, 16, axis=1) * s_dn


def _qkv_kernel(x_ref, mod_ref, g_ref, w_ref, qg_ref, kg_ref, bd_ref, cos_ref, sup_ref, sdn_ref,
                q_ref, k_ref, v_ref, *, rope):
    h = _modulate(x_ref[...], g_ref[...], mod_ref[0], mod_ref[1]).astype(BF16)
    nq = N_Q_HEADS * HEAD_DIM
    nkv = N_KV_HEADS * HEAD_DIM
    n_norm = (nq + nkv) // MXU_COLS
    q_gain = qg_ref[...] * (HEAD_DIM ** -0.5 * LOG2_E)
    k_gain = kg_ref[...]
    heads_per_slab = LANES // HEAD_DIM

    def project(ci):
        return _dot(h, w_ref[:, ci * MXU_COLS:(ci + 1) * MXU_COLS])

    def finish(ci, u, ms):
        u = u * lax.rsqrt(ms + NORM_EPS)
        for j in range(MXU_COLS // LANES):
            s = u[:, j * LANES:(j + 1) * LANES] * (q_gain if ci < n_norm - 1 else k_gain)
            if rope:
                s = _rope64(s, cos_ref[...], sup_ref[...], sdn_ref[...])
            if ci < n_norm - 1:
                lo = ci * MXU_COLS + j * LANES
                q_ref[:, lo:lo + LANES] = s.astype(BF16)
            else:
                for e in range(heads_per_slab):
                    k_ref[j * heads_per_slab + e] = s[:, e * HEAD_DIM:(e + 1) * HEAD_DIM].astype(BF16)

    us = {0: project(0), 1: project(1)}
    mss = {0: _head_meansq(us[0], bd_ref)}
    for ci in range(n_norm):
        if ci + 2 <= n_norm:
            us[ci + 2] = project(ci + 2)
        if ci + 1 < n_norm:
            mss[ci + 1] = _head_meansq(us[ci + 1], bd_ref)
        finish(ci, us.pop(ci), mss.pop(ci))
    vv = us.pop(n_norm)
    ones = jnp.ones((vv.shape[0], HEAD_DIM), F32)
    for e in range(N_KV_HEADS):
        v_ref[e] = jnp.concatenate([vv[:, e * HEAD_DIM:(e + 1) * HEAD_DIM], ones], axis=-1).astype(BF16)


def _qkv(x, mod, norm_g, w, qg, kg, bd, cos, s_up, s_dn, *, ctx):
    b, l, d = x.shape
    bm = min(l, 512)
    tab = pl.BlockSpec((bm, LANES), lambda i, t: (t, 0))
    kern = functools.partial(_qkv_kernel, rope=not ctx)
    return pl.pallas_call(
        kern,
        grid=(b, l // bm),
        in_specs=[
            pl.BlockSpec((None, bm, d), lambda i, t: (i, t, 0)),
            _mod_spec(mod, ctx),
            _resident(norm_g),
            _resident(w),
            _resident(qg),
            _resident(kg),
            _resident(bd),
            tab, tab, tab,
        ],
        out_specs=[
            pl.BlockSpec((None, bm, d), lambda i, t: (i, t, 0)),
            pl.BlockSpec((None, N_KV_HEADS, bm, HEAD_DIM), lambda i, t: (i, 0, t, 0)),
            pl.BlockSpec((None, N_KV_HEADS, bm, 2 * HEAD_DIM), lambda i, t: (i, 0, t, 0)),
        ],
        out_shape=[
            jax.ShapeDtypeStruct((b, l, d), BF16),
            jax.ShapeDtypeStruct((b, N_KV_HEADS, l, HEAD_DIM), BF16),
            jax.ShapeDtypeStruct((b, N_KV_HEADS, l, 2 * HEAD_DIM), BF16),
        ],
        compiler_params=_params(2),
        name="qkv",
    )(x, mod.stack, norm_g.stack, w.stack, qg.stack, kg.stack, bd.stack, cos, s_up, s_dn)


def _attn_kernel(*refs, n_sets):
    q_ref = refs[0]
    kv = refs[1:1 + 2 * n_sets]
    o_ref = refs[1 + 2 * n_sets]
    q = q_ref[...]
    ks = [kv[2 * i][...] for i in range(n_sets)]
    vs = [kv[2 * i + 1][...] for i in range(n_sets)]
    outs = []

    def scores(gi):
        qg = q[:, gi * HEAD_DIM:(gi + 1) * HEAD_DIM]
        return [_dot_nt(qg, k) for k in ks]

    ss_next = scores(0)
    for gi in range(GQA_GROUP):
        ss = ss_next
        if gi + 1 < GQA_GROUP:
            ss_next = scores(gi + 1)
        m = ss[0].max(axis=-1, keepdims=True)
        for s in ss[1:]:
            m = jnp.maximum(m, s.max(axis=-1, keepdims=True))
        acc = None
        for s, v in zip(ss, vs):
            pv = _dot(jnp.exp2(s - m).astype(BF16), v)
            acc = pv if acc is None else acc + pv
        o = acc * (1.0 / pltpu.roll(acc, HEAD_DIM, axis=1))
        outs.append(o[:, 0:HEAD_DIM])
    o_ref[...] = jnp.concatenate(outs, axis=-1).astype(BF16)


def _attention(q, kvs):
    b, l, d = q.shape
    bq = min(l, 512)
    width = GQA_GROUP * HEAD_DIM
    in_specs = [pl.BlockSpec((None, bq, width), lambda i, h, t: (i, t, h))]
    args = [q]
    for k, v in kvs:
        lk = k.shape[2]
        in_specs += [pl.BlockSpec((None, None, lk, HEAD_DIM), lambda i, h, t: (i, h, 0, 0)),
                     pl.BlockSpec((None, None, lk, 2 * HEAD_DIM), lambda i, h, t: (i, h, 0, 0))]
        args += [k, v]
    return pl.pallas_call(
        functools.partial(_attn_kernel, n_sets=len(kvs)),
        grid=(b, N_KV_HEADS, l // bq),
        in_specs=in_specs,
        out_specs=pl.BlockSpec((None, bq, width), lambda i, h, t: (i, t, h)),
        out_shape=jax.ShapeDtypeStruct((b, l, d), BF16),
        compiler_params=_params(3),
        name="attention",
    )(*args)


def _rope256(v, cos, sgn_sin):
    return v * cos + pltpu.roll(v, LANES // 2, axis=1) * sgn_sin


def _retproj_kernel(x_ref, mod_ref, g_ref, w_ref, cos_ref, sin_ref, *out_refs, ctx):
    h = _modulate(x_ref[...], g_ref[...], mod_ref[0], mod_ref[1]).astype(BF16)
    nqk = RET_HEADS * RET_HEAD_DIM
    nv = RET_HEADS * RET_V_DIM
    kscale = RET_HEAD_DIM ** -0.5
    col = nqk
    if ctx:
        k_ref, v_ref = out_refs
    else:
        q_ref, k_ref, v_ref, gt_ref = out_refs
        for ci in range(nqk // MXU_COLS):
            u = _dot(h, w_ref[:, ci * MXU_COLS:(ci + 1) * MXU_COLS])
            for j in range(MXU_COLS // LANES):
                sl = slice(j * LANES, (j + 1) * LANES)
                lo = ci * MXU_COLS + j * LANES
                q_ref[:, lo:lo + LANES] = _rope256(u[:, sl], cos_ref[:, sl], sin_ref[:, sl]).astype(BF16)
    for ci in range(nqk // MXU_COLS):
        u = _dot(h, w_ref[:, col + ci * MXU_COLS:col + (ci + 1) * MXU_COLS]) * kscale
        for j in range(MXU_COLS // LANES):
            sl = slice(j * LANES, (j + 1) * LANES)
            lo = ci * MXU_COLS + j * LANES
            s = u[:, sl]
            if not ctx:
                s = _rope256(s, cos_ref[:, sl], sin_ref[:, sl])
            k_ref[:, lo:lo + LANES] = s.astype(BF16)
    col += nqk
    for ci in range(nv // MXU_COLS):
        sl = slice(ci * MXU_COLS, (ci + 1) * MXU_COLS)
        v_ref[:, sl] = _dot(h, w_ref[:, col + ci * MXU_COLS:col + (ci + 1) * MXU_COLS]).astype(BF16)
    if not ctx:
        col += nv
        for ci in range(nv // MXU_COLS):
            sl = slice(ci * MXU_COLS, (ci + 1) * MXU_COLS)
            gt_ref[:, sl] = _dot(h, w_ref[:, col + ci * MXU_COLS:col + (ci + 1) * MXU_COLS]).astype(BF16)


def _retproj(x, mod, norm_g, w, cos, sin, *, ctx):
    b, l, d = x.shape
    bm = min(l, 512)
    nqk = RET_HEADS * RET_HEAD_DIM
    nv = RET_HEADS * RET_V_DIM
    row = lambda n: pl.BlockSpec((None, bm, n), lambda i, t: (i, t, 0))
    tab = pl.BlockSpec((bm, RET_HEAD_DIM), lambda i, t: (t, 0))
    widths = [nqk, nv] if ctx else [nqk, nqk, nv, nv]
    return pl.pallas_call(
        functools.partial(_retproj_kernel, ctx=ctx),
        grid=(b, l // bm),
        in_specs=[row(d), _mod_spec(mod, ctx), _resident(norm_g), _resident(w), tab, tab],
        out_specs=[row(n) for n in widths],
        out_shape=[jax.ShapeDtypeStruct((b, l, n), BF16) for n in widths],
        compiler_params=_params(2),
        name="retproj",
    )(x, mod.stack, norm_g.stack, w.stack, cos, sin)


def _retention_kernel(dec_ref, q_ref, k_ref, v_ref, kc_ref, vc_ref, o_ref, y_ref, *, seq, ctx_len):
    hd = pl.program_id(1)
    c = RET_BLOCK
    n_chunks = seq // c

    def log_gamma(i):
        dv = jnp.full((1, 1), dec_ref[i], F32)
        return jnp.log1p(-jnp.exp2(-dv))

    lf = log_gamma(hd)
    lb = log_gamma(RET_HEADS + hd)
    ii = lax.broadcasted_iota(jnp.int32, (c, c), 0)
    jj = lax.broadcasted_iota(jnp.int32, (c, c), 1)
    diff = (ii - jj).astype(F32)
    decay = jnp.where(diff >= 0, jnp.exp(diff * lf), jnp.exp(-diff * lb))
    pos = lax.broadcasted_iota(jnp.int32, (c, 1), 0).astype(F32)
    qd_f = jnp.exp((pos + 1.0) * lf)
    kd_f = jnp.exp((c - 1.0 - pos) * lf)
    qd_b = jnp.exp((c - pos) * lb)
    kd_b = jnp.exp(pos * lb)
    cd_f = jnp.exp(c * lf)
    cd_b = jnp.exp(c * lb)

    def outer(kd, v):
        return _dot(kd.T.astype(BF16), v)

    posc = lax.broadcasted_iota(jnp.int32, (ctx_len, 1), 0).astype(F32)
    kcf = kc_ref[...].astype(F32)
    vc = vc_ref[...]
    sf = outer(kcf * jnp.exp((ctx_len - 1.0 - posc) * lf), vc)
    sb = outer(kcf * jnp.exp(posc * lb), vc)

    def rows(i):
        return slice(i * c, (i + 1) * c)

    def intra(i):
        a = (_dot_nt(q_ref[rows(i), :], k_ref[rows(i), :]) * decay).astype(BF16)
        return _dot(a, v_ref[rows(i), :])

    for step in range(n_chunks):
        i = step
        j = n_chunks - 1 - step
        inter_f = _dot(q_ref[rows(i), :], sf.astype(BF16)) * qd_f
        inter_b = _dot(q_ref[rows(j), :], sb.astype(BF16)) * qd_b
        if i < j:
            y_ref[rows(i), :] = intra(i) + inter_f
            y_ref[rows(j), :] = intra(j) + inter_b
        else:
            o_ref[rows(i), :] = (y_ref[rows(i), :] + inter_f).astype(BF16)
            o_ref[rows(j), :] = (y_ref[rows(j), :] + inter_b).astype(BF16)
        if step < n_chunks - 1:
            sf = cd_f * sf + outer(k_ref[rows(i), :].astype(F32) * kd_f, v_ref[rows(i), :])
            sb = cd_b * sb + outer(k_ref[rows(j), :].astype(F32) * kd_b, v_ref[rows(j), :])


def _retention(dec, q, k, v, kc, vc):
    b, l, _ = q.shape
    lc = kc.shape[1]
    qk = lambda n: pl.BlockSpec((None, n, RET_HEAD_DIM), lambda i, h: (i, 0, h))
    vv = lambda n: pl.BlockSpec((None, n, RET_V_DIM), lambda i, h: (i, 0, h))
    return pl.pallas_call(
        functools.partial(_retention_kernel, seq=l, ctx_len=lc),
        grid=(b, RET_HEADS),
        in_specs=[pl.BlockSpec(memory_space=pltpu.SMEM),
                  qk(l), qk(l), vv(l), qk(lc), vv(lc)],
        out_specs=vv(l),
        out_shape=jax.ShapeDtypeStruct((b, l, RET_HEADS * RET_V_DIM), BF16),
        scratch_shapes=[pltpu.VMEM((l, RET_V_DIM), F32)],
        compiler_params=_params(2),
        name="retention",
    )(dec, q, k, v, kc, vc)


def _rope_tables(seq, head_dim):
    quarter = head_dim // 4
    rows = jnp.repeat(jnp.arange(seq // GRID_W, dtype=F32), GRID_W)
    cols = jnp.tile(jnp.arange(GRID_W, dtype=F32), seq // GRID_W)
    inv = ROPE_THETA ** (-jnp.arange(quarter, dtype=F32) / quarter)
    ang = jnp.stack([rows[:, None] * inv, cols[:, None] * inv], axis=1)
    cos = jnp.cos(ang)
    sin = jnp.sin(ang)
    cos_t = jnp.stack([cos, cos], axis=2).reshape(seq, head_dim)
    sin_first = jnp.stack([-sin, jnp.zeros_like(sin)], axis=2).reshape(seq, head_dim)
    sin_second = jnp.stack([jnp.zeros_like(sin), sin], axis=2).reshape(seq, head_dim)
    return cos_t, sin_first, sin_second


def kernel(x, c, ctx, c_ctx, ada_w, ada_b, norm_mix_g, norm_ffn_g, final_norm_g, conv_w_in, conv_k, conv_w_out, attn_w_qkv, attn_q_norm_g, attn_k_norm_g, attn_w_out, ret_w_in, ret_decay, ret_w_out, ffn_w_up, ffn_conv_k, ffn_conv_b, ffn_w_down):
    batch, seq, d = x.shape
    assert d == D_MODEL and seq % (2 * RET_BLOCK) == 0 and ctx.shape[1] % SUBLANES == 0

    cvec = jnp.zeros((MOD_ROWS, d), F32).at[:batch].set(c).at[CTX_ROW].set(c_ctx)
    mod_all = _ada(cvec, ada_w, ada_b).reshape(DEPTH, MOD_ROWS, 6, 1, d)

    a_cos, a_first, a_second = _rope_tables(seq, HEAD_DIM)
    rep = LANES // HEAD_DIM
    a_cos, a_first, a_second = (jnp.tile(t, (1, rep)) for t in (a_cos, a_first, a_second))
    r_cos, r_first, r_second = _rope_tables(seq, RET_HEAD_DIM)
    r_sin = r_first + r_second
    head_sum = jnp.kron(jnp.eye(MXU_COLS // HEAD_DIM, dtype=F32),
                        jnp.full((HEAD_DIM, HEAD_DIM), 1.0 / HEAD_DIM, F32)).astype(BF16)[None]

    rows = lambda v: v.reshape(v.shape[0], 1, -1)
    norm_mix, norm_ffn = rows(norm_mix_g), rows(norm_ffn_g)
    final_g = _Layer(final_norm_g.reshape(1, 1, d), 0)
    mix_w1, mix_w2 = conv_w_in.astype(BF16), conv_w_out.astype(BF16)
    no_bias = _Layer(jnp.zeros((1, 1, d), F32), 0)
    att_wq, att_wo = attn_w_qkv.astype(BF16), attn_w_out.astype(BF16)
    att_qg = rows(jnp.tile(attn_q_norm_g, (1, rep)))
    att_kg = rows(jnp.tile(attn_k_norm_g, (1, rep)))
    ret_wi, ret_wo = ret_w_in.astype(BF16), ret_w_out.astype(BF16)
    ffn_w1, ffn_w2, ffn_cb = ffn_w_up.astype(BF16), ffn_w_down.astype(BF16), rows(ffn_conv_b)

    kinds = [i % N_MIXERS for i in range(DEPTH)]
    reads_ctx = [kd in (1, 2) for kd in kinds]
    cx = ctx
    for i in range(DEPTH):
        kind = kinds[i]
        j = i // N_MIXERS
        ctx_out = any(reads_ctx[i + 1:])
        mod = _Layer(mod_all, i)
        ng = _Layer(norm_mix, i)
        last = i == DEPTH - 1
        if kind == 0:
            mix = functools.partial(_convmlp, mod=mod, norm_g=ng, w1=_Layer(mix_w1, j), ck=_Layer(conv_k, j),
                                    cb=no_bias, w2=_Layer(mix_w2, j), final_g=final_g, kind="mix")
            x = mix(x, ctx=False)
            if ctx_out:
                cx = mix(cx, ctx=True)
        elif kind == 1:
            wo = _Layer(att_wo, j)
            proj = functools.partial(_qkv, mod=mod, norm_g=ng, w=_Layer(att_wq, j), qg=_Layer(att_qg, j),
                                     kg=_Layer(att_kg, j), bd=_Layer(head_sum, 0),
                                     cos=a_cos, s_up=a_first, s_dn=a_second)
            q, k, v = proj(x, ctx=False)
            qc, kc, vc = proj(cx, ctx=True)
            o = _attention(q, [(k, v), (kc, vc)])
            x = _outproj([o], x, mod, wo, ctx=False)
            if ctx_out:
                oc = _attention(qc, [(kc, vc)])
                cx = _outproj([oc], cx, mod, wo, ctx=True)
        else:
            wi = _Layer(ret_wi, j)
            q, k, v, gt = _retproj(x, mod, ng, wi, r_cos, r_sin, ctx=False)
            kc, vc = _retproj(cx, mod, ng, wi, r_cos, r_sin, ctx=True)
            y = _retention(ret_decay[j].reshape(-1), q, k, v, kc, vc)
            x = _outproj([y, gt], x, mod, _Layer(ret_wo, j), ctx=False)
            assert not ctx_out
        ffn = functools.partial(_convmlp, mod=mod, norm_g=_Layer(norm_ffn, i), w1=_Layer(ffn_w1, i),
                                ck=_Layer(ffn_conv_k, i), cb=_Layer(ffn_cb, i), w2=_Layer(ffn_w2, i),
                                final_g=final_g, kind="ffn")
        x = ffn(x, ctx=False, final_norm=last)
        if ctx_out:
            cx = ffn(cx, ctx=True)
    return x
```

```python
import functools
from typing import NamedTuple

import jax
import jax.numpy as jnp
from jax import lax
from jax.experimental import pallas as pl
from jax.experimental.pallas import tpu as pltpu

F32 = jnp.float32
BF16 = jnp.bfloat16

D_MODEL = 1024
DEPTH = 4
GRID_W = 64
N_MIXERS = 3
HEAD_DIM = 64
N_Q_HEADS = D_MODEL // HEAD_DIM
N_KV_HEADS = N_Q_HEADS // 4
GQA_GROUP = N_Q_HEADS // N_KV_HEADS
ROPE_THETA = 10000.0
RET_HEAD_DIM = 256
RET_HEADS = D_MODEL // RET_HEAD_DIM
RET_V_DIM = 2 * RET_HEAD_DIM
RET_BLOCK = 256
D_FF = ((8 * D_MODEL // 3 + 127) // 128) * 128
NORM_EPS = 1e-6
LOG2_E = 1.4426950408889634

LANES = 128
SUBLANES = 8
BF16_ROWS = 16
MXU_COLS = 256
VMEM_LIMIT = 56 * 1024 * 1024

N_U_SLOTS = 11
V_ROWS = HEAD_DIM + 16
MOD_ROWS = 16
CTX_ROW = 8


def _params(n_axes):
    return pltpu.CompilerParams(dimension_semantics=("arbitrary",) * n_axes,
                                vmem_limit_bytes=VMEM_LIMIT)


class _Layer(NamedTuple):
    stack: jax.Array
    index: int

    @property
    def shape(self):
        return self.stack.shape[1:]


def _resident(p):
    zeros = (0,) * len(p.shape)
    return pl.BlockSpec((None,) + p.shape, lambda *_: (p.index,) + zeros,
                        pipeline_mode=pl.Buffered(1))


def _silu(v):
    return v * (1.0 / (1.0 + jnp.exp(-v)))


def _modulate(xf, g, shift, scale):
    ms = jnp.mean(xf * xf, axis=-1, keepdims=True)
    y = xf * lax.rsqrt(ms + NORM_EPS) * g
    return y * (1.0 + scale) + shift


def _dot(a, b):
    return jnp.dot(a, b, preferred_element_type=F32)


def _dot_nt(a, b):
    return lax.dot_general(a, b, (((1,), (1,)), ((), ())), preferred_element_type=F32)


def _ada_kernel(c_ref, w_ref, b_ref, o_ref):
    s = _silu(c_ref[...]).astype(BF16)
    o_ref[...] = _dot(s, w_ref[...].astype(BF16)) + b_ref[...]


def _ada(cvec, ada_w, ada_b):
    depth, d, n = ada_w.shape
    tn = 1536
    return pl.pallas_call(
        _ada_kernel,
        grid=(depth, n // tn),
        in_specs=[
            pl.BlockSpec((MOD_ROWS, d), lambda l, j: (0, 0)),
            pl.BlockSpec((None, d, tn), lambda l, j: (l, 0, j)),
            pl.BlockSpec((None, 1, tn), lambda l, j: (l, 0, j)),
        ],
        out_specs=pl.BlockSpec((None, MOD_ROWS, tn), lambda l, j: (l, 0, j)),
        out_shape=jax.ShapeDtypeStruct((depth, MOD_ROWS, n), F32),
        compiler_params=_params(2),
        name="ada",
    )(cvec, ada_w, ada_b.reshape(depth, 1, n))


def _mod_spec(mod, ctx):
    blk = (None, None, 6, 1, D_MODEL)
    if ctx:
        return pl.BlockSpec(blk, lambda b, t: (mod.index, CTX_ROW, 0, 0, 0))
    return pl.BlockSpec(blk, lambda b, t: (mod.index, b, 0, 0, 0))


def _convmlp_kernel(x_ref, xp_ref, xn_ref, mod_ref, g_ref, w1_ref, ck_ref, cb_ref, w2_ref, fg_ref,
                    o_ref, h_ref, u_ref, z_ref, *, kind, bm, mod_base, final_norm):
    t = pl.program_id(1)
    nt = pl.num_programs(1)
    shift = mod_ref[mod_base]
    scale = mod_ref[mod_base + 1]
    gate = mod_ref[mod_base + 2]
    g = g_ref[...]
    x = x_ref[...]
    h_ref[0:bm, :] = _modulate(x, g, shift, scale).astype(BF16)
    keep_n = (t < nt - 1).astype(F32)
    keep_p = (t > 0).astype(F32)
    halo = jnp.concatenate([_modulate(xn_ref[...], g, shift, scale) * keep_n,
                            _modulate(xp_ref[...], g, shift, scale) * keep_p], axis=0)
    h_ref[bm:bm + BF16_ROWS, :] = halo.astype(BF16)
    h = h_ref[...]
    n_mid = D_FF if kind == "ffn" else D_MODEL
    n_chunks = n_mid // MXU_COLS
    top = SUBLANES

    def rows_of(s, pos, n):
        return pl.ds(s + 2 * pos, n, stride=2)

    def put(slot, s, u):
        for j in range(MXU_COLS // LANES):
            uj = u[:, j * LANES:(j + 1) * LANES]
            u_ref[slot, j, rows_of(s, 0, top), :] = uj[bm + top:bm + 2 * top]
            u_ref[slot, j, rows_of(s, top, bm), :] = uj[0:bm]
            u_ref[slot, j, rows_of(s, top + bm, top), :] = uj[bm:bm + top]

    def conv3(slot, j, s, c0):
        k0 = ck_ref[0:1, c0:c0 + LANES]
        k1 = ck_ref[1:2, c0:c0 + LANES]
        k2 = ck_ref[2:3, c0:c0 + LANES]
        return (u_ref[slot, j, rows_of(s, top - 1, bm), :] * k0
                + u_ref[slot, j, rows_of(s, top, bm), :] * k1
                + u_ref[slot, j, rows_of(s, top + 1, bm), :] * k2)

    def up(ci):
        slot = ci % N_U_SLOTS
        cols = lambda s: slice(s * n_mid + ci * MXU_COLS, s * n_mid + (ci + 1) * MXU_COLS)
        if kind == "ffn":
            put(slot, 0, _dot(h, w1_ref[:, cols(0)]))
            put(slot, 1, _dot(h, w1_ref[:, cols(1)]))
        else:
            put(slot, 0, _dot(h, w1_ref[:, cols(0)]))
            put(slot, 1, _dot(h, w1_ref[:, cols(1)]) * _dot(h, w1_ref[:, cols(2)]))

    def mid(ci):
        slot = ci % N_U_SLOTS
        for j in range(MXU_COLS // LANES):
            c0 = ci * MXU_COLS + j * LANES
            if kind == "ffn":
                cv = conv3(slot, j, 0, c0) + cb_ref[:, c0:c0 + LANES]
                cg = conv3(slot, j, 1, D_FF + c0) + cb_ref[:, D_FF + c0:D_FF + c0 + LANES]
                z = _silu(cg) * cv
            else:
                z = u_ref[slot, j, rows_of(0, top, bm), :] * conv3(slot, j, 1, c0)
            z_ref[:, c0:c0 + LANES] = z.astype(BF16)

    up(0)
    for ci in range(n_chunks):
        if ci + 1 < n_chunks:
            up(ci + 1)
        mid(ci)
    out = x + gate * _dot(z_ref[...], w2_ref[...])
    if final_norm:
        ms = jnp.mean(out * out, axis=-1, keepdims=True)
        out = out * lax.rsqrt(ms + NORM_EPS) * fg_ref[...]
    o_ref[...] = out


def _convmlp(x, mod, norm_g, w1, ck, cb, w2, final_g, *, kind, ctx, final_norm=False):
    b, l, d = x.shape
    bm = min(l, 512)
    nt = l // bm
    hb = bm // SUBLANES
    last_hb = l // SUBLANES - 1
    kern = functools.partial(_convmlp_kernel, kind=kind, bm=bm,
                             mod_base=3 if kind == "ffn" else 0, final_norm=final_norm)
    return pl.pallas_call(
        kern,
        grid=(b, nt),
        in_specs=[
            pl.BlockSpec((None, bm, d), lambda i, t: (i, t, 0)),
            pl.BlockSpec((None, SUBLANES, d), lambda i, t: (i, jnp.maximum(t * hb - 1, 0), 0)),
            pl.BlockSpec((None, SUBLANES, d), lambda i, t: (i, jnp.minimum((t + 1) * hb, last_hb), 0)),
            _mod_spec(mod, ctx),
            _resident(norm_g),
            _resident(w1),
            _resident(ck),
            _resident(cb),
            _resident(w2),
            _resident(final_g),
        ],
        out_specs=pl.BlockSpec((None, bm, d), lambda i, t: (i, t, 0)),
        out_shape=jax.ShapeDtypeStruct((b, l, d), F32),
        scratch_shapes=[pltpu.VMEM((bm + BF16_ROWS, d), BF16),
                        pltpu.VMEM((N_U_SLOTS, MXU_COLS // LANES, 2 * (bm + 2 * SUBLANES), LANES), F32),
                        pltpu.VMEM((bm, w2.shape[0]), BF16)],
        compiler_params=_params(2),
        name="convmlp_" + kind,
    )(x, x, x, mod.stack, norm_g.stack, w1.stack, ck.stack, cb.stack, w2.stack, final_g.stack)


def _outproj_kernel(*refs, gated):
    if not gated:
        a_ref, x_ref, mod_ref, w_ref, o_ref = refs
        acc = _dot(a_ref[...], w_ref[...])
    else:
        y_ref, gt_ref, x_ref, mod_ref, w_ref, o_ref = refs

        def gate(hd):
            cols = slice(hd * RET_V_DIM, (hd + 1) * RET_V_DIM)
            y = y_ref[:, cols].astype(F32)
            ms = jnp.mean(y * y, axis=-1, keepdims=True)
            return (_silu(gt_ref[:, cols].astype(F32)) * (y * lax.rsqrt(ms + NORM_EPS))).astype(BF16)

        a_next = gate(0)
        acc = None
        for hd in range(RET_HEADS):
            a = a_next
            if hd + 1 < RET_HEADS:
                a_next = gate(hd + 1)
            part = _dot(a, w_ref[hd * RET_V_DIM:(hd + 1) * RET_V_DIM, :])
            acc = part if acc is None else acc + part
    o_ref[...] = x_ref[...] + mod_ref[2] * acc


def _outproj(acts, x, mod, w, *, ctx):
    b, l, d = x.shape
    bm = min(l, 512)
    act_specs = [pl.BlockSpec((None, bm, a.shape[-1]), lambda i, t: (i, t, 0)) for a in acts]
    return pl.pallas_call(
        functools.partial(_outproj_kernel, gated=len(acts) == 2),
        grid=(b, l // bm),
        in_specs=act_specs + [
            pl.BlockSpec((None, bm, d), lambda i, t: (i, t, 0)),
            _mod_spec(mod, ctx),
            _resident(w),
        ],
        out_specs=pl.BlockSpec((None, bm, d), lambda i, t: (i, t, 0)),
        out_shape=jax.ShapeDtypeStruct((b, l, d), F32),
        input_output_aliases={len(acts): 0},
        compiler_params=_params(2),
        name="outproj",
    )(*acts, x, mod.stack, w.stack)


def _head_meansq(v, bd_ref):
    return _dot((v * v).astype(BF16), bd_ref[...])


def _rope64(v, cos, s_up, s_dn):
    return v * cos + pltpu.roll(v, LANES - 16, axis=1) * s_up + pltpu.roll(v, 16, axis=1) * s_dn


def _qkv_kernel(x_ref, mod_ref, g_ref, w_ref, qg_ref, kg_ref, bd_ref, cos_ref, sup_ref, sdn_ref,
                q_ref, k_ref, v_ref, *, rope):
    h = _modulate(x_ref[...], g_ref[...], mod_ref[0], mod_ref[1]).astype(BF16)
    nq = N_Q_HEADS * HEAD_DIM
    nkv = N_KV_HEADS * HEAD_DIM
    n_norm = (nq + nkv) // MXU_COLS
    q_gain = qg_ref[...] * (HEAD_DIM ** -0.5 * LOG2_E)
    k_gain = kg_ref[...]
    heads_per_slab = LANES // HEAD_DIM

    def project(ci):
        return _dot(h, w_ref[:, ci * MXU_COLS:(ci + 1) * MXU_COLS])

    def finish(ci, u, ms):
        u = u * lax.rsqrt(ms + NORM_EPS)
        for j in range(MXU_COLS // LANES):
            s = u[:, j * LANES:(j + 1) * LANES] * (q_gain if ci < n_norm - 1 else k_gain)
            if rope:
                s = _rope64(s, cos_ref[...], sup_ref[...], sdn_ref[...])
            if ci < n_norm - 1:
                lo = ci * MXU_COLS + j * LANES
                q_ref[:, lo:lo + LANES] = s.astype(BF16)
            else:
                for e in range(heads_per_slab):
                    k_ref[j * heads_per_slab + e] = s[:, e * HEAD_DIM:(e + 1) * HEAD_DIM].astype(BF16)

    us = {0: project(0), 1: project(1)}
    mss = {0: _head_meansq(us[0], bd_ref)}
    for ci in range(n_norm):
        if ci + 2 <= n_norm:
            us[ci + 2] = project(ci + 2)
        if ci + 1 < n_norm:
            mss[ci + 1] = _head_meansq(us[ci + 1], bd_ref)
        finish(ci, us.pop(ci), mss.pop(ci))
    vt = us.pop(n_norm).T
    for e in range(N_KV_HEADS):
        v_ref[e, 0:HEAD_DIM, :] = vt[e * HEAD_DIM:(e + 1) * HEAD_DIM, :].astype(BF16)
        v_ref[e, HEAD_DIM:V_ROWS, :] = jnp.ones((V_ROWS - HEAD_DIM, vt.shape[1]), BF16)


def _qkv(x, mod, norm_g, w, qg, kg, bd, cos, s_up, s_dn, *, ctx):
    b, l, d = x.shape
    bm = min(l, 512)
    tab = pl.BlockSpec((bm, LANES), lambda i, t: (t, 0))
    kern = functools.partial(_qkv_kernel, rope=not ctx)
    return pl.pallas_call(
        kern,
        grid=(b, l // bm),
        in_specs=[
            pl.BlockSpec((None, bm, d), lambda i, t: (i, t, 0)),
            _mod_spec(mod, ctx),
            _resident(norm_g),
            _resident(w),
            _resident(qg),
            _resident(kg),
            _resident(bd),
            tab, tab, tab,
        ],
        out_specs=[
            pl.BlockSpec((None, bm, d), lambda i, t: (i, t, 0)),
            pl.BlockSpec((None, N_KV_HEADS, bm, HEAD_DIM), lambda i, t: (i, 0, t, 0)),
            pl.BlockSpec((None, N_KV_HEADS, V_ROWS, bm), lambda i, t: (i, 0, 0, t)),
        ],
        out_shape=[
            jax.ShapeDtypeStruct((b, l, d), BF16),
            jax.ShapeDtypeStruct((b, N_KV_HEADS, l, HEAD_DIM), BF16),
            jax.ShapeDtypeStruct((b, N_KV_HEADS, V_ROWS, l), BF16),
        ],
        compiler_params=_params(2),
        name="qkv",
    )(x, mod.stack, norm_g.stack, w.stack, qg.stack, kg.stack, bd.stack, cos, s_up, s_dn)


def _attn_kernel(*refs, n_sets):
    q_ref = refs[0]
    kv = refs[1:1 + 2 * n_sets]
    o_ref = refs[1 + 2 * n_sets]
    q = q_ref[...]
    ks = [kv[2 * i][...] for i in range(n_sets)]
    vts = [kv[2 * i + 1][...] for i in range(n_sets)]
    outs = []

    def scores(gi):
        qg = q[:, gi * HEAD_DIM:(gi + 1) * HEAD_DIM]
        return [_dot_nt(k, qg) for k in ks]

    ss_next = scores(0)
    for gi in range(GQA_GROUP):
        ss = ss_next
        if gi + 1 < GQA_GROUP:
            ss_next = scores(gi + 1)
        m = ss[0].max(axis=0, keepdims=True)
        for s in ss[1:]:
            m = jnp.maximum(m, s.max(axis=0, keepdims=True))
        acc = None
        for s, vt in zip(ss, vts):
            pv = _dot(vt, jnp.exp2(s - m).astype(BF16))
            acc = pv if acc is None else acc + pv
        outs.append(acc[0:HEAD_DIM] * (1.0 / acc[HEAD_DIM:HEAD_DIM + 1]))
    o_ref[...] = jnp.concatenate(outs, axis=0).T.astype(BF16)


def _attention(q, kvs):
    b, l, d = q.shape
    bq = min(l, 512)
    width = GQA_GROUP * HEAD_DIM
    in_specs = [pl.BlockSpec((None, bq, width), lambda i, h, t: (i, t, h))]
    args = [q]
    for k, v in kvs:
        lk = k.shape[2]
        in_specs += [pl.BlockSpec((None, None, lk, HEAD_DIM), lambda i, h, t: (i, h, 0, 0)),
                     pl.BlockSpec((None, None, V_ROWS, lk), lambda i, h, t: (i, h, 0, 0))]
        args += [k, v]
    return pl.pallas_call(
        functools.partial(_attn_kernel, n_sets=len(kvs)),
        grid=(b, N_KV_HEADS, l // bq),
        in_specs=in_specs,
        out_specs=pl.BlockSpec((None, bq, width), lambda i, h, t: (i, t, h)),
        out_shape=jax.ShapeDtypeStruct((b, l, d), BF16),
        compiler_params=_params(3),
        name="attention",
    )(*args)


def _rope256(v, cos, sgn_sin):
    return v * cos + pltpu.roll(v, LANES // 2, axis=1) * sgn_sin


def _retproj_kernel(x_ref, mod_ref, g_ref, w_ref, cos_ref, sin_ref, *out_refs, ctx):
    h = _modulate(x_ref[...], g_ref[...], mod_ref[0], mod_ref[1]).astype(BF16)
    nqk = RET_HEADS * RET_HEAD_DIM
    nv = RET_HEADS * RET_V_DIM
    kscale = RET_HEAD_DIM ** -0.5
    col = nqk
    if ctx:
        k_ref, v_ref = out_refs
    else:
        q_ref, k_ref, v_ref, gt_ref = out_refs
        for ci in range(nqk // MXU_COLS):
            u = _dot(h, w_ref[:, ci * MXU_COLS:(ci + 1) * MXU_COLS])
            for j in range(MXU_COLS // LANES):
                sl = slice(j * LANES, (j + 1) * LANES)
                lo = ci * MXU_COLS + j * LANES
                q_ref[:, lo:lo + LANES] = _rope256(u[:, sl], cos_ref[:, sl], sin_ref[:, sl]).astype(BF16)
    for ci in range(nqk // MXU_COLS):
        u = _dot(h, w_ref[:, col + ci * MXU_COLS:col + (ci + 1) * MXU_COLS]) * kscale
        for j in range(MXU_COLS // LANES):
            sl = slice(j * LANES, (j + 1) * LANES)
            lo = ci * MXU_COLS + j * LANES
            s = u[:, sl]
            if not ctx:
                s = _rope256(s, cos_ref[:, sl], sin_ref[:, sl])
            k_ref[:, lo:lo + LANES] = s.astype(BF16)
    col += nqk
    for ci in range(nv // MXU_COLS):
        sl = slice(ci * MXU_COLS, (ci + 1) * MXU_COLS)
        v_ref[:, sl] = _dot(h, w_ref[:, col + ci * MXU_COLS:col + (ci + 1) * MXU_COLS]).astype(BF16)
    if not ctx:
        col += nv
        for ci in range(nv // MXU_COLS):
            sl = slice(ci * MXU_COLS, (ci + 1) * MXU_COLS)
            gt_ref[:, sl] = _dot(h, w_ref[:, col + ci * MXU_COLS:col + (ci + 1) * MXU_COLS]).astype(BF16)


def _retproj(x, mod, norm_g, w, cos, sin, *, ctx):
    b, l, d = x.shape
    bm = min(l, 512)
    nqk = RET_HEADS * RET_HEAD_DIM
    nv = RET_HEADS * RET_V_DIM
    row = lambda n: pl.BlockSpec((None, bm, n), lambda i, t: (i, t, 0))
    tab = pl.BlockSpec((bm, RET_HEAD_DIM), lambda i, t: (t, 0))
    widths = [nqk, nv] if ctx else [nqk, nqk, nv, nv]
    return pl.pallas_call(
        functools.partial(_retproj_kernel, ctx=ctx),
        grid=(b, l // bm),
        in_specs=[row(d), _mod_spec(mod, ctx), _resident(norm_g), _resident(w), tab, tab],
        out_specs=[row(n) for n in widths],
        out_shape=[jax.ShapeDtypeStruct((b, l, n), BF16) for n in widths],
        compiler_params=_params(2),
        name="retproj",
    )(x, mod.stack, norm_g.stack, w.stack, cos, sin)


def _retention_kernel(dec_ref, q_ref, k_ref, v_ref, kc_ref, vc_ref, o_ref, y_ref, *, seq, ctx_len):
    hd = pl.program_id(1)
    c = RET_BLOCK
    n_chunks = seq // c

    def log_gamma(i):
        dv = jnp.full((1, 1), dec_ref[i], F32)
        return jnp.log1p(-jnp.exp2(-dv))

    lf = log_gamma(hd)
    lb = log_gamma(RET_HEADS + hd)
    ii = lax.broadcasted_iota(jnp.int32, (c, c), 0)
    jj = lax.broadcasted_iota(jnp.int32, (c, c), 1)
    diff = (ii - jj).astype(F32)
    decay = jnp.where(diff >= 0, jnp.exp(diff * lf), jnp.exp(-diff * lb))
    pos = lax.broadcasted_iota(jnp.int32, (c, 1), 0).astype(F32)
    qd_f = jnp.exp((pos + 1.0) * lf)
    kd_f = jnp.exp((c - 1.0 - pos) * lf)
    qd_b = jnp.exp((c - pos) * lb)
    kd_b = jnp.exp(pos * lb)
    cd_f = jnp.exp(c * lf)
    cd_b = jnp.exp(c * lb)

    def outer(kd, v):
        return _dot(kd.T.astype(BF16), v)

    posc = lax.broadcasted_iota(jnp.int32, (ctx_len, 1), 0).astype(F32)
    kcf = kc_ref[...].astype(F32)
    vc = vc_ref[...]
    sf = outer(kcf * jnp.exp((ctx_len - 1.0 - posc) * lf), vc)
    sb = outer(kcf * jnp.exp(posc * lb), vc)

    def rows(i):
        return slice(i * c, (i + 1) * c)

    def intra(i):
        a = (_dot_nt(q_ref[rows(i), :], k_ref[rows(i), :]) * decay).astype(BF16)
        return _dot(a, v_ref[rows(i), :])

    for step in range(n_chunks):
        i = step
        j = n_chunks - 1 - step
        inter_f = _dot(q_ref[rows(i), :], sf.astype(BF16)) * qd_f
        inter_b = _dot(q_ref[rows(j), :], sb.astype(BF16)) * qd_b
        if i < j:
            y_ref[rows(i), :] = intra(i) + inter_f
            y_ref[rows(j), :] = intra(j) + inter_b
        else:
            o_ref[rows(i), :] = (y_ref[rows(i), :] + inter_f).astype(BF16)
            o_ref[rows(j), :] = (y_ref[rows(j), :] + inter_b).astype(BF16)
        if step < n_chunks - 1:
            sf = cd_f * sf + outer(k_ref[rows(i), :].astype(F32) * kd_f, v_ref[rows(i), :])
            sb = cd_b * sb + outer(k_ref[rows(j), :].astype(F32) * kd_b, v_ref[rows(j), :])


def _retention(dec, q, k, v, kc, vc):
    b, l, _ = q.shape
    lc = kc.shape[1]
    qk = lambda n: pl.BlockSpec((None, n, RET_HEAD_DIM), lambda i, h: (i, 0, h))
    vv = lambda n: pl.BlockSpec((None, n, RET_V_DIM), lambda i, h: (i, 0, h))
    return pl.pallas_call(
        functools.partial(_retention_kernel, seq=l, ctx_len=lc),
        grid=(b, RET_HEADS),
        in_specs=[pl.BlockSpec(memory_space=pltpu.SMEM),
                  qk(l), qk(l), vv(l), qk(lc), vv(lc)],
        out_specs=vv(l),
        out_shape=jax.ShapeDtypeStruct((b, l, RET_HEADS * RET_V_DIM), BF16),
        scratch_shapes=[pltpu.VMEM((l, RET_V_DIM), F32)],
        compiler_params=_params(2),
        name="retention",
    )(dec, q, k, v, kc, vc)


def _rope_tables(seq, head_dim):
    quarter = head_dim // 4
    rows = jnp.repeat(jnp.arange(seq // GRID_W, dtype=F32), GRID_W)
    cols = jnp.tile(jnp.arange(GRID_W, dtype=F32), seq // GRID_W)
    inv = ROPE_THETA ** (-jnp.arange(quarter, dtype=F32) / quarter)
    ang = jnp.stack([rows[:, None] * inv, cols[:, None] * inv], axis=1)
    cos = jnp.cos(ang)
    sin = jnp.sin(ang)
    cos_t = jnp.stack([cos, cos], axis=2).reshape(seq, head_dim)
    sin_first = jnp.stack([-sin, jnp.zeros_like(sin)], axis=2).reshape(seq, head_dim)
    sin_second = jnp.stack([jnp.zeros_like(sin), sin], axis=2).reshape(seq, head_dim)
    return cos_t, sin_first, sin_second


def kernel(x, c, ctx, c_ctx, ada_w, ada_b, norm_mix_g, norm_ffn_g, final_norm_g, conv_w_in, conv_k, conv_w_out, attn_w_qkv, attn_q_norm_g, attn_k_norm_g, attn_w_out, ret_w_in, ret_decay, ret_w_out, ffn_w_up, ffn_conv_k, ffn_conv_b, ffn_w_down):
    batch, seq, d = x.shape
    assert d == D_MODEL and seq % (2 * RET_BLOCK) == 0 and ctx.shape[1] % SUBLANES == 0

    cvec = jnp.zeros((MOD_ROWS, d), F32).at[:batch].set(c).at[CTX_ROW].set(c_ctx)
    mod_all = _ada(cvec, ada_w, ada_b).reshape(DEPTH, MOD_ROWS, 6, 1, d)

    a_cos, a_first, a_second = _rope_tables(seq, HEAD_DIM)
    rep = LANES // HEAD_DIM
    a_cos, a_first, a_second = (jnp.tile(t, (1, rep)) for t in (a_cos, a_first, a_second))
    r_cos, r_first, r_second = _rope_tables(seq, RET_HEAD_DIM)
    r_sin = r_first + r_second
    head_sum = jnp.kron(jnp.eye(MXU_COLS // HEAD_DIM, dtype=F32),
                        jnp.full((HEAD_DIM, HEAD_DIM), 1.0 / HEAD_DIM, F32)).astype(BF16)[None]

    rows = lambda v: v.reshape(v.shape[0], 1, -1)
    norm_mix, norm_ffn = rows(norm_mix_g), rows(norm_ffn_g)
    final_g = _Layer(final_norm_g.reshape(1, 1, d), 0)
    mix_w1, mix_w2 = conv_w_in.astype(BF16), conv_w_out.astype(BF16)
    no_bias = _Layer(jnp.zeros((1, 1, d), F32), 0)
    att_wq, att_wo = attn_w_qkv.astype(BF16), attn_w_out.astype(BF16)
    att_qg = rows(jnp.tile(attn_q_norm_g, (1, rep)))
    att_kg = rows(jnp.tile(attn_k_norm_g, (1, rep)))
    ret_wi, ret_wo = ret_w_in.astype(BF16), ret_w_out.astype(BF16)
    ffn_w1, ffn_w2, ffn_cb = ffn_w_up.astype(BF16), ffn_w_down.astype(BF16), rows(ffn_conv_b)

    kinds = [i % N_MIXERS for i in range(DEPTH)]
    reads_ctx = [kd in (1, 2) for kd in kinds]
    cx = ctx
    for i in range(DEPTH):
        kind = kinds[i]
        j = i // N_MIXERS
        ctx_out = any(reads_ctx[i + 1:])
        mod = _Layer(mod_all, i)
        ng = _Layer(norm_mix, i)
        last = i == DEPTH - 1
        if kind == 0:
            mix = functools.partial(_convmlp, mod=mod, norm_g=ng, w1=_Layer(mix_w1, j), ck=_Layer(conv_k, j),
                                    cb=no_bias, w2=_Layer(mix_w2, j), final_g=final_g, kind="mix")
            x = mix(x, ctx=False)
            if ctx_out:
                cx = mix(cx, ctx=True)
        elif kind == 1:
            wo = _Layer(att_wo, j)
            proj = functools.partial(_qkv, mod=mod, norm_g=ng, w=_Layer(att_wq, j), qg=_Layer(att_qg, j),
                                     kg=_Layer(att_kg, j), bd=_Layer(head_sum, 0),
                                     cos=a_cos, s_up=a_first, s_dn=a_second)
            q, k, v = proj(x, ctx=False)
            qc, kc, vc = proj(cx, ctx=True)
            o = _attention(q, [(k, v), (kc, vc)])
            x = _outproj([o], x, mod, wo, ctx=False)
            if ctx_out:
                oc = _attention(qc, [(kc, vc)])
                cx = _outproj([oc], cx, mod, wo, ctx=True)
        else:
            wi = _Layer(ret_wi, j)
            q, k, v, gt = _retproj(x, mod, ng, wi, r_cos, r_sin, ctx=False)
            kc, vc = _retproj(cx, mod, ng, wi, r_cos, r_sin, ctx=True)
            y = _retention(ret_decay[j].reshape(-1), q, k, v, kc, vc)
            x = _outproj([y, gt], x, mod, _Layer(ret_wo, j), ctx=False)
            assert not ctx_out
        ffn = functools.partial(_convmlp, mod=mod, norm_g=_Layer(norm_ffn, i), w1=_Layer(ffn_w1, i),
                                ck=_Layer(ffn_conv_k, i), cb=_Layer(ffn_cb, i), w2=_Layer(ffn_w2, i),
                                final_g=final_g, kind="ffn")
        x = ffn(x, ctx=False, final_norm=last)
        if ctx_out:
            cx = ffn(cx, ctx=True)
    return x
```

```python
import functools
from typing import NamedTuple

import jax
import jax.numpy as jnp
from jax import lax
from jax.experimental import pallas as pl
from jax.experimental.pallas import tpu as pltpu

F32 = jnp.float32
BF16 = jnp.bfloat16

D_MODEL = 1024
DEPTH = 4
GRID_W = 64
N_MIXERS = 3
HEAD_DIM = 64
N_Q_HEADS = D_MODEL // HEAD_DIM
N_KV_HEADS = N_Q_HEADS // 4
GQA_GROUP = N_Q_HEADS // N_KV_HEADS
ROPE_THETA = 10000.0
RET_HEAD_DIM = 256
RET_HEADS = D_MODEL // RET_HEAD_DIM
RET_V_DIM = 2 * RET_HEAD_DIM
RET_BLOCK = 256
D_FF = ((8 * D_MODEL // 3 + 127) // 128) * 128
NORM_EPS = 1e-6
LOG2_E = 1.4426950408889634

LANES = 128
SUBLANES = 8
BF16_ROWS = 16
MXU_COLS = 256
VMEM_LIMIT = 56 * 1024 * 1024

N_U_SLOTS = 11
MOD_ROWS = 16
CTX_ROW = 8


def _params(n_axes):
    return pltpu.CompilerParams(dimension_semantics=("arbitrary",) * n_axes,
                                vmem_limit_bytes=VMEM_LIMIT)


class _Layer(NamedTuple):
    stack: jax.Array
    index: int

    @property
    def shape(self):
        return self.stack.shape[1:]


def _resident(p):
    zeros = (0,) * len(p.shape)
    return pl.BlockSpec((None,) + p.shape, lambda *_: (p.index,) + zeros,
                        pipeline_mode=pl.Buffered(1))


def _silu(v):
    return v * (1.0 / (1.0 + jnp.exp(-v)))


def _modulate(xf, g, shift, scale):
    ms = jnp.mean(xf * xf, axis=-1, keepdims=True)
    y = xf * lax.rsqrt(ms + NORM_EPS) * g
    return y * (1.0 + scale) + shift


def _dot(a, b):
    return jnp.dot(a, b, preferred_element_type=F32)


def _dot_nt(a, b):
    return lax.dot_general(a, b, (((1,), (1,)), ((), ())), preferred_element_type=F32)


def _ada_kernel(c_ref, w_ref, b_ref, o_ref):
    s = _silu(c_ref[...]).astype(BF16)
    o_ref[...] = _dot(s, w_ref[...].astype(BF16)) + b_ref[...]


def _ada(cvec, ada_w, ada_b):
    depth, d, n = ada_w.shape
    tn = 1536
    return pl.pallas_call(
        _ada_kernel,
        grid=(depth, n // tn),
        in_specs=[
            pl.BlockSpec((MOD_ROWS, d), lambda l, j: (0, 0)),
            pl.BlockSpec((None, d, tn), lambda l, j: (l, 0, j)),
            pl.BlockSpec((None, 1, tn), lambda l, j: (l, 0, j)),
        ],
        out_specs=pl.BlockSpec((None, MOD_ROWS, tn), lambda l, j: (l, 0, j)),
        out_shape=jax.ShapeDtypeStruct((depth, MOD_ROWS, n), F32),
        compiler_params=_params(2),
        name="ada",
    )(cvec, ada_w, ada_b.reshape(depth, 1, n))


def _mod_spec(mod, ctx):
    blk = (None, None, 6, 1, D_MODEL)
    if ctx:
        return pl.BlockSpec(blk, lambda b, t: (mod.index, CTX_ROW, 0, 0, 0))
    return pl.BlockSpec(blk, lambda b, t: (mod.index, b, 0, 0, 0))


def _convmlp_kernel(x_ref, xp_ref, xn_ref, mod_ref, g_ref, w1_ref, ck_ref, cb_ref, w2_ref, fg_ref,
                    o_ref, h_ref, u_ref, z_ref, *, kind, bm, mod_base, final_norm):
    t = pl.program_id(1)
    nt = pl.num_programs(1)
    shift = mod_ref[mod_base]
    scale = mod_ref[mod_base + 1]
    gate = mod_ref[mod_base + 2]
    g = g_ref[...]
    x = x_ref[...]
    h_ref[0:bm, :] = _modulate(x, g, shift, scale).astype(BF16)
    keep_n = (t < nt - 1).astype(F32)
    keep_p = (t > 0).astype(F32)
    halo = jnp.concatenate([_modulate(xn_ref[...], g, shift, scale) * keep_n,
                            _modulate(xp_ref[...], g, shift, scale) * keep_p], axis=0)
    h_ref[bm:bm + BF16_ROWS, :] = halo.astype(BF16)
    h = h_ref[...]
    n_mid = D_FF if kind == "ffn" else D_MODEL
    n_chunks = n_mid // MXU_COLS
    top = SUBLANES

    def rows_of(s, pos, n):
        return pl.ds(s + 2 * pos, n, stride=2)

    def put(slot, s, u):
        for j in range(MXU_COLS // LANES):
            uj = u[:, j * LANES:(j + 1) * LANES]
            u_ref[slot, j, rows_of(s, 0, top), :] = uj[bm + top:bm + 2 * top]
            u_ref[slot, j, rows_of(s, top, bm), :] = uj[0:bm]
            u_ref[slot, j, rows_of(s, top + bm, top), :] = uj[bm:bm + top]

    def conv3(slot, j, s, c0):
        k0 = ck_ref[0:1, c0:c0 + LANES]
        k1 = ck_ref[1:2, c0:c0 + LANES]
        k2 = ck_ref[2:3, c0:c0 + LANES]
        return (u_ref[slot, j, rows_of(s, top - 1, bm), :] * k0
                + u_ref[slot, j, rows_of(s, top, bm), :] * k1
                + u_ref[slot, j, rows_of(s, top + 1, bm), :] * k2)

    def up(ci):
        slot = ci % N_U_SLOTS
        cols = lambda s: slice(s * n_mid + ci * MXU_COLS, s * n_mid + (ci + 1) * MXU_COLS)
        if kind == "ffn":
            put(slot, 0, _dot(h, w1_ref[:, cols(0)]))
            put(slot, 1, _dot(h, w1_ref[:, cols(1)]))
        else:
            put(slot, 0, _dot(h, w1_ref[:, cols(0)]))
            put(slot, 1, _dot(h, w1_ref[:, cols(1)]) * _dot(h, w1_ref[:, cols(2)]))

    def mid(ci):
        slot = ci % N_U_SLOTS
        for j in range(MXU_COLS // LANES):
            c0 = ci * MXU_COLS + j * LANES
            if kind == "ffn":
                cv = conv3(slot, j, 0, c0) + cb_ref[:, c0:c0 + LANES]
                cg = conv3(slot, j, 1, D_FF + c0) + cb_ref[:, D_FF + c0:D_FF + c0 + LANES]
                z = _silu(cg) * cv
            else:
                z = u_ref[slot, j, rows_of(0, top, bm), :] * conv3(slot, j, 1, c0)
            z_ref[:, c0:c0 + LANES] = z.astype(BF16)

    up(0)
    for ci in range(n_chunks):
        if ci + 1 < n_chunks:
            up(ci + 1)
        mid(ci)
    out = x + gate * _dot(z_ref[...], w2_ref[...])
    if final_norm:
        ms = jnp.mean(out * out, axis=-1, keepdims=True)
        out = out * lax.rsqrt(ms + NORM_EPS) * fg_ref[...]
    o_ref[...] = out


def _convmlp(x, mod, norm_g, w1, ck, cb, w2, final_g, *, kind, ctx, final_norm=False):
    b, l, d = x.shape
    bm = min(l, 512)
    nt = l // bm
    hb = bm // SUBLANES
    last_hb = l // SUBLANES - 1
    kern = functools.partial(_convmlp_kernel, kind=kind, bm=bm,
                             mod_base=3 if kind == "ffn" else 0, final_norm=final_norm)
    return pl.pallas_call(
        kern,
        grid=(b, nt),
        in_specs=[
            pl.BlockSpec((None, bm, d), lambda i, t: (i, t, 0)),
            pl.BlockSpec((None, SUBLANES, d), lambda i, t: (i, jnp.maximum(t * hb - 1, 0), 0)),
            pl.BlockSpec((None, SUBLANES, d), lambda i, t: (i, jnp.minimum((t + 1) * hb, last_hb), 0)),
            _mod_spec(mod, ctx),
            _resident(norm_g),
            _resident(w1),
            _resident(ck),
            _resident(cb),
            _resident(w2),
            _resident(final_g),
        ],
        out_specs=pl.BlockSpec((None, bm, d), lambda i, t: (i, t, 0)),
        out_shape=jax.ShapeDtypeStruct((b, l, d), F32),
        scratch_shapes=[pltpu.VMEM((bm + BF16_ROWS, d), BF16),
                        pltpu.VMEM((N_U_SLOTS, MXU_COLS // LANES, 2 * (bm + 2 * SUBLANES), LANES), F32),
                        pltpu.VMEM((bm, w2.shape[0]), BF16)],
        compiler_params=_params(2),
        name="convmlp_" + kind,
    )(x, x, x, mod.stack, norm_g.stack, w1.stack, ck.stack, cb.stack, w2.stack, final_g.stack)


def _outproj_kernel(*refs, gated):
    if not gated:
        a_ref, x_ref, mod_ref, w_ref, o_ref = refs
        acc = _dot(a_ref[...], w_ref[...])
    else:
        y_ref, gt_ref, x_ref, mod_ref, w_ref, o_ref = refs

        def gate(hd):
            cols = slice(hd * RET_V_DIM, (hd + 1) * RET_V_DIM)
            y = y_ref[:, cols].astype(F32)
            ms = jnp.mean(y * y, axis=-1, keepdims=True)
            return (gt_ref[:, cols].astype(F32) * (y * lax.rsqrt(ms + NORM_EPS))).astype(BF16)

        a_next = gate(0)
        acc = None
        for hd in range(RET_HEADS):
            a = a_next
            if hd + 1 < RET_HEADS:
                a_next = gate(hd + 1)
            part = _dot(a, w_ref[hd * RET_V_DIM:(hd + 1) * RET_V_DIM, :])
            acc = part if acc is None else acc + part
    o_ref[...] = x_ref[...] + mod_ref[2] * acc


def _outproj(acts, x, mod, w, *, ctx):
    b, l, d = x.shape
    bm = min(l, 512)
    act_specs = [pl.BlockSpec((None, bm, a.shape[-1]), lambda i, t: (i, t, 0)) for a in acts]
    return pl.pallas_call(
        functools.partial(_outproj_kernel, gated=len(acts) == 2),
        grid=(b, l // bm),
        in_specs=act_specs + [
            pl.BlockSpec((None, bm, d), lambda i, t: (i, t, 0)),
            _mod_spec(mod, ctx),
            _resident(w),
        ],
        out_specs=pl.BlockSpec((None, bm, d), lambda i, t: (i, t, 0)),
        out_shape=jax.ShapeDtypeStruct((b, l, d), F32),
        input_output_aliases={len(acts): 0},
        compiler_params=_params(2),
        name="outproj",
    )(*acts, x, mod.stack, w.stack)


def _head_meansq(v, bd_ref):
    return _dot((v * v).astype(BF16), bd_ref[...])


def _rope64(v, cos, s_up, s_dn):
    return v * cos + pltpu.roll(v, LANES - 16, axis=1) * s_up + pltpu.roll(v, 16, axis=1) * s_dn


def _qkv_kernel(x_ref, mod_ref, g_ref, w_ref, qg_ref, kg_ref, bd_ref, cos_ref, sup_ref, sdn_ref,
                q_ref, k_ref, v_ref, *, rope):
    h = _modulate(x_ref[...], g_ref[...], mod_ref[0], mod_ref[1]).astype(BF16)
    nq = N_Q_HEADS * HEAD_DIM
    nkv = N_KV_HEADS * HEAD_DIM
    n_norm = (nq + nkv) // MXU_COLS
    q_gain = qg_ref[...] * (HEAD_DIM ** -0.5 * LOG2_E)
    k_gain = kg_ref[...]
    heads_per_slab = LANES // HEAD_DIM

    def project(ci):
        return _dot(h, w_ref[:, ci * MXU_COLS:(ci + 1) * MXU_COLS])

    def finish(ci, u, ms):
        u = u * lax.rsqrt(ms + NORM_EPS)
        for j in range(MXU_COLS // LANES):
            s = u[:, j * LANES:(j + 1) * LANES] * (q_gain if ci < n_norm - 1 else k_gain)
            if rope:
                s = _rope64(s, cos_ref[...], sup_ref[...], sdn_ref[...])
            if ci < n_norm - 1:
                lo = ci * MXU_COLS + j * LANES
                q_ref[:, lo:lo + LANES] = s.astype(BF16)
            else:
                for e in range(heads_per_slab):
                    k_ref[j * heads_per_slab + e] = s[:, e * HEAD_DIM:(e + 1) * HEAD_DIM].astype(BF16)

    us = {0: project(0), 1: project(1)}
    mss = {0: _head_meansq(us[0], bd_ref)}
    for ci in range(n_norm):
        if ci + 2 <= n_norm:
            us[ci + 2] = project(ci + 2)
        if ci + 1 < n_norm:
            mss[ci + 1] = _head_meansq(us[ci + 1], bd_ref)
        finish(ci, us.pop(ci), mss.pop(ci))
    vv = us.pop(n_norm)
    ones = jnp.ones((vv.shape[0], HEAD_DIM), F32)
    for e in range(N_KV_HEADS):
        v_ref[e] = jnp.concatenate([vv[:, e * HEAD_DIM:(e + 1) * HEAD_DIM], ones], axis=-1).astype(BF16)


def _qkv(x, mod, norm_g, w, qg, kg, bd, cos, s_up, s_dn, *, ctx):
    b, l, d = x.shape
    bm = min(l, 512)
    tab = pl.BlockSpec((bm, LANES), lambda i, t: (t, 0))
    kern = functools.partial(_qkv_kernel, rope=not ctx)
    return pl.pallas_call(
        kern,
        grid=(b, l // bm),
        in_specs=[
            pl.BlockSpec((None, bm, d), lambda i, t: (i, t, 0)),
            _mod_spec(mod, ctx),
            _resident(norm_g),
            _resident(w),
            _resident(qg),
            _resident(kg),
            _resident(bd),
            tab, tab, tab,
        ],
        out_specs=[
            pl.BlockSpec((None, bm, d), lambda i, t: (i, t, 0)),
            pl.BlockSpec((None, N_KV_HEADS, bm, HEAD_DIM), lambda i, t: (i, 0, t, 0)),
            pl.BlockSpec((None, N_KV_HEADS, bm, 2 * HEAD_DIM), lambda i, t: (i, 0, t, 0)),
        ],
        out_shape=[
            jax.ShapeDtypeStruct((b, l, d), BF16),
            jax.ShapeDtypeStruct((b, N_KV_HEADS, l, HEAD_DIM), BF16),
            jax.ShapeDtypeStruct((b, N_KV_HEADS, l, 2 * HEAD_DIM), BF16),
        ],
        compiler_params=_params(2),
        name="qkv",
    )(x, mod.stack, norm_g.stack, w.stack, qg.stack, kg.stack, bd.stack, cos, s_up, s_dn)


def _attn_kernel(*refs, n_sets):
    q_ref = refs[0]
    kv = refs[1:1 + 2 * n_sets]
    o_ref = refs[1 + 2 * n_sets]
    q = q_ref[...]
    ks = [kv[2 * i][...] for i in range(n_sets)]
    vs = [kv[2 * i + 1][...] for i in range(n_sets)]
    outs = []

    def scores(gi):
        qg = q[:, gi * HEAD_DIM:(gi + 1) * HEAD_DIM]
        return [_dot_nt(qg, k) for k in ks]

    ss_next = scores(0)
    for gi in range(GQA_GROUP):
        ss = ss_next
        if gi + 1 < GQA_GROUP:
            ss_next = scores(gi + 1)
        m = ss[0].max(axis=-1, keepdims=True)
        for s in ss[1:]:
            m = jnp.maximum(m, s.max(axis=-1, keepdims=True))
        acc = None
        for s, v in zip(ss, vs):
            pv = _dot(jnp.exp2(s - m).astype(BF16), v)
            acc = pv if acc is None else acc + pv
        o = acc * (1.0 / pltpu.roll(acc, HEAD_DIM, axis=1))
        outs.append(o[:, 0:HEAD_DIM])
    o_ref[...] = jnp.concatenate(outs, axis=-1).astype(BF16)


def _attention(q, kvs):
    b, l, d = q.shape
    bq = min(l, 512)
    width = GQA_GROUP * HEAD_DIM
    in_specs = [pl.BlockSpec((None, bq, width), lambda i, h, t: (i, t, h))]
    args = [q]
    for k, v in kvs:
        lk = k.shape[2]
        in_specs += [pl.BlockSpec((None, None, lk, HEAD_DIM), lambda i, h, t: (i, h, 0, 0)),
                     pl.BlockSpec((None, None, lk, 2 * HEAD_DIM), lambda i, h, t: (i, h, 0, 0))]
        args += [k, v]
    return pl.pallas_call(
        functools.partial(_attn_kernel, n_sets=len(kvs)),
        grid=(b, N_KV_HEADS, l // bq),
        in_specs=in_specs,
        out_specs=pl.BlockSpec((None, bq, width), lambda i, h, t: (i, t, h)),
        out_shape=jax.ShapeDtypeStruct((b, l, d), BF16),
        compiler_params=_params(3),
        name="attention",
    )(*args)


def _rope256(v, cos, sgn_sin):
    return v * cos + pltpu.roll(v, LANES // 2, axis=1) * sgn_sin


def _retproj_kernel(x_ref, mod_ref, g_ref, w_ref, cos_ref, sin_ref, *out_refs, ctx):
    h = _modulate(x_ref[...], g_ref[...], mod_ref[0], mod_ref[1]).astype(BF16)
    nqk = RET_HEADS * RET_HEAD_DIM
    nv = RET_HEADS * RET_V_DIM
    kscale = RET_HEAD_DIM ** -0.5
    col = nqk
    if ctx:
        k_ref, v_ref = out_refs
    else:
        q_ref, k_ref, v_ref, gt_ref = out_refs
        for ci in range(nqk // MXU_COLS):
            u = _dot(h, w_ref[:, ci * MXU_COLS:(ci + 1) * MXU_COLS])
            for j in range(MXU_COLS // LANES):
                sl = slice(j * LANES, (j + 1) * LANES)
                lo = ci * MXU_COLS + j * LANES
                q_ref[:, lo:lo + LANES] = _rope256(u[:, sl], cos_ref[:, sl], sin_ref[:, sl]).astype(BF16)
    for ci in range(nqk // MXU_COLS):
        u = _dot(h, w_ref[:, col + ci * MXU_COLS:col + (ci + 1) * MXU_COLS]) * kscale
        for j in range(MXU_COLS // LANES):
            sl = slice(j * LANES, (j + 1) * LANES)
            lo = ci * MXU_COLS + j * LANES
            s = u[:, sl]
            if not ctx:
                s = _rope256(s, cos_ref[:, sl], sin_ref[:, sl])
            k_ref[:, lo:lo + LANES] = s.astype(BF16)
    col += nqk
    for ci in range(nv // MXU_COLS):
        sl = slice(ci * MXU_COLS, (ci + 1) * MXU_COLS)
        v_ref[:, sl] = _dot(h, w_ref[:, col + ci * MXU_COLS:col + (ci + 1) * MXU_COLS]).astype(BF16)
    if not ctx:
        col += nv
        for ci in range(nv // MXU_COLS):
            sl = slice(ci * MXU_COLS, (ci + 1) * MXU_COLS)
            gt_ref[:, sl] = _silu(_dot(h, w_ref[:, col + ci * MXU_COLS:col + (ci + 1) * MXU_COLS])).astype(BF16)


def _retproj(x, mod, norm_g, w, cos, sin, *, ctx):
    b, l, d = x.shape
    bm = min(l, 512)
    nqk = RET_HEADS * RET_HEAD_DIM
    nv = RET_HEADS * RET_V_DIM
    row = lambda n: pl.BlockSpec((None, bm, n), lambda i, t: (i, t, 0))
    tab = pl.BlockSpec((bm, RET_HEAD_DIM), lambda i, t: (t, 0))
    widths = [nqk, nv] if ctx else [nqk, nqk, nv, nv]
    return pl.pallas_call(
        functools.partial(_retproj_kernel, ctx=ctx),
        grid=(b, l // bm),
        in_specs=[row(d), _mod_spec(mod, ctx), _resident(norm_g), _resident(w), tab, tab],
        out_specs=[row(n) for n in widths],
        out_shape=[jax.ShapeDtypeStruct((b, l, n), BF16) for n in widths],
        compiler_params=_params(2),
        name="retproj",
    )(x, mod.stack, norm_g.stack, w.stack, cos, sin)


def _retention_kernel(dec_ref, q_ref, k_ref, v_ref, kc_ref, vc_ref, o_ref, y_ref, *, seq, ctx_len):
    hd = pl.program_id(1)
    c = RET_BLOCK
    n_chunks = seq // c

    def log_gamma(i):
        dv = jnp.full((1, 1), dec_ref[i], F32)
        return jnp.log1p(-jnp.exp2(-dv))

    lf = log_gamma(hd)
    lb = log_gamma(RET_HEADS + hd)
    ii = lax.broadcasted_iota(jnp.int32, (c, c), 0)
    jj = lax.broadcasted_iota(jnp.int32, (c, c), 1)
    diff = (ii - jj).astype(F32)
    decay = jnp.where(diff >= 0, jnp.exp(diff * lf), jnp.exp(-diff * lb))
    pos = lax.broadcasted_iota(jnp.int32, (c, 1), 0).astype(F32)
    qd_f = jnp.exp((pos + 1.0) * lf)
    kd_f = jnp.exp((c - 1.0 - pos) * lf)
    qd_b = jnp.exp((c - pos) * lb)
    kd_b = jnp.exp(pos * lb)
    cd_f = jnp.exp(c * lf)
    cd_b = jnp.exp(c * lb)

    def outer(kd, v):
        return _dot(kd.T.astype(BF16), v)

    posc = lax.broadcasted_iota(jnp.int32, (ctx_len, 1), 0).astype(F32)
    kcf = kc_ref[...].astype(F32)
    vc = vc_ref[...]
    sf = outer(kcf * jnp.exp((ctx_len - 1.0 - posc) * lf), vc)
    sb = outer(kcf * jnp.exp(posc * lb), vc)

    def rows(i):
        return slice(i * c, (i + 1) * c)

    def intra(i):
        a = (_dot_nt(q_ref[rows(i), :], k_ref[rows(i), :]) * decay).astype(BF16)
        return _dot(a, v_ref[rows(i), :])

    for step in range(n_chunks):
        i = step
        j = n_chunks - 1 - step
        inter_f = _dot(q_ref[rows(i), :], sf.astype(BF16)) * qd_f
        inter_b = _dot(q_ref[rows(j), :], sb.astype(BF16)) * qd_b
        if i < j:
            y_ref[rows(i), :] = intra(i) + inter_f
            y_ref[rows(j), :] = intra(j) + inter_b
        else:
            o_ref[rows(i), :] = (y_ref[rows(i), :] + inter_f).astype(BF16)
            o_ref[rows(j), :] = (y_ref[rows(j), :] + inter_b).astype(BF16)
        if step < n_chunks - 1:
            sf = cd_f * sf + outer(k_ref[rows(i), :].astype(F32) * kd_f, v_ref[rows(i), :])
            sb = cd_b * sb + outer(k_ref[rows(j), :].astype(F32) * kd_b, v_ref[rows(j), :])


def _retention(dec, q, k, v, kc, vc):
    b, l, _ = q.shape
    lc = kc.shape[1]
    qk = lambda n: pl.BlockSpec((None, n, RET_HEAD_DIM), lambda i, h: (i, 0, h))
    vv = lambda n: pl.BlockSpec((None, n, RET_V_DIM), lambda i, h: (i, 0, h))
    return pl.pallas_call(
        functools.partial(_retention_kernel, seq=l, ctx_len=lc),
        grid=(b, RET_HEADS),
        in_specs=[pl.BlockSpec(memory_space=pltpu.SMEM),
                  qk(l), qk(l), vv(l), qk(lc), vv(lc)],
        out_specs=vv(l),
        out_shape=jax.ShapeDtypeStruct((b, l, RET_HEADS * RET_V_DIM), BF16),
        scratch_shapes=[pltpu.VMEM((l, RET_V_DIM), F32)],
        compiler_params=_params(2),
        name="retention",
    )(dec, q, k, v, kc, vc)


def _rope_tables(seq, head_dim):
    quarter = head_dim // 4
    rows = jnp.repeat(jnp.arange(seq // GRID_W, dtype=F32), GRID_W)
    cols = jnp.tile(jnp.arange(GRID_W, dtype=F32), seq // GRID_W)
    inv = ROPE_THETA ** (-jnp.arange(quarter, dtype=F32) / quarter)
    ang = jnp.stack([rows[:, None] * inv, cols[:, None] * inv], axis=1)
    cos = jnp.cos(ang)
    sin = jnp.sin(ang)
    cos_t = jnp.stack([cos, cos], axis=2).reshape(seq, head_dim)
    sin_first = jnp.stack([-sin, jnp.zeros_like(sin)], axis=2).reshape(seq, head_dim)
    sin_second = jnp.stack([jnp.zeros_like(sin), sin], axis=2).reshape(seq, head_dim)
    return cos_t, sin_first, sin_second


def kernel(x, c, ctx, c_ctx, ada_w, ada_b, norm_mix_g, norm_ffn_g, final_norm_g, conv_w_in, conv_k, conv_w_out, attn_w_qkv, attn_q_norm_g, attn_k_norm_g, attn_w_out, ret_w_in, ret_decay, ret_w_out, ffn_w_up, ffn_conv_k, ffn_conv_b, ffn_w_down):
    batch, seq, d = x.shape
    assert d == D_MODEL and seq % (2 * RET_BLOCK) == 0 and ctx.shape[1] % SUBLANES == 0

    cvec = jnp.zeros((MOD_ROWS, d), F32).at[:batch].set(c).at[CTX_ROW].set(c_ctx)
    mod_all = _ada(cvec, ada_w, ada_b).reshape(DEPTH, MOD_ROWS, 6, 1, d)

    a_cos, a_first, a_second = _rope_tables(seq, HEAD_DIM)
    rep = LANES // HEAD_DIM
    a_cos, a_first, a_second = (jnp.tile(t, (1, rep)) for t in (a_cos, a_first, a_second))
    r_cos, r_first, r_second = _rope_tables(seq, RET_HEAD_DIM)
    r_sin = r_first + r_second
    head_sum = jnp.kron(jnp.eye(MXU_COLS // HEAD_DIM, dtype=F32),
                        jnp.full((HEAD_DIM, HEAD_DIM), 1.0 / HEAD_DIM, F32)).astype(BF16)[None]

    rows = lambda v: v.reshape(v.shape[0], 1, -1)
    norm_mix, norm_ffn = rows(norm_mix_g), rows(norm_ffn_g)
    final_g = _Layer(final_norm_g.reshape(1, 1, d), 0)
    mix_w1, mix_w2 = conv_w_in.astype(BF16), conv_w_out.astype(BF16)
    no_bias = _Layer(jnp.zeros((1, 1, d), F32), 0)
    att_wq, att_wo = attn_w_qkv.astype(BF16), attn_w_out.astype(BF16)
    att_qg = rows(jnp.tile(attn_q_norm_g, (1, rep)))
    att_kg = rows(jnp.tile(attn_k_norm_g, (1, rep)))
    ret_wi, ret_wo = ret_w_in.astype(BF16), ret_w_out.astype(BF16)
    ffn_w1, ffn_w2, ffn_cb = ffn_w_up.astype(BF16), ffn_w_down.astype(BF16), rows(ffn_conv_b)

    kinds = [i % N_MIXERS for i in range(DEPTH)]
    reads_ctx = [kd in (1, 2) for kd in kinds]
    cx = ctx
    for i in range(DEPTH):
        kind = kinds[i]
        j = i // N_MIXERS
        ctx_out = any(reads_ctx[i + 1:])
        mod = _Layer(mod_all, i)
        ng = _Layer(norm_mix, i)
        last = i == DEPTH - 1
        if kind == 0:
            mix = functools.partial(_convmlp, mod=mod, norm_g=ng, w1=_Layer(mix_w1, j), ck=_Layer(conv_k, j),
                                    cb=no_bias, w2=_Layer(mix_w2, j), final_g=final_g, kind="mix")
            x = mix(x, ctx=False)
            if ctx_out:
                cx = mix(cx, ctx=True)
        elif kind == 1:
            wo = _Layer(att_wo, j)
            proj = functools.partial(_qkv, mod=mod, norm_g=ng, w=_Layer(att_wq, j), qg=_Layer(att_qg, j),
                                     kg=_Layer(att_kg, j), bd=_Layer(head_sum, 0),
                                     cos=a_cos, s_up=a_first, s_dn=a_second)
            q, k, v = proj(x, ctx=False)
            qc, kc, vc = proj(cx, ctx=True)
            o = _attention(q, [(k, v), (kc, vc)])
            x = _outproj([o], x, mod, wo, ctx=False)
            if ctx_out:
                oc = _attention(qc, [(kc, vc)])
                cx = _outproj([oc], cx, mod, wo, ctx=True)
        else:
            wi = _Layer(ret_wi, j)
            q, k, v, gt = _retproj(x, mod, ng, wi, r_cos, r_sin, ctx=False)
            kc, vc = _retproj(cx, mod, ng, wi, r_cos, r_sin, ctx=True)
            y = _retention(ret_decay[j].reshape(-1), q, k, v, kc, vc)
            x = _outproj([y, gt], x, mod, _Layer(ret_wo, j), ctx=False)
            assert not ctx_out
        ffn = functools.partial(_convmlp, mod=mod, norm_g=_Layer(norm_ffn, i), w1=_Layer(ffn_w1, i),
                                ck=_Layer(ffn_conv_k, i), cb=_Layer(ffn_cb, i), w2=_Layer(ffn_w2, i),
                                final_g=final_g, kind="ffn")
        x = ffn(x, ctx=False, final_norm=last)
        if ctx_out:
            cx = ffn(cx, ctx=True)
    return x
```

```python
import functools
from typing import NamedTuple

import jax
import jax.numpy as jnp
from jax import lax
from jax.experimental import pallas as pl
from jax.experimental.pallas import tpu as pltpu

F32 = jnp.float32
BF16 = jnp.bfloat16

D_MODEL = 1024
DEPTH = 4
GRID_W = 64
N_MIXERS = 3
HEAD_DIM = 64
N_Q_HEADS = D_MODEL // HEAD_DIM
N_KV_HEADS = N_Q_HEADS // 4
GQA_GROUP = N_Q_HEADS // N_KV_HEADS
ROPE_THETA = 10000.0
RET_HEAD_DIM = 256
RET_HEADS = D_MODEL // RET_HEAD_DIM
RET_V_DIM = 2 * RET_HEAD_DIM
RET_BLOCK = 256
D_FF = ((8 * D_MODEL // 3 + 127) // 128) * 128
NORM_EPS = 1e-6
LOG2_E = 1.4426950408889634

LANES = 128
SUBLANES = 8
BF16_ROWS = 16
MXU_COLS = 256
VMEM_LIMIT = 56 * 1024 * 1024

FFN_TILE_ROWS = 512
TILE_ROWS = 1024
MOD_ROWS = 16
CTX_ROW = 8


def _params(n_axes):
    return pltpu.CompilerParams(dimension_semantics=("arbitrary",) * n_axes,
                                vmem_limit_bytes=VMEM_LIMIT)


class _Layer(NamedTuple):
    stack: jax.Array
    index: int

    @property
    def shape(self):
        return self.stack.shape[1:]


def _resident(p):
    zeros = (0,) * len(p.shape)
    return pl.BlockSpec((None,) + p.shape, lambda *_: (p.index,) + zeros,
                        pipeline_mode=pl.Buffered(1))


def _silu(v):
    return v * (1.0 / (1.0 + jnp.exp(-v)))


def _modulate(xf, g, shift, scale):
    ms = jnp.mean(xf * xf, axis=-1, keepdims=True)
    y = xf * lax.rsqrt(ms + NORM_EPS) * g
    return y * (1.0 + scale) + shift


def _dot(a, b):
    return jnp.dot(a, b, preferred_element_type=F32)


def _dot_nt(a, b):
    return lax.dot_general(a, b, (((1,), (1,)), ((), ())), preferred_element_type=F32)


def _ada_kernel(c_ref, w_ref, b_ref, o_ref):
    s = _silu(c_ref[...]).astype(BF16)
    o_ref[...] = _dot(s, w_ref[...].astype(BF16)) + b_ref[...]


def _ada(cvec, ada_w, ada_b):
    depth, d, n = ada_w.shape
    tn = 1536
    return pl.pallas_call(
        _ada_kernel,
        grid=(depth, n // tn),
        in_specs=[
            pl.BlockSpec((MOD_ROWS, d), lambda l, j: (0, 0)),
            pl.BlockSpec((None, d, tn), lambda l, j: (l, 0, j)),
            pl.BlockSpec((None, 1, tn), lambda l, j: (l, 0, j)),
        ],
        out_specs=pl.BlockSpec((None, MOD_ROWS, tn), lambda l, j: (l, 0, j)),
        out_shape=jax.ShapeDtypeStruct((depth, MOD_ROWS, n), F32),
        compiler_params=_params(2),
        name="ada",
    )(cvec, ada_w, ada_b.reshape(depth, 1, n))


def _mod_spec(mod, ctx):
    blk = (None, None, 6, 1, D_MODEL)
    if ctx:
        return pl.BlockSpec(blk, lambda b, t: (mod.index, CTX_ROW, 0, 0, 0))
    return pl.BlockSpec(blk, lambda b, t: (mod.index, b, 0, 0, 0))


def _convmlp_kernel(x_ref, xp_ref, xn_ref, mod_ref, g_ref, w1_ref, ck_ref, cb_ref, w2_ref, fg_ref,
                    o_ref, h_ref, u_ref, z_ref, *, kind, bm, mod_base, final_norm):
    t = pl.program_id(1)
    nt = pl.num_programs(1)
    shift = mod_ref[mod_base]
    scale = mod_ref[mod_base + 1]
    gate = mod_ref[mod_base + 2]
    g = g_ref[...]
    x = x_ref[...]
    h_ref[0:bm, :] = _modulate(x, g, shift, scale).astype(BF16)
    keep_n = (t < nt - 1).astype(F32)
    keep_p = (t > 0).astype(F32)
    halo = jnp.concatenate([_modulate(xn_ref[...], g, shift, scale) * keep_n,
                            _modulate(xp_ref[...], g, shift, scale) * keep_p], axis=0)
    h_ref[bm:bm + BF16_ROWS, :] = halo.astype(BF16)
    h = h_ref[...]
    n_mid = D_FF if kind == "ffn" else D_MODEL
    n_chunks = n_mid // MXU_COLS
    top = SUBLANES

    def rows_of(s, pos, n):
        return pl.ds(s + 2 * pos, n, stride=2)

    def put(slot, s, u):
        for j in range(MXU_COLS // LANES):
            uj = u[:, j * LANES:(j + 1) * LANES]
            u_ref[slot, j, rows_of(s, 0, top), :] = uj[bm + top:bm + 2 * top]
            u_ref[slot, j, rows_of(s, top, bm), :] = uj[0:bm]
            u_ref[slot, j, rows_of(s, top + bm, top), :] = uj[bm:bm + top]

    def conv3(slot, j, s, c0):
        k0 = ck_ref[0:1, c0:c0 + LANES]
        k1 = ck_ref[1:2, c0:c0 + LANES]
        k2 = ck_ref[2:3, c0:c0 + LANES]
        return (u_ref[slot, j, rows_of(s, top - 1, bm), :] * k0
                + u_ref[slot, j, rows_of(s, top, bm), :] * k1
                + u_ref[slot, j, rows_of(s, top + 1, bm), :] * k2)

    def up(ci):
        slot = ci
        cols = lambda s: slice(s * n_mid + ci * MXU_COLS, s * n_mid + (ci + 1) * MXU_COLS)
        if kind == "ffn":
            put(slot, 0, _dot(h, w1_ref[:, cols(0)]))
            put(slot, 1, _dot(h, w1_ref[:, cols(1)]))
        else:
            put(slot, 0, _dot(h, w1_ref[:, cols(0)]))
            put(slot, 1, _dot(h, w1_ref[:, cols(1)]) * _dot(h, w1_ref[:, cols(2)]))

    def mid(ci):
        slot = ci
        for j in range(MXU_COLS // LANES):
            c0 = ci * MXU_COLS + j * LANES
            if kind == "ffn":
                cv = conv3(slot, j, 0, c0) + cb_ref[:, c0:c0 + LANES]
                cg = conv3(slot, j, 1, D_FF + c0) + cb_ref[:, D_FF + c0:D_FF + c0 + LANES]
                z = _silu(cg) * cv
            else:
                z = u_ref[slot, j, rows_of(0, top, bm), :] * conv3(slot, j, 1, c0)
            z_ref[:, c0:c0 + LANES] = z.astype(BF16)

    up(0)
    for ci in range(n_chunks):
        if ci + 1 < n_chunks:
            up(ci + 1)
        mid(ci)
    out = x + gate * _dot(z_ref[...], w2_ref[...])
    if final_norm:
        ms = jnp.mean(out * out, axis=-1, keepdims=True)
        out = out * lax.rsqrt(ms + NORM_EPS) * fg_ref[...]
    o_ref[...] = out


def _convmlp(x, mod, norm_g, w1, ck, cb, w2, final_g, *, kind, ctx, final_norm=False):
    b, l, d = x.shape
    bm = min(l, FFN_TILE_ROWS if kind == "ffn" else TILE_ROWS)
    n_chunks = w2.shape[0] // MXU_COLS
    nt = l // bm
    hb = bm // SUBLANES
    last_hb = l // SUBLANES - 1
    kern = functools.partial(_convmlp_kernel, kind=kind, bm=bm,
                             mod_base=3 if kind == "ffn" else 0, final_norm=final_norm)
    return pl.pallas_call(
        kern,
        grid=(b, nt),
        in_specs=[
            pl.BlockSpec((None, bm, d), lambda i, t: (i, t, 0)),
            pl.BlockSpec((None, SUBLANES, d), lambda i, t: (i, jnp.maximum(t * hb - 1, 0), 0)),
            pl.BlockSpec((None, SUBLANES, d), lambda i, t: (i, jnp.minimum((t + 1) * hb, last_hb), 0)),
            _mod_spec(mod, ctx),
            _resident(norm_g),
            _resident(w1),
            _resident(ck),
            _resident(cb),
            _resident(w2),
            _resident(final_g),
        ],
        out_specs=pl.BlockSpec((None, bm, d), lambda i, t: (i, t, 0)),
        out_shape=jax.ShapeDtypeStruct((b, l, d), F32),
        scratch_shapes=[pltpu.VMEM((bm + BF16_ROWS, d), BF16),
                        pltpu.VMEM((n_chunks, MXU_COLS // LANES, 2 * (bm + 2 * SUBLANES), LANES), F32),
                        pltpu.VMEM((bm, w2.shape[0]), BF16)],
        compiler_params=_params(2),
        name="convmlp_" + kind,
    )(x, x, x, mod.stack, norm_g.stack, w1.stack, ck.stack, cb.stack, w2.stack, final_g.stack)


def _outproj_kernel(*refs, gated):
    if not gated:
        a_ref, x_ref, mod_ref, w_ref, o_ref = refs
        acc = _dot(a_ref[...], w_ref[...])
    else:
        y_ref, gt_ref, x_ref, mod_ref, w_ref, o_ref = refs

        def gate(hd):
            cols = slice(hd * RET_V_DIM, (hd + 1) * RET_V_DIM)
            y = y_ref[:, cols].astype(F32)
            ms = jnp.mean(y * y, axis=-1, keepdims=True)
            return (_silu(gt_ref[:, cols].astype(F32)) * (y * lax.rsqrt(ms + NORM_EPS))).astype(BF16)

        a_next = gate(0)
        acc = None
        for hd in range(RET_HEADS):
            a = a_next
            if hd + 1 < RET_HEADS:
                a_next = gate(hd + 1)
            part = _dot(a, w_ref[hd * RET_V_DIM:(hd + 1) * RET_V_DIM, :])
            acc = part if acc is None else acc + part
    o_ref[...] = x_ref[...] + mod_ref[2] * acc


def _outproj(acts, x, mod, w, *, ctx):
    b, l, d = x.shape
    bm = min(l, TILE_ROWS if len(acts) == 1 else TILE_ROWS // 2)
    act_specs = [pl.BlockSpec((None, bm, a.shape[-1]), lambda i, t: (i, t, 0)) for a in acts]
    return pl.pallas_call(
        functools.partial(_outproj_kernel, gated=len(acts) == 2),
        grid=(b, l // bm),
        in_specs=act_specs + [
            pl.BlockSpec((None, bm, d), lambda i, t: (i, t, 0)),
            _mod_spec(mod, ctx),
            _resident(w),
        ],
        out_specs=pl.BlockSpec((None, bm, d), lambda i, t: (i, t, 0)),
        out_shape=jax.ShapeDtypeStruct((b, l, d), F32),
        input_output_aliases={len(acts): 0},
        compiler_params=_params(2),
        name="outproj",
    )(*acts, x, mod.stack, w.stack)


def _head_meansq(v, bd_ref):
    return _dot((v * v).astype(BF16), bd_ref[...])


def _rope64(v, cos, s_up, s_dn):
    return v * cos + pltpu.roll(v, LANES - 16, axis=1) * s_up + pltpu.roll(v, 16, axis=1) * s_dn


def _qkv_kernel(x_ref, mod_ref, g_ref, w_ref, qg_ref, kg_ref, bd_ref, cos_ref, sup_ref, sdn_ref,
                q_ref, k_ref, v_ref, *, rope):
    h = _modulate(x_ref[...], g_ref[...], mod_ref[0], mod_ref[1]).astype(BF16)
    nq = N_Q_HEADS * HEAD_DIM
    nkv = N_KV_HEADS * HEAD_DIM
    n_norm = (nq + nkv) // MXU_COLS
    q_gain = qg_ref[...] * (HEAD_DIM ** -0.5 * LOG2_E)
    k_gain = kg_ref[...]
    heads_per_slab = LANES // HEAD_DIM

    def project(ci):
        return _dot(h, w_ref[:, ci * MXU_COLS:(ci + 1) * MXU_COLS])

    def finish(ci, u, ms):
        u = u * lax.rsqrt(ms + NORM_EPS)
        for j in range(MXU_COLS // LANES):
            s = u[:, j * LANES:(j + 1) * LANES] * (q_gain if ci < n_norm - 1 else k_gain)
            if rope:
                s = _rope64(s, cos_ref[...], sup_ref[...], sdn_ref[...])
            if ci < n_norm - 1:
                lo = ci * MXU_COLS + j * LANES
                q_ref[:, lo:lo + LANES] = s.astype(BF16)
            else:
                for e in range(heads_per_slab):
                    k_ref[j * heads_per_slab + e] = s[:, e * HEAD_DIM:(e + 1) * HEAD_DIM].astype(BF16)

    us = {0: project(0), 1: project(1)}
    mss = {0: _head_meansq(us[0], bd_ref)}
    for ci in range(n_norm):
        if ci + 2 <= n_norm:
            us[ci + 2] = project(ci + 2)
        if ci + 1 < n_norm:
            mss[ci + 1] = _head_meansq(us[ci + 1], bd_ref)
        finish(ci, us.pop(ci), mss.pop(ci))
    vv = us.pop(n_norm)
    ones = jnp.ones((vv.shape[0], HEAD_DIM), F32)
    for e in range(N_KV_HEADS):
        v_ref[e] = jnp.concatenate([vv[:, e * HEAD_DIM:(e + 1) * HEAD_DIM], ones], axis=-1).astype(BF16)


def _qkv(x, mod, norm_g, w, qg, kg, bd, cos, s_up, s_dn, *, ctx):
    b, l, d = x.shape
    bm = min(l, TILE_ROWS)
    tab = pl.BlockSpec((bm, LANES), lambda i, t: (t, 0))
    kern = functools.partial(_qkv_kernel, rope=not ctx)
    return pl.pallas_call(
        kern,
        grid=(b, l // bm),
        in_specs=[
            pl.BlockSpec((None, bm, d), lambda i, t: (i, t, 0)),
            _mod_spec(mod, ctx),
            _resident(norm_g),
            _resident(w),
            _resident(qg),
            _resident(kg),
            _resident(bd),
            tab, tab, tab,
        ],
        out_specs=[
            pl.BlockSpec((None, bm, d), lambda i, t: (i, t, 0)),
            pl.BlockSpec((None, N_KV_HEADS, bm, HEAD_DIM), lambda i, t: (i, 0, t, 0)),
            pl.BlockSpec((None, N_KV_HEADS, bm, 2 * HEAD_DIM), lambda i, t: (i, 0, t, 0)),
        ],
        out_shape=[
            jax.ShapeDtypeStruct((b, l, d), BF16),
            jax.ShapeDtypeStruct((b, N_KV_HEADS, l, HEAD_DIM), BF16),
            jax.ShapeDtypeStruct((b, N_KV_HEADS, l, 2 * HEAD_DIM), BF16),
        ],
        compiler_params=_params(2),
        name="qkv",
    )(x, mod.stack, norm_g.stack, w.stack, qg.stack, kg.stack, bd.stack, cos, s_up, s_dn)


def _attn_kernel(*refs, n_sets):
    q_ref = refs[0]
    kv = refs[1:1 + 2 * n_sets]
    o_ref = refs[1 + 2 * n_sets]
    q = q_ref[...]
    ks = [kv[2 * i][...] for i in range(n_sets)]
    vs = [kv[2 * i + 1][...] for i in range(n_sets)]
    outs = []

    def scores(gi):
        qg = q[:, gi * HEAD_DIM:(gi + 1) * HEAD_DIM]
        return [_dot_nt(qg, k) for k in ks]

    ss_next = scores(0)
    for gi in range(GQA_GROUP):
        ss = ss_next
        if gi + 1 < GQA_GROUP:
            ss_next = scores(gi + 1)
        m = ss[0].max(axis=-1, keepdims=True)
        for s in ss[1:]:
            m = jnp.maximum(m, s.max(axis=-1, keepdims=True))
        acc = None
        for s, v in zip(ss, vs):
            pv = _dot(jnp.exp2(s - m).astype(BF16), v)
            acc = pv if acc is None else acc + pv
        o = acc * (1.0 / pltpu.roll(acc, HEAD_DIM, axis=1))
        outs.append(o[:, 0:HEAD_DIM])
    o_ref[...] = jnp.concatenate(outs, axis=-1).astype(BF16)


def _attention(q, kvs):
    b, l, d = q.shape
    bq = min(l, TILE_ROWS)
    width = GQA_GROUP * HEAD_DIM
    in_specs = [pl.BlockSpec((None, bq, width), lambda i, h, t: (i, t, h))]
    args = [q]
    for k, v in kvs:
        lk = k.shape[2]
        in_specs += [pl.BlockSpec((None, None, lk, HEAD_DIM), lambda i, h, t: (i, h, 0, 0)),
                     pl.BlockSpec((None, None, lk, 2 * HEAD_DIM), lambda i, h, t: (i, h, 0, 0))]
        args += [k, v]
    return pl.pallas_call(
        functools.partial(_attn_kernel, n_sets=len(kvs)),
        grid=(b, N_KV_HEADS, l // bq),
        in_specs=in_specs,
        out_specs=pl.BlockSpec((None, bq, width), lambda i, h, t: (i, t, h)),
        out_shape=jax.ShapeDtypeStruct((b, l, d), BF16),
        compiler_params=_params(3),
        name="attention",
    )(*args)


def _rope256(v, cos, sgn_sin):
    return v * cos + pltpu.roll(v, LANES // 2, axis=1) * sgn_sin


def _retproj_kernel(x_ref, mod_ref, g_ref, w_ref, cos_ref, sin_ref, *out_refs, ctx):
    h = _modulate(x_ref[...], g_ref[...], mod_ref[0], mod_ref[1]).astype(BF16)
    nqk = RET_HEADS * RET_HEAD_DIM
    nv = RET_HEADS * RET_V_DIM
    kscale = RET_HEAD_DIM ** -0.5
    col = nqk
    if ctx:
        k_ref, v_ref = out_refs
    else:
        q_ref, k_ref, v_ref, gt_ref = out_refs
        for ci in range(nqk // MXU_COLS):
            u = _dot(h, w_ref[:, ci * MXU_COLS:(ci + 1) * MXU_COLS])
            for j in range(MXU_COLS // LANES):
                sl = slice(j * LANES, (j + 1) * LANES)
                lo = ci * MXU_COLS + j * LANES
                q_ref[:, lo:lo + LANES] = _rope256(u[:, sl], cos_ref[:, sl], sin_ref[:, sl]).astype(BF16)
    for ci in range(nqk // MXU_COLS):
        u = _dot(h, w_ref[:, col + ci * MXU_COLS:col + (ci + 1) * MXU_COLS]) * kscale
        for j in range(MXU_COLS // LANES):
            sl = slice(j * LANES, (j + 1) * LANES)
            lo = ci * MXU_COLS + j * LANES
            s = u[:, sl]
            if not ctx:
                s = _rope256(s, cos_ref[:, sl], sin_ref[:, sl])
            k_ref[:, lo:lo + LANES] = s.astype(BF16)
    col += nqk
    for ci in range(nv // MXU_COLS):
        sl = slice(ci * MXU_COLS, (ci + 1) * MXU_COLS)
        v_ref[:, sl] = _dot(h, w_ref[:, col + ci * MXU_COLS:col + (ci + 1) * MXU_COLS]).astype(BF16)
    if not ctx:
        col += nv
        for ci in range(nv // MXU_COLS):
            sl = slice(ci * MXU_COLS, (ci + 1) * MXU_COLS)
            gt_ref[:, sl] = _dot(h, w_ref[:, col + ci * MXU_COLS:col + (ci + 1) * MXU_COLS]).astype(BF16)


def _retproj(x, mod, norm_g, w, cos, sin, *, ctx):
    b, l, d = x.shape
    bm = min(l, TILE_ROWS)
    nqk = RET_HEADS * RET_HEAD_DIM
    nv = RET_HEADS * RET_V_DIM
    row = lambda n: pl.BlockSpec((None, bm, n), lambda i, t: (i, t, 0))
    tab = pl.BlockSpec((bm, RET_HEAD_DIM), lambda i, t: (t, 0))
    widths = [nqk, nv] if ctx else [nqk, nqk, nv, nv]
    return pl.pallas_call(
        functools.partial(_retproj_kernel, ctx=ctx),
        grid=(b, l // bm),
        in_specs=[row(d), _mod_spec(mod, ctx), _resident(norm_g), _resident(w), tab, tab],
        out_specs=[row(n) for n in widths],
        out_shape=[jax.ShapeDtypeStruct((b, l, n), BF16) for n in widths],
        compiler_params=_params(2),
        name="retproj",
    )(x, mod.stack, norm_g.stack, w.stack, cos, sin)


def _retention_kernel(dec_ref, q_ref, k_ref, v_ref, kc_ref, vc_ref, o_ref, y_ref, *, seq, ctx_len):
    hd = pl.program_id(1)
    c = RET_BLOCK
    n_chunks = seq // c

    def log_gamma(i):
        dv = jnp.full((1, 1), dec_ref[i], F32)
        return jnp.log1p(-jnp.exp2(-dv))

    lf = log_gamma(hd)
    lb = log_gamma(RET_HEADS + hd)
    ii = lax.broadcasted_iota(jnp.int32, (c, c), 0)
    jj = lax.broadcasted_iota(jnp.int32, (c, c), 1)
    diff = (ii - jj).astype(F32)
    decay = jnp.where(diff >= 0, jnp.exp(diff * lf), jnp.exp(-diff * lb))
    pos = lax.broadcasted_iota(jnp.int32, (c, 1), 0).astype(F32)
    qd_f = jnp.exp((pos + 1.0) * lf)
    kd_f = jnp.exp((c - 1.0 - pos) * lf)
    qd_b = jnp.exp((c - pos) * lb)
    kd_b = jnp.exp(pos * lb)
    cd_f = jnp.exp(c * lf)
    cd_b = jnp.exp(c * lb)

    def outer(kd, v):
        return _dot(kd.T.astype(BF16), v)

    posc = lax.broadcasted_iota(jnp.int32, (ctx_len, 1), 0).astype(F32)
    kcf = kc_ref[...].astype(F32)
    vc = vc_ref[...]
    sf = outer(kcf * jnp.exp((ctx_len - 1.0 - posc) * lf), vc)
    sb = outer(kcf * jnp.exp(posc * lb), vc)

    def rows(i):
        return slice(i * c, (i + 1) * c)

    def intra(i):
        a = (_dot_nt(q_ref[rows(i), :], k_ref[rows(i), :]) * decay).astype(BF16)
        return _dot(a, v_ref[rows(i), :])

    for step in range(n_chunks):
        i = step
        j = n_chunks - 1 - step
        inter_f = _dot(q_ref[rows(i), :], sf.astype(BF16)) * qd_f
        inter_b = _dot(q_ref[rows(j), :], sb.astype(BF16)) * qd_b
        if i < j:
            y_ref[rows(i), :] = intra(i) + inter_f
            y_ref[rows(j), :] = intra(j) + inter_b
        else:
            o_ref[rows(i), :] = (y_ref[rows(i), :] + inter_f).astype(BF16)
            o_ref[rows(j), :] = (y_ref[rows(j), :] + inter_b).astype(BF16)
        if step < n_chunks - 1:
            sf = cd_f * sf + outer(k_ref[rows(i), :].astype(F32) * kd_f, v_ref[rows(i), :])
            sb = cd_b * sb + outer(k_ref[rows(j), :].astype(F32) * kd_b, v_ref[rows(j), :])


def _retention(dec, q, k, v, kc, vc):
    b, l, _ = q.shape
    lc = kc.shape[1]
    qk = lambda n: pl.BlockSpec((None, n, RET_HEAD_DIM), lambda i, h: (i, 0, h))
    vv = lambda n: pl.BlockSpec((None, n, RET_V_DIM), lambda i, h: (i, 0, h))
    return pl.pallas_call(
        functools.partial(_retention_kernel, seq=l, ctx_len=lc),
        grid=(b, RET_HEADS),
        in_specs=[pl.BlockSpec(memory_space=pltpu.SMEM),
                  qk(l), qk(l), vv(l), qk(lc), vv(lc)],
        out_specs=vv(l),
        out_shape=jax.ShapeDtypeStruct((b, l, RET_HEADS * RET_V_DIM), BF16),
        scratch_shapes=[pltpu.VMEM((l, RET_V_DIM), F32)],
        compiler_params=_params(2),
        name="retention",
    )(dec, q, k, v, kc, vc)


def _rope_tables(seq, head_dim):
    quarter = head_dim // 4
    rows = jnp.repeat(jnp.arange(seq // GRID_W, dtype=F32), GRID_W)
    cols = jnp.tile(jnp.arange(GRID_W, dtype=F32), seq // GRID_W)
    inv = ROPE_THETA ** (-jnp.arange(quarter, dtype=F32) / quarter)
    ang = jnp.stack([rows[:, None] * inv, cols[:, None] * inv], axis=1)
    cos = jnp.cos(ang)
    sin = jnp.sin(ang)
    cos_t = jnp.stack([cos, cos], axis=2).reshape(seq, head_dim)
    sin_first = jnp.stack([-sin, jnp.zeros_like(sin)], axis=2).reshape(seq, head_dim)
    sin_second = jnp.stack([jnp.zeros_like(sin), sin], axis=2).reshape(seq, head_dim)
    return cos_t, sin_first, sin_second


def kernel(x, c, ctx, c_ctx, ada_w, ada_b, norm_mix_g, norm_ffn_g, final_norm_g, conv_w_in, conv_k, conv_w_out, attn_w_qkv, attn_q_norm_g, attn_k_norm_g, attn_w_out, ret_w_in, ret_decay, ret_w_out, ffn_w_up, ffn_conv_k, ffn_conv_b, ffn_w_down):
    batch, seq, d = x.shape
    assert d == D_MODEL and seq % (2 * RET_BLOCK) == 0 and ctx.shape[1] % SUBLANES == 0

    cvec = jnp.zeros((MOD_ROWS, d), F32).at[:batch].set(c).at[CTX_ROW].set(c_ctx)
    mod_all = _ada(cvec, ada_w, ada_b).reshape(DEPTH, MOD_ROWS, 6, 1, d)

    a_cos, a_first, a_second = _rope_tables(seq, HEAD_DIM)
    rep = LANES // HEAD_DIM
    a_cos, a_first, a_second = (jnp.tile(t, (1, rep)) for t in (a_cos, a_first, a_second))
    r_cos, r_first, r_second = _rope_tables(seq, RET_HEAD_DIM)
    r_sin = r_first + r_second
    head_sum = jnp.kron(jnp.eye(MXU_COLS // HEAD_DIM, dtype=F32),
                        jnp.full((HEAD_DIM, HEAD_DIM), 1.0 / HEAD_DIM, F32)).astype(BF16)[None]

    rows = lambda v: v.reshape(v.shape[0], 1, -1)
    norm_mix, norm_ffn = rows(norm_mix_g), rows(norm_ffn_g)
    final_g = _Layer(final_norm_g.reshape(1, 1, d), 0)
    mix_w1, mix_w2 = conv_w_in.astype(BF16), conv_w_out.astype(BF16)
    no_bias = _Layer(jnp.zeros((1, 1, d), F32), 0)
    att_wq, att_wo = attn_w_qkv.astype(BF16), attn_w_out.astype(BF16)
    att_qg = rows(jnp.tile(attn_q_norm_g, (1, rep)))
    att_kg = rows(jnp.tile(attn_k_norm_g, (1, rep)))
    ret_wi, ret_wo = ret_w_in.astype(BF16), ret_w_out.astype(BF16)
    ffn_w1, ffn_w2, ffn_cb = ffn_w_up.astype(BF16), ffn_w_down.astype(BF16), rows(ffn_conv_b)

    kinds = [i % N_MIXERS for i in range(DEPTH)]
    reads_ctx = [kd in (1, 2) for kd in kinds]
    cx = ctx
    for i in range(DEPTH):
        kind = kinds[i]
        j = i // N_MIXERS
        ctx_out = any(reads_ctx[i + 1:])
        mod = _Layer(mod_all, i)
        ng = _Layer(norm_mix, i)
        last = i == DEPTH - 1
        if kind == 0:
            mix = functools.partial(_convmlp, mod=mod, norm_g=ng, w1=_Layer(mix_w1, j), ck=_Layer(conv_k, j),
                                    cb=no_bias, w2=_Layer(mix_w2, j), final_g=final_g, kind="mix")
            x = mix(x, ctx=False)
            if ctx_out:
                cx = mix(cx, ctx=True)
        elif kind == 1:
            wo = _Layer(att_wo, j)
            proj = functools.partial(_qkv, mod=mod, norm_g=ng, w=_Layer(att_wq, j), qg=_Layer(att_qg, j),
                                     kg=_Layer(att_kg, j), bd=_Layer(head_sum, 0),
                                     cos=a_cos, s_up=a_first, s_dn=a_second)
            q, k, v = proj(x, ctx=False)
            qc, kc, vc = proj(cx, ctx=True)
            o = _attention(q, [(k, v), (kc, vc)])
            x = _outproj([o], x, mod, wo, ctx=False)
            if ctx_out:
                oc = _attention(qc, [(kc, vc)])
                cx = _outproj([oc], cx, mod, wo, ctx=True)
        else:
            wi = _Layer(ret_wi, j)
            q, k, v, gt = _retproj(x, mod, ng, wi, r_cos, r_sin, ctx=False)
            kc, vc = _retproj(cx, mod, ng, wi, r_cos, r_sin, ctx=True)
            y = _retention(ret_decay[j].reshape(-1), q, k, v, kc, vc)
            x = _outproj([y, gt], x, mod, _Layer(ret_wo, j), ctx=False)
            assert not ctx_out
        ffn = functools.partial(_convmlp, mod=mod, norm_g=_Layer(norm_ffn, i), w1=_Layer(ffn_w1, i),
                                ck=_Layer(ffn_conv_k, i), cb=_Layer(ffn_cb, i), w2=_Layer(ffn_w2, i),
                                final_g=final_g, kind="ffn")
        x = ffn(x, ctx=False, final_norm=last)
        if ctx_out:
            cx = ffn(cx, ctx=True)
    return x
```

```python
import functools
from typing import NamedTuple

import jax
import jax.numpy as jnp
from jax import lax
from jax.experimental import pallas as pl
from jax.experimental.pallas import tpu as pltpu

F32 = jnp.float32
BF16 = jnp.bfloat16

D_MODEL = 1024
DEPTH = 4
GRID_W = 64
N_MIXERS = 3
HEAD_DIM = 64
N_Q_HEADS = D_MODEL // HEAD_DIM
N_KV_HEADS = N_Q_HEADS // 4
GQA_GROUP = N_Q_HEADS // N_KV_HEADS
ROPE_THETA = 10000.0
RET_HEAD_DIM = 256
RET_HEADS = D_MODEL // RET_HEAD_DIM
RET_V_DIM = 2 * RET_HEAD_DIM
RET_BLOCK = 256
RET_HEADS_PER_STEP = 2
D_FF = ((8 * D_MODEL // 3 + 127) // 128) * 128
NORM_EPS = 1e-6
LOG2_E = 1.4426950408889634

LANES = 128
SUBLANES = 8
BF16_ROWS = 16
MXU_COLS = 256
VMEM_LIMIT = 56 * 1024 * 1024

FFN_TILE_ROWS = 512
TILE_ROWS = 1024
MOD_ROWS = 16
CTX_ROW = 8


def _params(n_axes):
    return pltpu.CompilerParams(dimension_semantics=("arbitrary",) * n_axes,
                                vmem_limit_bytes=VMEM_LIMIT)


class _Layer(NamedTuple):
    stack: jax.Array
    index: int

    @property
    def shape(self):
        return self.stack.shape[1:]


def _resident(p):
    zeros = (0,) * len(p.shape)
    return pl.BlockSpec((None,) + p.shape, lambda *_: (p.index,) + zeros,
                        pipeline_mode=pl.Buffered(1))


def _silu(v):
    return v * (1.0 / (1.0 + jnp.exp(-v)))


def _modulate(xf, g, shift, scale):
    ms = jnp.mean(xf * xf, axis=-1, keepdims=True)
    y = xf * lax.rsqrt(ms + NORM_EPS) * g
    return y * (1.0 + scale) + shift


def _dot(a, b):
    return jnp.dot(a, b, preferred_element_type=F32)


def _dot_nt(a, b):
    return lax.dot_general(a, b, (((1,), (1,)), ((), ())), preferred_element_type=F32)


def _ada_kernel(c_ref, w_ref, b_ref, o_ref):
    s = _silu(c_ref[...]).astype(BF16)
    o_ref[...] = _dot(s, w_ref[...].astype(BF16)) + b_ref[...]


def _ada(cvec, ada_w, ada_b):
    depth, d, n = ada_w.shape
    tn = 1536
    return pl.pallas_call(
        _ada_kernel,
        grid=(depth, n // tn),
        in_specs=[
            pl.BlockSpec((MOD_ROWS, d), lambda l, j: (0, 0)),
            pl.BlockSpec((None, d, tn), lambda l, j: (l, 0, j)),
            pl.BlockSpec((None, 1, tn), lambda l, j: (l, 0, j)),
        ],
        out_specs=pl.BlockSpec((None, MOD_ROWS, tn), lambda l, j: (l, 0, j)),
        out_shape=jax.ShapeDtypeStruct((depth, MOD_ROWS, n), F32),
        compiler_params=_params(2),
        name="ada",
    )(cvec, ada_w, ada_b.reshape(depth, 1, n))


def _mod_spec(mod, ctx):
    blk = (None, None, 6, 1, D_MODEL)
    if ctx:
        return pl.BlockSpec(blk, lambda b, t: (mod.index, CTX_ROW, 0, 0, 0))
    return pl.BlockSpec(blk, lambda b, t: (mod.index, b, 0, 0, 0))


def _convmlp_kernel(x_ref, xp_ref, xn_ref, mod_ref, g_ref, w1_ref, ck_ref, cb_ref, w2_ref, fg_ref,
                    o_ref, h_ref, u_ref, z_ref, *, kind, bm, mod_base, final_norm):
    t = pl.program_id(1)
    nt = pl.num_programs(1)
    shift = mod_ref[mod_base]
    scale = mod_ref[mod_base + 1]
    gate = mod_ref[mod_base + 2]
    g = g_ref[...]
    x = x_ref[...]
    h_ref[0:bm, :] = _modulate(x, g, shift, scale).astype(BF16)
    keep_n = (t < nt - 1).astype(F32)
    keep_p = (t > 0).astype(F32)
    halo = jnp.concatenate([_modulate(xn_ref[...], g, shift, scale) * keep_n,
                            _modulate(xp_ref[...], g, shift, scale) * keep_p], axis=0)
    h_ref[bm:bm + BF16_ROWS, :] = halo.astype(BF16)
    h = h_ref[...]
    n_mid = D_FF if kind == "ffn" else D_MODEL
    n_chunks = n_mid // MXU_COLS
    top = SUBLANES

    def rows_of(s, pos, n):
        return pl.ds(s + 2 * pos, n, stride=2)

    def put(slot, s, u):
        for j in range(MXU_COLS // LANES):
            uj = u[:, j * LANES:(j + 1) * LANES]
            u_ref[slot, j, rows_of(s, 0, top), :] = uj[bm + top:bm + 2 * top]
            u_ref[slot, j, rows_of(s, top, bm), :] = uj[0:bm]
            u_ref[slot, j, rows_of(s, top + bm, top), :] = uj[bm:bm + top]

    def conv3(slot, j, s, c0):
        k0 = ck_ref[0:1, c0:c0 + LANES]
        k1 = ck_ref[1:2, c0:c0 + LANES]
        k2 = ck_ref[2:3, c0:c0 + LANES]
        return (u_ref[slot, j, rows_of(s, top - 1, bm), :] * k0
                + u_ref[slot, j, rows_of(s, top, bm), :] * k1
                + u_ref[slot, j, rows_of(s, top + 1, bm), :] * k2)

    def up(ci):
        slot = ci
        cols = lambda s: slice(s * n_mid + ci * MXU_COLS, s * n_mid + (ci + 1) * MXU_COLS)
        if kind == "ffn":
            put(slot, 0, _dot(h, w1_ref[:, cols(0)]))
            put(slot, 1, _dot(h, w1_ref[:, cols(1)]))
        else:
            put(slot, 0, _dot(h, w1_ref[:, cols(0)]))
            put(slot, 1, _dot(h, w1_ref[:, cols(1)]) * _dot(h, w1_ref[:, cols(2)]))

    def mid(ci):
        slot = ci
        for j in range(MXU_COLS // LANES):
            c0 = ci * MXU_COLS + j * LANES
            if kind == "ffn":
                cv = conv3(slot, j, 0, c0) + cb_ref[:, c0:c0 + LANES]
                cg = conv3(slot, j, 1, D_FF + c0) + cb_ref[:, D_FF + c0:D_FF + c0 + LANES]
                z = _silu(cg) * cv
            else:
                z = u_ref[slot, j, rows_of(0, top, bm), :] * conv3(slot, j, 1, c0)
            z_ref[:, c0:c0 + LANES] = z.astype(BF16)

    up(0)
    for ci in range(n_chunks):
        if ci + 1 < n_chunks:
            up(ci + 1)
        mid(ci)
    out = x + gate * _dot(z_ref[...], w2_ref[...])
    if final_norm:
        ms = jnp.mean(out * out, axis=-1, keepdims=True)
        out = out * lax.rsqrt(ms + NORM_EPS) * fg_ref[...]
    o_ref[...] = out


def _convmlp(x, mod, norm_g, w1, ck, cb, w2, final_g, *, kind, ctx, final_norm=False):
    b, l, d = x.shape
    bm = min(l, FFN_TILE_ROWS if kind == "ffn" else TILE_ROWS)
    n_chunks = w2.shape[0] // MXU_COLS
    nt = l // bm
    hb = bm // SUBLANES
    last_hb = l // SUBLANES - 1
    kern = functools.partial(_convmlp_kernel, kind=kind, bm=bm,
                             mod_base=3 if kind == "ffn" else 0, final_norm=final_norm)
    return pl.pallas_call(
        kern,
        grid=(b, nt),
        in_specs=[
            pl.BlockSpec((None, bm, d), lambda i, t: (i, t, 0)),
            pl.BlockSpec((None, SUBLANES, d), lambda i, t: (i, jnp.maximum(t * hb - 1, 0), 0)),
            pl.BlockSpec((None, SUBLANES, d), lambda i, t: (i, jnp.minimum((t + 1) * hb, last_hb), 0)),
            _mod_spec(mod, ctx),
            _resident(norm_g),
            _resident(w1),
            _resident(ck),
            _resident(cb),
            _resident(w2),
            _resident(final_g),
        ],
        out_specs=pl.BlockSpec((None, bm, d), lambda i, t: (i, t, 0)),
        out_shape=jax.ShapeDtypeStruct((b, l, d), F32),
        scratch_shapes=[pltpu.VMEM((bm + BF16_ROWS, d), BF16),
                        pltpu.VMEM((n_chunks, MXU_COLS // LANES, 2 * (bm + 2 * SUBLANES), LANES), F32),
                        pltpu.VMEM((bm, w2.shape[0]), BF16)],
        compiler_params=_params(2),
        name="convmlp_" + kind,
    )(x, x, x, mod.stack, norm_g.stack, w1.stack, ck.stack, cb.stack, w2.stack, final_g.stack)


def _outproj_kernel(*refs, gated):
    if not gated:
        a_ref, x_ref, mod_ref, w_ref, o_ref = refs
        acc = _dot(a_ref[...], w_ref[...])
    else:
        y_ref, gt_ref, x_ref, mod_ref, w_ref, o_ref = refs

        def gate(hd):
            cols = slice(hd * RET_V_DIM, (hd + 1) * RET_V_DIM)
            y = y_ref[:, cols].astype(F32)
            ms = jnp.mean(y * y, axis=-1, keepdims=True)
            return (_silu(gt_ref[:, cols].astype(F32)) * (y * lax.rsqrt(ms + NORM_EPS))).astype(BF16)

        a_next = gate(0)
        acc = None
        for hd in range(RET_HEADS):
            a = a_next
            if hd + 1 < RET_HEADS:
                a_next = gate(hd + 1)
            part = _dot(a, w_ref[hd * RET_V_DIM:(hd + 1) * RET_V_DIM, :])
            acc = part if acc is None else acc + part
    o_ref[...] = x_ref[...] + mod_ref[2] * acc


def _outproj(acts, x, mod, w, *, ctx):
    b, l, d = x.shape
    bm = min(l, TILE_ROWS if len(acts) == 1 else TILE_ROWS // 2)
    act_specs = [pl.BlockSpec((None, bm, a.shape[-1]), lambda i, t: (i, t, 0)) for a in acts]
    return pl.pallas_call(
        functools.partial(_outproj_kernel, gated=len(acts) == 2),
        grid=(b, l // bm),
        in_specs=act_specs + [
            pl.BlockSpec((None, bm, d), lambda i, t: (i, t, 0)),
            _mod_spec(mod, ctx),
            _resident(w),
        ],
        out_specs=pl.BlockSpec((None, bm, d), lambda i, t: (i, t, 0)),
        out_shape=jax.ShapeDtypeStruct((b, l, d), F32),
        input_output_aliases={len(acts): 0},
        compiler_params=_params(2),
        name="outproj",
    )(*acts, x, mod.stack, w.stack)


def _head_meansq(v, bd_ref):
    return _dot((v * v).astype(BF16), bd_ref[...])


def _rope64(v, cos, s_up, s_dn):
    return v * cos + pltpu.roll(v, LANES - 16, axis=1) * s_up + pltpu.roll(v, 16, axis=1) * s_dn


def _qkv_kernel(x_ref, mod_ref, g_ref, w_ref, qg_ref, kg_ref, bd_ref, cos_ref, sup_ref, sdn_ref,
                q_ref, k_ref, v_ref, *, rope):
    h = _modulate(x_ref[...], g_ref[...], mod_ref[0], mod_ref[1]).astype(BF16)
    nq = N_Q_HEADS * HEAD_DIM
    nkv = N_KV_HEADS * HEAD_DIM
    n_norm = (nq + nkv) // MXU_COLS
    q_gain = qg_ref[...] * (HEAD_DIM ** -0.5 * LOG2_E)
    k_gain = kg_ref[...]
    heads_per_slab = LANES // HEAD_DIM

    def project(ci):
        return _dot(h, w_ref[:, ci * MXU_COLS:(ci + 1) * MXU_COLS])

    def finish(ci, u, ms):
        u = u * lax.rsqrt(ms + NORM_EPS)
        for j in range(MXU_COLS // LANES):
            s = u[:, j * LANES:(j + 1) * LANES] * (q_gain if ci < n_norm - 1 else k_gain)
            if rope:
                s = _rope64(s, cos_ref[...], sup_ref[...], sdn_ref[...])
            if ci < n_norm - 1:
                lo = ci * MXU_COLS + j * LANES
                q_ref[:, lo:lo + LANES] = s.astype(BF16)
            else:
                for e in range(heads_per_slab):
                    k_ref[j * heads_per_slab + e] = s[:, e * HEAD_DIM:(e + 1) * HEAD_DIM].astype(BF16)

    us = {0: project(0), 1: project(1)}
    mss = {0: _head_meansq(us[0], bd_ref)}
    for ci in range(n_norm):
        if ci + 2 <= n_norm:
            us[ci + 2] = project(ci + 2)
        if ci + 1 < n_norm:
            mss[ci + 1] = _head_meansq(us[ci + 1], bd_ref)
        finish(ci, us.pop(ci), mss.pop(ci))
    vv = us.pop(n_norm)
    ones = jnp.ones((vv.shape[0], HEAD_DIM), F32)
    for e in range(N_KV_HEADS):
        v_ref[e] = jnp.concatenate([vv[:, e * HEAD_DIM:(e + 1) * HEAD_DIM], ones], axis=-1).astype(BF16)


def _qkv(x, mod, norm_g, w, qg, kg, bd, cos, s_up, s_dn, *, ctx):
    b, l, d = x.shape
    bm = min(l, TILE_ROWS // 2)
    tab = pl.BlockSpec((bm, LANES), lambda i, t: (t, 0))
    kern = functools.partial(_qkv_kernel, rope=not ctx)
    return pl.pallas_call(
        kern,
        grid=(b, l // bm),
        in_specs=[
            pl.BlockSpec((None, bm, d), lambda i, t: (i, t, 0)),
            _mod_spec(mod, ctx),
            _resident(norm_g),
            _resident(w),
            _resident(qg),
            _resident(kg),
            _resident(bd),
            tab, tab, tab,
        ],
        out_specs=[
            pl.BlockSpec((None, bm, d), lambda i, t: (i, t, 0)),
            pl.BlockSpec((None, N_KV_HEADS, bm, HEAD_DIM), lambda i, t: (i, 0, t, 0)),
            pl.BlockSpec((None, N_KV_HEADS, bm, 2 * HEAD_DIM), lambda i, t: (i, 0, t, 0)),
        ],
        out_shape=[
            jax.ShapeDtypeStruct((b, l, d), BF16),
            jax.ShapeDtypeStruct((b, N_KV_HEADS, l, HEAD_DIM), BF16),
            jax.ShapeDtypeStruct((b, N_KV_HEADS, l, 2 * HEAD_DIM), BF16),
        ],
        compiler_params=_params(2),
        name="qkv",
    )(x, mod.stack, norm_g.stack, w.stack, qg.stack, kg.stack, bd.stack, cos, s_up, s_dn)


def _attn_kernel(*refs, n_sets):
    q_ref = refs[0]
    kv = refs[1:1 + 2 * n_sets]
    o_ref = refs[1 + 2 * n_sets]
    q = q_ref[...]
    ks = [kv[2 * i][...] for i in range(n_sets)]
    vs = [kv[2 * i + 1][...] for i in range(n_sets)]
    outs = []

    def scores(gi):
        qg = q[:, gi * HEAD_DIM:(gi + 1) * HEAD_DIM]
        return [_dot_nt(qg, k) for k in ks]

    ss_next = scores(0)
    for gi in range(GQA_GROUP):
        ss = ss_next
        if gi + 1 < GQA_GROUP:
            ss_next = scores(gi + 1)
        m = ss[0].max(axis=-1, keepdims=True)
        for s in ss[1:]:
            m = jnp.maximum(m, s.max(axis=-1, keepdims=True))
        acc = None
        for s, v in zip(ss, vs):
            pv = _dot(jnp.exp2(s - m).astype(BF16), v)
            acc = pv if acc is None else acc + pv
        o = acc * (1.0 / pltpu.roll(acc, HEAD_DIM, axis=1))
        outs.append(o[:, 0:HEAD_DIM])
    o_ref[...] = jnp.concatenate(outs, axis=-1).astype(BF16)


def _attention(q, kvs):
    b, l, d = q.shape
    bq = min(l, TILE_ROWS)
    width = GQA_GROUP * HEAD_DIM
    in_specs = [pl.BlockSpec((None, bq, width), lambda i, h, t: (i, t, h))]
    args = [q]
    for k, v in kvs:
        lk = k.shape[2]
        in_specs += [pl.BlockSpec((None, None, lk, HEAD_DIM), lambda i, h, t: (i, h, 0, 0)),
                     pl.BlockSpec((None, None, lk, 2 * HEAD_DIM), lambda i, h, t: (i, h, 0, 0))]
        args += [k, v]
    return pl.pallas_call(
        functools.partial(_attn_kernel, n_sets=len(kvs)),
        grid=(b, N_KV_HEADS, l // bq),
        in_specs=in_specs,
        out_specs=pl.BlockSpec((None, bq, width), lambda i, h, t: (i, t, h)),
        out_shape=jax.ShapeDtypeStruct((b, l, d), BF16),
        compiler_params=_params(3),
        name="attention",
    )(*args)


def _rope256(v, cos, sgn_sin):
    return v * cos + pltpu.roll(v, LANES // 2, axis=1) * sgn_sin


def _retproj_kernel(x_ref, mod_ref, g_ref, w_ref, cos_ref, sin_ref, *out_refs, ctx):
    h = _modulate(x_ref[...], g_ref[...], mod_ref[0], mod_ref[1]).astype(BF16)
    nqk = RET_HEADS * RET_HEAD_DIM
    nv = RET_HEADS * RET_V_DIM
    kscale = RET_HEAD_DIM ** -0.5
    col = nqk
    if ctx:
        k_ref, v_ref = out_refs
    else:
        q_ref, k_ref, v_ref, gt_ref = out_refs
        for ci in range(nqk // MXU_COLS):
            u = _dot(h, w_ref[:, ci * MXU_COLS:(ci + 1) * MXU_COLS])
            for j in range(MXU_COLS // LANES):
                sl = slice(j * LANES, (j + 1) * LANES)
                lo = ci * MXU_COLS + j * LANES
                q_ref[:, lo:lo + LANES] = _rope256(u[:, sl], cos_ref[:, sl], sin_ref[:, sl]).astype(BF16)
    for ci in range(nqk // MXU_COLS):
        u = _dot(h, w_ref[:, col + ci * MXU_COLS:col + (ci + 1) * MXU_COLS]) * kscale
        for j in range(MXU_COLS // LANES):
            sl = slice(j * LANES, (j + 1) * LANES)
            lo = ci * MXU_COLS + j * LANES
            s = u[:, sl]
            if not ctx:
                s = _rope256(s, cos_ref[:, sl], sin_ref[:, sl])
            k_ref[:, lo:lo + LANES] = s.astype(BF16)
    col += nqk
    for ci in range(nv // MXU_COLS):
        sl = slice(ci * MXU_COLS, (ci + 1) * MXU_COLS)
        v_ref[:, sl] = _dot(h, w_ref[:, col + ci * MXU_COLS:col + (ci + 1) * MXU_COLS]).astype(BF16)
    if not ctx:
        col += nv
        for ci in range(nv // MXU_COLS):
            sl = slice(ci * MXU_COLS, (ci + 1) * MXU_COLS)
            gt_ref[:, sl] = _dot(h, w_ref[:, col + ci * MXU_COLS:col + (ci + 1) * MXU_COLS]).astype(BF16)


def _retproj(x, mod, norm_g, w, cos, sin, *, ctx):
    b, l, d = x.shape
    bm = min(l, TILE_ROWS)
    nqk = RET_HEADS * RET_HEAD_DIM
    nv = RET_HEADS * RET_V_DIM
    row = lambda n: pl.BlockSpec((None, bm, n), lambda i, t: (i, t, 0))
    tab = pl.BlockSpec((bm, RET_HEAD_DIM), lambda i, t: (t, 0))
    widths = [nqk, nv] if ctx else [nqk, nqk, nv, nv]
    return pl.pallas_call(
        functools.partial(_retproj_kernel, ctx=ctx),
        grid=(b, l // bm),
        in_specs=[row(d), _mod_spec(mod, ctx), _resident(norm_g), _resident(w), tab, tab],
        out_specs=[row(n) for n in widths],
        out_shape=[jax.ShapeDtypeStruct((b, l, n), BF16) for n in widths],
        compiler_params=_params(2),
        name="retproj",
    )(x, mod.stack, norm_g.stack, w.stack, cos, sin)


def _retention_kernel(dec_ref, q_ref, k_ref, v_ref, kc_ref, vc_ref, o_ref, y_ref, *, seq, ctx_len):
    c = RET_BLOCK
    n_chunks = seq // c
    ii = lax.broadcasted_iota(jnp.int32, (c, c), 0)
    jj = lax.broadcasted_iota(jnp.int32, (c, c), 1)
    diff = (ii - jj).astype(F32)
    pos = lax.broadcasted_iota(jnp.int32, (c, 1), 0).astype(F32)
    posc = lax.broadcasted_iota(jnp.int32, (ctx_len, 1), 0).astype(F32)

    def outer(kd, v):
        return _dot(kd.T.astype(BF16), v)

    def rows(i):
        return slice(i * c, (i + 1) * c)

    class Head:
        def __init__(self, hh):
            hd = pl.program_id(1) * RET_HEADS_PER_STEP + hh

            def log_gamma(idx):
                dv = jnp.full((1, 1), dec_ref[idx], F32)
                return jnp.log1p(-jnp.exp2(-dv))

            lf = log_gamma(hd)
            lb = log_gamma(RET_HEADS + hd)
            self.hh = hh
            self.qk = slice(hh * RET_HEAD_DIM, (hh + 1) * RET_HEAD_DIM)
            self.vv = slice(hh * RET_V_DIM, (hh + 1) * RET_V_DIM)
            self.decay = jnp.where(diff >= 0, jnp.exp(diff * lf), jnp.exp(-diff * lb))
            self.qd_f = jnp.exp((pos + 1.0) * lf)
            self.kd_f = jnp.exp((c - 1.0 - pos) * lf)
            self.qd_b = jnp.exp((c - pos) * lb)
            self.kd_b = jnp.exp(pos * lb)
            self.cd_f = jnp.exp(c * lf)
            self.cd_b = jnp.exp(c * lb)
            kcf = kc_ref[:, self.qk].astype(F32)
            vc = vc_ref[:, self.vv]
            self.sf = outer(kcf * jnp.exp((ctx_len - 1.0 - posc) * lf), vc)
            self.sb = outer(kcf * jnp.exp(posc * lb), vc)

        def q(self, i):
            return q_ref[rows(i), self.qk]

        def k(self, i):
            return k_ref[rows(i), self.qk]

        def v(self, i):
            return v_ref[rows(i), self.vv]

        def intra(self, i):
            a = (_dot_nt(self.q(i), self.k(i)) * self.decay).astype(BF16)
            return _dot(a, self.v(i))

    heads = [Head(hh) for hh in range(RET_HEADS_PER_STEP)]
    for step in range(n_chunks):
        i = step
        j = n_chunks - 1 - step
        for h in heads:
            inter_f = _dot(h.q(i), h.sf.astype(BF16)) * h.qd_f
            inter_b = _dot(h.q(j), h.sb.astype(BF16)) * h.qd_b
            if i < j:
                y_ref[h.hh, rows(i), :] = h.intra(i) + inter_f
                y_ref[h.hh, rows(j), :] = h.intra(j) + inter_b
            else:
                o_ref[rows(i), h.vv] = (y_ref[h.hh, rows(i), :] + inter_f).astype(BF16)
                o_ref[rows(j), h.vv] = (y_ref[h.hh, rows(j), :] + inter_b).astype(BF16)
            if step < n_chunks - 1:
                h.sf = h.cd_f * h.sf + outer(h.k(i).astype(F32) * h.kd_f, h.v(i))
                h.sb = h.cd_b * h.sb + outer(h.k(j).astype(F32) * h.kd_b, h.v(j))


def _retention(dec, q, k, v, kc, vc):
    b, l, _ = q.shape
    lc = kc.shape[1]
    hs = RET_HEADS_PER_STEP
    qk = lambda n: pl.BlockSpec((None, n, hs * RET_HEAD_DIM), lambda i, h: (i, 0, h))
    vv = lambda n: pl.BlockSpec((None, n, hs * RET_V_DIM), lambda i, h: (i, 0, h))
    return pl.pallas_call(
        functools.partial(_retention_kernel, seq=l, ctx_len=lc),
        grid=(b, RET_HEADS // hs),
        in_specs=[pl.BlockSpec(memory_space=pltpu.SMEM),
                  qk(l), qk(l), vv(l), qk(lc), vv(lc)],
        out_specs=vv(l),
        out_shape=jax.ShapeDtypeStruct((b, l, RET_HEADS * RET_V_DIM), BF16),
        scratch_shapes=[pltpu.VMEM((hs, l, RET_V_DIM), F32)],
        compiler_params=_params(2),
        name="retention",
    )(dec, q, k, v, kc, vc)


def _rope_tables(seq, head_dim):
    quarter = head_dim // 4
    rows = jnp.repeat(jnp.arange(seq // GRID_W, dtype=F32), GRID_W)
    cols = jnp.tile(jnp.arange(GRID_W, dtype=F32), seq // GRID_W)
    inv = ROPE_THETA ** (-jnp.arange(quarter, dtype=F32) / quarter)
    ang = jnp.stack([rows[:, None] * inv, cols[:, None] * inv], axis=1)
    cos = jnp.cos(ang)
    sin = jnp.sin(ang)
    cos_t = jnp.stack([cos, cos], axis=2).reshape(seq, head_dim)
    sin_first = jnp.stack([-sin, jnp.zeros_like(sin)], axis=2).reshape(seq, head_dim)
    sin_second = jnp.stack([jnp.zeros_like(sin), sin], axis=2).reshape(seq, head_dim)
    return cos_t, sin_first, sin_second


def kernel(x, c, ctx, c_ctx, ada_w, ada_b, norm_mix_g, norm_ffn_g, final_norm_g, conv_w_in, conv_k, conv_w_out, attn_w_qkv, attn_q_norm_g, attn_k_norm_g, attn_w_out, ret_w_in, ret_decay, ret_w_out, ffn_w_up, ffn_conv_k, ffn_conv_b, ffn_w_down):
    batch, seq, d = x.shape
    assert d == D_MODEL and seq % (2 * RET_BLOCK) == 0 and ctx.shape[1] % SUBLANES == 0

    cvec = jnp.zeros((MOD_ROWS, d), F32).at[:batch].set(c).at[CTX_ROW].set(c_ctx)
    mod_all = _ada(cvec, ada_w, ada_b).reshape(DEPTH, MOD_ROWS, 6, 1, d)

    a_cos, a_first, a_second = _rope_tables(seq, HEAD_DIM)
    rep = LANES // HEAD_DIM
    a_cos, a_first, a_second = (jnp.tile(t, (1, rep)) for t in (a_cos, a_first, a_second))
    r_cos, r_first, r_second = _rope_tables(seq, RET_HEAD_DIM)
    r_sin = r_first + r_second
    head_sum = jnp.kron(jnp.eye(MXU_COLS // HEAD_DIM, dtype=F32),
                        jnp.full((HEAD_DIM, HEAD_DIM), 1.0 / HEAD_DIM, F32)).astype(BF16)[None]

    rows = lambda v: v.reshape(v.shape[0], 1, -1)
    norm_mix, norm_ffn = rows(norm_mix_g), rows(norm_ffn_g)
    final_g = _Layer(final_norm_g.reshape(1, 1, d), 0)
    mix_w1, mix_w2 = conv_w_in.astype(BF16), conv_w_out.astype(BF16)
    no_bias = _Layer(jnp.zeros((1, 1, d), F32), 0)
    att_wq, att_wo = attn_w_qkv.astype(BF16), attn_w_out.astype(BF16)
    att_qg = rows(jnp.tile(attn_q_norm_g, (1, rep)))
    att_kg = rows(jnp.tile(attn_k_norm_g, (1, rep)))
    ret_wi, ret_wo = ret_w_in.astype(BF16), ret_w_out.astype(BF16)
    ffn_w1, ffn_w2, ffn_cb = ffn_w_up.astype(BF16), ffn_w_down.astype(BF16), rows(ffn_conv_b)

    kinds = [i % N_MIXERS for i in range(DEPTH)]
    reads_ctx = [kd in (1, 2) for kd in kinds]
    cx = ctx
    for i in range(DEPTH):
        kind = kinds[i]
        j = i // N_MIXERS
        ctx_out = any(reads_ctx[i + 1:])
        mod = _Layer(mod_all, i)
        ng = _Layer(norm_mix, i)
        last = i == DEPTH - 1
        if kind == 0:
            mix = functools.partial(_convmlp, mod=mod, norm_g=ng, w1=_Layer(mix_w1, j), ck=_Layer(conv_k, j),
                                    cb=no_bias, w2=_Layer(mix_w2, j), final_g=final_g, kind="mix")
            x = mix(x, ctx=False)
            if ctx_out:
                cx = mix(cx, ctx=True)
        elif kind == 1:
            wo = _Layer(att_wo, j)
            proj = functools.partial(_qkv, mod=mod, norm_g=ng, w=_Layer(att_wq, j), qg=_Layer(att_qg, j),
                                     kg=_Layer(att_kg, j), bd=_Layer(head_sum, 0),
                                     cos=a_cos, s_up=a_first, s_dn=a_second)
            q, k, v = proj(x, ctx=False)
            qc, kc, vc = proj(cx, ctx=True)
            o = _attention(q, [(k, v), (kc, vc)])
            x = _outproj([o], x, mod, wo, ctx=False)
            if ctx_out:
                oc = _attention(qc, [(kc, vc)])
                cx = _outproj([oc], cx, mod, wo, ctx=True)
        else:
            wi = _Layer(ret_wi, j)
            q, k, v, gt = _retproj(x, mod, ng, wi, r_cos, r_sin, ctx=False)
            kc, vc = _retproj(cx, mod, ng, wi, r_cos, r_sin, ctx=True)
            y = _retention(ret_decay[j].reshape(-1), q, k, v, kc, vc)
            x = _outproj([y, gt], x, mod, _Layer(ret_wo, j), ctx=False)
            assert not ctx_out
        ffn = functools.partial(_convmlp, mod=mod, norm_g=_Layer(norm_ffn, i), w1=_Layer(ffn_w1, i),
                                ck=_Layer(ffn_conv_k, i), cb=_Layer(ffn_cb, i), w2=_Layer(ffn_w2, i),
                                final_g=final_g, kind="ffn")
        x = ffn(x, ctx=False, final_norm=last)
        if ctx_out:
            cx = ffn(cx, ctx=True)
    return x
```

```python
import functools
from typing import NamedTuple

import jax
import jax.numpy as jnp
from jax import lax
from jax.experimental import pallas as pl
from jax.experimental.pallas import tpu as pltpu

F32 = jnp.float32
BF16 = jnp.bfloat16

D_MODEL = 1024
DEPTH = 4
GRID_W = 64
N_MIXERS = 3
HEAD_DIM = 64
N_Q_HEADS = D_MODEL // HEAD_DIM
N_KV_HEADS = N_Q_HEADS // 4
GQA_GROUP = N_Q_HEADS // N_KV_HEADS
ROPE_THETA = 10000.0
RET_HEAD_DIM = 256
RET_HEADS = D_MODEL // RET_HEAD_DIM
RET_V_DIM = 2 * RET_HEAD_DIM
RET_BLOCK = 256
RET_HEADS_PER_STEP = 2
D_FF = ((8 * D_MODEL // 3 + 127) // 128) * 128
NORM_EPS = 1e-6
LOG2_E = 1.4426950408889634

LANES = 128
SUBLANES = 8
BF16_ROWS = 16
MXU_COLS = 256
VMEM_LIMIT = 56 * 1024 * 1024

FFN_TILE_ROWS = 512
TILE_ROWS = 1024
MOD_ROWS = 16
CTX_ROW = 8


def _params(n_axes):
    return pltpu.CompilerParams(dimension_semantics=("arbitrary",) * n_axes,
                                vmem_limit_bytes=VMEM_LIMIT)


class _Layer(NamedTuple):
    stack: jax.Array
    index: int

    @property
    def shape(self):
        return self.stack.shape[1:]


def _resident(p):
    zeros = (0,) * len(p.shape)
    return pl.BlockSpec((None,) + p.shape, lambda *_: (p.index,) + zeros,
                        pipeline_mode=pl.Buffered(1))


def _silu(v):
    return v * (1.0 / (1.0 + jnp.exp(-v)))


def _modulate(xf, g, shift, scale):
    ms = jnp.mean(xf * xf, axis=-1, keepdims=True)
    y = xf * lax.rsqrt(ms + NORM_EPS) * g
    return y * (1.0 + scale) + shift


def _dot(a, b):
    return jnp.dot(a, b, preferred_element_type=F32)


def _dot_nt(a, b):
    return lax.dot_general(a, b, (((1,), (1,)), ((), ())), preferred_element_type=F32)


def _ada_kernel(c_ref, w_ref, b_ref, o_ref):
    s = _silu(c_ref[...]).astype(BF16)
    o_ref[...] = _dot(s, w_ref[...].astype(BF16)) + b_ref[...]


def _ada(cvec, ada_w, ada_b):
    depth, d, n = ada_w.shape
    tn = 1536
    return pl.pallas_call(
        _ada_kernel,
        grid=(depth, n // tn),
        in_specs=[
            pl.BlockSpec((MOD_ROWS, d), lambda l, j: (0, 0)),
            pl.BlockSpec((None, d, tn), lambda l, j: (l, 0, j)),
            pl.BlockSpec((None, 1, tn), lambda l, j: (l, 0, j)),
        ],
        out_specs=pl.BlockSpec((None, MOD_ROWS, tn), lambda l, j: (l, 0, j)),
        out_shape=jax.ShapeDtypeStruct((depth, MOD_ROWS, n), F32),
        compiler_params=_params(2),
        name="ada",
    )(cvec, ada_w, ada_b.reshape(depth, 1, n))


def _mod_spec(mod, ctx):
    blk = (None, None, 6, 1, D_MODEL)
    if ctx:
        return pl.BlockSpec(blk, lambda b, t: (mod.index, CTX_ROW, 0, 0, 0))
    return pl.BlockSpec(blk, lambda b, t: (mod.index, b, 0, 0, 0))


def _convmlp_kernel(x_ref, xp_ref, xn_ref, mod_ref, g_ref, w1_ref, ck_ref, cb_ref, w2_ref, fg_ref,
                    o_ref, h_ref, u_ref, z_ref, *, kind, bm, mod_base, final_norm):
    t = pl.program_id(1)
    nt = pl.num_programs(1)
    shift = mod_ref[mod_base]
    scale = mod_ref[mod_base + 1]
    gate = mod_ref[mod_base + 2]
    g = g_ref[...]
    x = x_ref[...]
    h_ref[0:bm, :] = _modulate(x, g, shift, scale).astype(BF16)
    keep_n = (t < nt - 1).astype(F32)
    keep_p = (t > 0).astype(F32)
    halo = jnp.concatenate([_modulate(xn_ref[...], g, shift, scale) * keep_n,
                            _modulate(xp_ref[...], g, shift, scale) * keep_p], axis=0)
    h_ref[bm:bm + BF16_ROWS, :] = halo.astype(BF16)
    h = h_ref[...]
    n_mid = D_FF if kind == "ffn" else D_MODEL
    n_chunks = n_mid // MXU_COLS
    top = SUBLANES

    def rows_of(s, pos, n):
        return pl.ds(s + 2 * pos, n, stride=2)

    def put(slot, s, u):
        for j in range(MXU_COLS // LANES):
            uj = u[:, j * LANES:(j + 1) * LANES]
            u_ref[slot, j, rows_of(s, 0, top), :] = uj[bm + top:bm + 2 * top]
            u_ref[slot, j, rows_of(s, top, bm), :] = uj[0:bm]
            u_ref[slot, j, rows_of(s, top + bm, top), :] = uj[bm:bm + top]

    def conv3(slot, j, s, c0):
        k0 = ck_ref[0:1, c0:c0 + LANES]
        k1 = ck_ref[1:2, c0:c0 + LANES]
        k2 = ck_ref[2:3, c0:c0 + LANES]
        return (u_ref[slot, j, rows_of(s, top - 1, bm), :] * k0
                + u_ref[slot, j, rows_of(s, top, bm), :] * k1
                + u_ref[slot, j, rows_of(s, top + 1, bm), :] * k2)

    def up(ci):
        slot = ci
        cols = lambda s: slice(s * n_mid + ci * MXU_COLS, s * n_mid + (ci + 1) * MXU_COLS)
        if kind == "ffn":
            put(slot, 0, _dot(h, w1_ref[:, cols(0)]))
            put(slot, 1, _dot(h, w1_ref[:, cols(1)]))
        else:
            put(slot, 0, _dot(h, w1_ref[:, cols(0)]))
            put(slot, 1, _dot(h, w1_ref[:, cols(1)]) * _dot(h, w1_ref[:, cols(2)]))

    def mid(ci):
        slot = ci
        for j in range(MXU_COLS // LANES):
            c0 = ci * MXU_COLS + j * LANES
            if kind == "ffn":
                cv = conv3(slot, j, 0, c0) + cb_ref[:, c0:c0 + LANES]
                cg = conv3(slot, j, 1, D_FF + c0) + cb_ref[:, D_FF + c0:D_FF + c0 + LANES]
                z = _silu(cg) * cv
            else:
                z = u_ref[slot, j, rows_of(0, top, bm), :] * conv3(slot, j, 1, c0)
            z_ref[:, c0:c0 + LANES] = z.astype(BF16)

    up(0)
    for ci in range(n_chunks):
        if ci + 1 < n_chunks:
            up(ci + 1)
        mid(ci)
    out = x + gate * _dot(z_ref[...], w2_ref[...])
    if final_norm:
        ms = jnp.mean(out * out, axis=-1, keepdims=True)
        out = out * lax.rsqrt(ms + NORM_EPS) * fg_ref[...]
    o_ref[...] = out


def _convmlp(x, mod, norm_g, w1, ck, cb, w2, final_g, *, kind, ctx, final_norm=False):
    b, l, d = x.shape
    bm = min(l, FFN_TILE_ROWS if kind == "ffn" else TILE_ROWS)
    n_chunks = w2.shape[0] // MXU_COLS
    nt = l // bm
    hb = bm // SUBLANES
    last_hb = l // SUBLANES - 1
    kern = functools.partial(_convmlp_kernel, kind=kind, bm=bm,
                             mod_base=3 if kind == "ffn" else 0, final_norm=final_norm)
    return pl.pallas_call(
        kern,
        grid=(b, nt),
        in_specs=[
            pl.BlockSpec((None, bm, d), lambda i, t: (i, t, 0)),
            pl.BlockSpec((None, SUBLANES, d), lambda i, t: (i, jnp.maximum(t * hb - 1, 0), 0)),
            pl.BlockSpec((None, SUBLANES, d), lambda i, t: (i, jnp.minimum((t + 1) * hb, last_hb), 0)),
            _mod_spec(mod, ctx),
            _resident(norm_g),
            _resident(w1),
            _resident(ck),
            _resident(cb),
            _resident(w2),
            _resident(final_g),
        ],
        out_specs=pl.BlockSpec((None, bm, d), lambda i, t: (i, t, 0)),
        out_shape=jax.ShapeDtypeStruct((b, l, d), F32),
        scratch_shapes=[pltpu.VMEM((bm + BF16_ROWS, d), BF16),
                        pltpu.VMEM((n_chunks, MXU_COLS // LANES, 2 * (bm + 2 * SUBLANES), LANES), F32),
                        pltpu.VMEM((bm, w2.shape[0]), BF16)],
        compiler_params=_params(2),
        name="convmlp_" + kind,
    )(x, x, x, mod.stack, norm_g.stack, w1.stack, ck.stack, cb.stack, w2.stack, final_g.stack)


def _outproj_kernel(*refs, gated):
    if not gated:
        a_ref, x_ref, mod_ref, w_ref, o_ref = refs
        acc = _dot(a_ref[...], w_ref[...])
    else:
        y_ref, gt_ref, x_ref, mod_ref, w_ref, o_ref = refs

        def gate(hd):
            cols = slice(hd * RET_V_DIM, (hd + 1) * RET_V_DIM)
            y = y_ref[:, cols].astype(F32)
            ms = jnp.mean(y * y, axis=-1, keepdims=True)
            return (_silu(gt_ref[:, cols].astype(F32)) * (y * lax.rsqrt(ms + NORM_EPS))).astype(BF16)

        a_next = gate(0)
        acc = None
        for hd in range(RET_HEADS):
            a = a_next
            if hd + 1 < RET_HEADS:
                a_next = gate(hd + 1)
            part = _dot(a, w_ref[hd * RET_V_DIM:(hd + 1) * RET_V_DIM, :])
            acc = part if acc is None else acc + part
    o_ref[...] = x_ref[...] + mod_ref[2] * acc


def _outproj(acts, x, mod, w, *, ctx):
    b, l, d = x.shape
    bm = min(l, 2 * TILE_ROWS if len(acts) == 1 else TILE_ROWS // 2)
    act_specs = [pl.BlockSpec((None, bm, a.shape[-1]), lambda i, t: (i, t, 0)) for a in acts]
    return pl.pallas_call(
        functools.partial(_outproj_kernel, gated=len(acts) == 2),
        grid=(b, l // bm),
        in_specs=act_specs + [
            pl.BlockSpec((None, bm, d), lambda i, t: (i, t, 0)),
            _mod_spec(mod, ctx),
            _resident(w),
        ],
        out_specs=pl.BlockSpec((None, bm, d), lambda i, t: (i, t, 0)),
        out_shape=jax.ShapeDtypeStruct((b, l, d), F32),
        input_output_aliases={len(acts): 0},
        compiler_params=_params(2),
        name="outproj",
    )(*acts, x, mod.stack, w.stack)


def _head_meansq(v, bd_ref):
    return _dot((v * v).astype(BF16), bd_ref[...])


def _rope64(v, cos, s_up, s_dn):
    return v * cos + pltpu.roll(v, LANES - 16, axis=1) * s_up + pltpu.roll(v, 16, axis=1) * s_dn


def _qkv_kernel(x_ref, mod_ref, g_ref, w_ref, qg_ref, kg_ref, bd_ref, cos_ref, sup_ref, sdn_ref,
                q_ref, k_ref, v_ref, *, rope):
    h = _modulate(x_ref[...], g_ref[...], mod_ref[0], mod_ref[1]).astype(BF16)
    nq = N_Q_HEADS * HEAD_DIM
    nkv = N_KV_HEADS * HEAD_DIM
    n_norm = (nq + nkv) // MXU_COLS
    q_gain = qg_ref[...] * (HEAD_DIM ** -0.5 * LOG2_E)
    k_gain = kg_ref[...]
    heads_per_slab = LANES // HEAD_DIM

    def project(ci):
        return _dot(h, w_ref[:, ci * MXU_COLS:(ci + 1) * MXU_COLS])

    def finish(ci, u, ms):
        u = u * lax.rsqrt(ms + NORM_EPS)
        for j in range(MXU_COLS // LANES):
            s = u[:, j * LANES:(j + 1) * LANES] * (q_gain if ci < n_norm - 1 else k_gain)
            if rope:
                s = _rope64(s, cos_ref[...], sup_ref[...], sdn_ref[...])
            if ci < n_norm - 1:
                lo = ci * MXU_COLS + j * LANES
                q_ref[:, lo:lo + LANES] = s.astype(BF16)
            else:
                for e in range(heads_per_slab):
                    k_ref[j * heads_per_slab + e] = s[:, e * HEAD_DIM:(e + 1) * HEAD_DIM].astype(BF16)

    us = {0: project(0), 1: project(1)}
    mss = {0: _head_meansq(us[0], bd_ref)}
    for ci in range(n_norm):
        if ci + 2 <= n_norm:
            us[ci + 2] = project(ci + 2)
        if ci + 1 < n_norm:
            mss[ci + 1] = _head_meansq(us[ci + 1], bd_ref)
        finish(ci, us.pop(ci), mss.pop(ci))
    vv = us.pop(n_norm)
    ones = jnp.ones((vv.shape[0], HEAD_DIM), F32)
    for e in range(N_KV_HEADS):
        v_ref[e] = jnp.concatenate([vv[:, e * HEAD_DIM:(e + 1) * HEAD_DIM], ones], axis=-1).astype(BF16)


def _qkv(x, mod, norm_g, w, qg, kg, bd, cos, s_up, s_dn, *, ctx):
    b, l, d = x.shape
    bm = min(l, TILE_ROWS // 2)
    tab = pl.BlockSpec((bm, LANES), lambda i, t: (t, 0))
    kern = functools.partial(_qkv_kernel, rope=not ctx)
    return pl.pallas_call(
        kern,
        grid=(b, l // bm),
        in_specs=[
            pl.BlockSpec((None, bm, d), lambda i, t: (i, t, 0)),
            _mod_spec(mod, ctx),
            _resident(norm_g),
            _resident(w),
            _resident(qg),
            _resident(kg),
            _resident(bd),
            tab, tab, tab,
        ],
        out_specs=[
            pl.BlockSpec((None, bm, d), lambda i, t: (i, t, 0)),
            pl.BlockSpec((None, N_KV_HEADS, bm, HEAD_DIM), lambda i, t: (i, 0, t, 0)),
            pl.BlockSpec((None, N_KV_HEADS, bm, 2 * HEAD_DIM), lambda i, t: (i, 0, t, 0)),
        ],
        out_shape=[
            jax.ShapeDtypeStruct((b, l, d), BF16),
            jax.ShapeDtypeStruct((b, N_KV_HEADS, l, HEAD_DIM), BF16),
            jax.ShapeDtypeStruct((b, N_KV_HEADS, l, 2 * HEAD_DIM), BF16),
        ],
        compiler_params=_params(2),
        name="qkv",
    )(x, mod.stack, norm_g.stack, w.stack, qg.stack, kg.stack, bd.stack, cos, s_up, s_dn)


def _attn_kernel(*refs, n_sets):
    q_ref = refs[0]
    kv = refs[1:1 + 2 * n_sets]
    o_ref = refs[1 + 2 * n_sets]
    q = q_ref[...]
    ks = [kv[2 * i][...] for i in range(n_sets)]
    vs = [kv[2 * i + 1][...] for i in range(n_sets)]
    outs = []

    def scores(gi):
        qg = q[:, gi * HEAD_DIM:(gi + 1) * HEAD_DIM]
        return [_dot_nt(qg, k) for k in ks]

    ss_next = scores(0)
    for gi in range(GQA_GROUP):
        ss = ss_next
        if gi + 1 < GQA_GROUP:
            ss_next = scores(gi + 1)
        m = ss[0].max(axis=-1, keepdims=True)
        for s in ss[1:]:
            m = jnp.maximum(m, s.max(axis=-1, keepdims=True))
        acc = None
        for s, v in zip(ss, vs):
            pv = _dot(jnp.exp2(s - m).astype(BF16), v)
            acc = pv if acc is None else acc + pv
        o = acc * (1.0 / pltpu.roll(acc, HEAD_DIM, axis=1))
        outs.append(o[:, 0:HEAD_DIM])
    o_ref[...] = jnp.concatenate(outs, axis=-1).astype(BF16)


def _attention(q, kvs):
    b, l, d = q.shape
    bq = min(l, TILE_ROWS)
    width = GQA_GROUP * HEAD_DIM
    in_specs = [pl.BlockSpec((None, bq, width), lambda i, h, t: (i, t, h))]
    args = [q]
    for k, v in kvs:
        lk = k.shape[2]
        in_specs += [pl.BlockSpec((None, None, lk, HEAD_DIM), lambda i, h, t: (i, h, 0, 0)),
                     pl.BlockSpec((None, None, lk, 2 * HEAD_DIM), lambda i, h, t: (i, h, 0, 0))]
        args += [k, v]
    return pl.pallas_call(
        functools.partial(_attn_kernel, n_sets=len(kvs)),
        grid=(b, N_KV_HEADS, l // bq),
        in_specs=in_specs,
        out_specs=pl.BlockSpec((None, bq, width), lambda i, h, t: (i, t, h)),
        out_shape=jax.ShapeDtypeStruct((b, l, d), BF16),
        compiler_params=_params(3),
        name="attention",
    )(*args)


def _rope256(v, cos, sgn_sin):
    return v * cos + pltpu.roll(v, LANES // 2, axis=1) * sgn_sin


def _retproj_kernel(x_ref, mod_ref, g_ref, w_ref, cos_ref, sin_ref, *out_refs, ctx):
    h = _modulate(x_ref[...], g_ref[...], mod_ref[0], mod_ref[1]).astype(BF16)
    nqk = RET_HEADS * RET_HEAD_DIM
    nv = RET_HEADS * RET_V_DIM
    kscale = RET_HEAD_DIM ** -0.5
    col = nqk
    if ctx:
        k_ref, v_ref = out_refs
    else:
        q_ref, k_ref, v_ref, gt_ref = out_refs
        for ci in range(nqk // MXU_COLS):
            u = _dot(h, w_ref[:, ci * MXU_COLS:(ci + 1) * MXU_COLS])
            for j in range(MXU_COLS // LANES):
                sl = slice(j * LANES, (j + 1) * LANES)
                lo = ci * MXU_COLS + j * LANES
                q_ref[:, lo:lo + LANES] = _rope256(u[:, sl], cos_ref[:, sl], sin_ref[:, sl]).astype(BF16)
    for ci in range(nqk // MXU_COLS):
        u = _dot(h, w_ref[:, col + ci * MXU_COLS:col + (ci + 1) * MXU_COLS]) * kscale
        for j in range(MXU_COLS // LANES):
            sl = slice(j * LANES, (j + 1) * LANES)
            lo = ci * MXU_COLS + j * LANES
            s = u[:, sl]
            if not ctx:
                s = _rope256(s, cos_ref[:, sl], sin_ref[:, sl])
            k_ref[:, lo:lo + LANES] = s.astype(BF16)
    col += nqk
    for ci in range(nv // MXU_COLS):
        sl = slice(ci * MXU_COLS, (ci + 1) * MXU_COLS)
        v_ref[:, sl] = _dot(h, w_ref[:, col + ci * MXU_COLS:col + (ci + 1) * MXU_COLS]).astype(BF16)
    if not ctx:
        col += nv
        for ci in range(nv // MXU_COLS):
            sl = slice(ci * MXU_COLS, (ci + 1) * MXU_COLS)
            gt_ref[:, sl] = _dot(h, w_ref[:, col + ci * MXU_COLS:col + (ci + 1) * MXU_COLS]).astype(BF16)


def _retproj(x, mod, norm_g, w, cos, sin, *, ctx):
    b, l, d = x.shape
    bm = min(l, TILE_ROWS)
    nqk = RET_HEADS * RET_HEAD_DIM
    nv = RET_HEADS * RET_V_DIM
    row = lambda n: pl.BlockSpec((None, bm, n), lambda i, t: (i, t, 0))
    tab = pl.BlockSpec((bm, RET_HEAD_DIM), lambda i, t: (t, 0))
    widths = [nqk, nv] if ctx else [nqk, nqk, nv, nv]
    return pl.pallas_call(
        functools.partial(_retproj_kernel, ctx=ctx),
        grid=(b, l // bm),
        in_specs=[row(d), _mod_spec(mod, ctx), _resident(norm_g), _resident(w), tab, tab],
        out_specs=[row(n) for n in widths],
        out_shape=[jax.ShapeDtypeStruct((b, l, n), BF16) for n in widths],
        compiler_params=_params(2),
        name="retproj",
    )(x, mod.stack, norm_g.stack, w.stack, cos, sin)


def _retention_kernel(dec_ref, q_ref, k_ref, v_ref, kc_ref, vc_ref, o_ref, y_ref, *, seq, ctx_len):
    c = RET_BLOCK
    n_chunks = seq // c
    ii = lax.broadcasted_iota(jnp.int32, (c, c), 0)
    jj = lax.broadcasted_iota(jnp.int32, (c, c), 1)
    diff = (ii - jj).astype(F32)
    pos = lax.broadcasted_iota(jnp.int32, (c, 1), 0).astype(F32)
    posc = lax.broadcasted_iota(jnp.int32, (ctx_len, 1), 0).astype(F32)

    def outer(kd, v):
        return _dot(kd.T.astype(BF16), v)

    def rows(i):
        return slice(i * c, (i + 1) * c)

    class Head:
        def __init__(self, hh):
            hd = pl.program_id(1) * RET_HEADS_PER_STEP + hh

            def log_gamma(idx):
                dv = jnp.full((1, 1), dec_ref[idx], F32)
                return jnp.log1p(-jnp.exp2(-dv))

            lf = log_gamma(hd)
            lb = log_gamma(RET_HEADS + hd)
            self.hh = hh
            self.qk = slice(hh * RET_HEAD_DIM, (hh + 1) * RET_HEAD_DIM)
            self.vv = slice(hh * RET_V_DIM, (hh + 1) * RET_V_DIM)
            self.decay = jnp.where(diff >= 0, jnp.exp(diff * lf), jnp.exp(-diff * lb))
            self.qd_f = jnp.exp((pos + 1.0) * lf)
            self.kd_f = jnp.exp((c - 1.0 - pos) * lf)
            self.qd_b = jnp.exp((c - pos) * lb)
            self.kd_b = jnp.exp(pos * lb)
            self.cd_f = jnp.exp(c * lf)
            self.cd_b = jnp.exp(c * lb)
            kcf = kc_ref[:, self.qk].astype(F32)
            vc = vc_ref[:, self.vv]
            self.sf = outer(kcf * jnp.exp((ctx_len - 1.0 - posc) * lf), vc)
            self.sb = outer(kcf * jnp.exp(posc * lb), vc)

        def q(self, i):
            return q_ref[rows(i), self.qk]

        def k(self, i):
            return k_ref[rows(i), self.qk]

        def v(self, i):
            return v_ref[rows(i), self.vv]

        def intra(self, i):
            a = (_dot_nt(self.q(i), self.k(i)) * self.decay).astype(BF16)
            return _dot(a, self.v(i))

    heads = [Head(hh) for hh in range(RET_HEADS_PER_STEP)]
    for step in range(n_chunks):
        i = step
        j = n_chunks - 1 - step
        for h in heads:
            inter_f = _dot(h.q(i), h.sf.astype(BF16)) * h.qd_f
            inter_b = _dot(h.q(j), h.sb.astype(BF16)) * h.qd_b
            if i < j:
                y_ref[h.hh, rows(i), :] = h.intra(i) + inter_f
                y_ref[h.hh, rows(j), :] = h.intra(j) + inter_b
            else:
                o_ref[rows(i), h.vv] = (y_ref[h.hh, rows(i), :] + inter_f).astype(BF16)
                o_ref[rows(j), h.vv] = (y_ref[h.hh, rows(j), :] + inter_b).astype(BF16)
            if step < n_chunks - 1:
                h.sf = h.cd_f * h.sf + outer(h.k(i).astype(F32) * h.kd_f, h.v(i))
                h.sb = h.cd_b * h.sb + outer(h.k(j).astype(F32) * h.kd_b, h.v(j))


def _retention(dec, q, k, v, kc, vc):
    b, l, _ = q.shape
    lc = kc.shape[1]
    hs = RET_HEADS_PER_STEP
    qk = lambda n: pl.BlockSpec((None, n, hs * RET_HEAD_DIM), lambda i, h: (i, 0, h))
    vv = lambda n: pl.BlockSpec((None, n, hs * RET_V_DIM), lambda i, h: (i, 0, h))
    return pl.pallas_call(
        functools.partial(_retention_kernel, seq=l, ctx_len=lc),
        grid=(b, RET_HEADS // hs),
        in_specs=[pl.BlockSpec(memory_space=pltpu.SMEM),
                  qk(l), qk(l), vv(l), qk(lc), vv(lc)],
        out_specs=vv(l),
        out_shape=jax.ShapeDtypeStruct((b, l, RET_HEADS * RET_V_DIM), BF16),
        scratch_shapes=[pltpu.VMEM((hs, l, RET_V_DIM), F32)],
        compiler_params=_params(2),
        name="retention",
    )(dec, q, k, v, kc, vc)


def _rope_tables(seq, head_dim):
    quarter = head_dim // 4
    rows = jnp.repeat(jnp.arange(seq // GRID_W, dtype=F32), GRID_W)
    cols = jnp.tile(jnp.arange(GRID_W, dtype=F32), seq // GRID_W)
    inv = ROPE_THETA ** (-jnp.arange(quarter, dtype=F32) / quarter)
    ang = jnp.stack([rows[:, None] * inv, cols[:, None] * inv], axis=1)
    cos = jnp.cos(ang)
    sin = jnp.sin(ang)
    cos_t = jnp.stack([cos, cos], axis=2).reshape(seq, head_dim)
    sin_first = jnp.stack([-sin, jnp.zeros_like(sin)], axis=2).reshape(seq, head_dim)
    sin_second = jnp.stack([jnp.zeros_like(sin), sin], axis=2).reshape(seq, head_dim)
    return cos_t, sin_first, sin_second


def kernel(x, c, ctx, c_ctx, ada_w, ada_b, norm_mix_g, norm_ffn_g, final_norm_g, conv_w_in, conv_k, conv_w_out, attn_w_qkv, attn_q_norm_g, attn_k_norm_g, attn_w_out, ret_w_in, ret_decay, ret_w_out, ffn_w_up, ffn_conv_k, ffn_conv_b, ffn_w_down):
    batch, seq, d = x.shape
    assert d == D_MODEL and seq % (2 * RET_BLOCK) == 0 and ctx.shape[1] % SUBLANES == 0

    cvec = jnp.zeros((MOD_ROWS, d), F32).at[:batch].set(c).at[CTX_ROW].set(c_ctx)
    mod_all = _ada(cvec, ada_w, ada_b).reshape(DEPTH, MOD_ROWS, 6, 1, d)

    a_cos, a_first, a_second = _rope_tables(seq, HEAD_DIM)
    rep = LANES // HEAD_DIM
    a_cos, a_first, a_second = (jnp.tile(t, (1, rep)) for t in (a_cos, a_first, a_second))
    r_cos, r_first, r_second = _rope_tables(seq, RET_HEAD_DIM)
    r_sin = r_first + r_second
    head_sum = jnp.kron(jnp.eye(MXU_COLS // HEAD_DIM, dtype=F32),
                        jnp.full((HEAD_DIM, HEAD_DIM), 1.0 / HEAD_DIM, F32)).astype(BF16)[None]

    rows = lambda v: v.reshape(v.shape[0], 1, -1)
    norm_mix, norm_ffn = rows(norm_mix_g), rows(norm_ffn_g)
    final_g = _Layer(final_norm_g.reshape(1, 1, d), 0)
    mix_w1, mix_w2 = conv_w_in.astype(BF16), conv_w_out.astype(BF16)
    no_bias = _Layer(jnp.zeros((1, 1, d), F32), 0)
    att_wq, att_wo = attn_w_qkv.astype(BF16), attn_w_out.astype(BF16)
    att_qg = rows(jnp.tile(attn_q_norm_g, (1, rep)))
    att_kg = rows(jnp.tile(attn_k_norm_g, (1, rep)))
    ret_wi, ret_wo = ret_w_in.astype(BF16), ret_w_out.astype(BF16)
    ffn_w1, ffn_w2, ffn_cb = ffn_w_up.astype(BF16), ffn_w_down.astype(BF16), rows(ffn_conv_b)

    kinds = [i % N_MIXERS for i in range(DEPTH)]
    reads_ctx = [kd in (1, 2) for kd in kinds]
    cx = ctx
    for i in range(DEPTH):
        kind = kinds[i]
        j = i // N_MIXERS
        ctx_out = any(reads_ctx[i + 1:])
        mod = _Layer(mod_all, i)
        ng = _Layer(norm_mix, i)
        last = i == DEPTH - 1
        if kind == 0:
            mix = functools.partial(_convmlp, mod=mod, norm_g=ng, w1=_Layer(mix_w1, j), ck=_Layer(conv_k, j),
                                    cb=no_bias, w2=_Layer(mix_w2, j), final_g=final_g, kind="mix")
            x = mix(x, ctx=False)
            if ctx_out:
                cx = mix(cx, ctx=True)
        elif kind == 1:
            wo = _Layer(att_wo, j)
            proj = functools.partial(_qkv, mod=mod, norm_g=ng, w=_Layer(att_wq, j), qg=_Layer(att_qg, j),
                                     kg=_Layer(att_kg, j), bd=_Layer(head_sum, 0),
                                     cos=a_cos, s_up=a_first, s_dn=a_second)
            q, k, v = proj(x, ctx=False)
            qc, kc, vc = proj(cx, ctx=True)
            o = _attention(q, [(k, v), (kc, vc)])
            x = _outproj([o], x, mod, wo, ctx=False)
            if ctx_out:
                oc = _attention(qc, [(kc, vc)])
                cx = _outproj([oc.reshape(1, -1, d)], cx.reshape(1, -1, d), mod, wo, ctx=True).reshape(cx.shape)
        else:
            wi = _Layer(ret_wi, j)
            q, k, v, gt = _retproj(x, mod, ng, wi, r_cos, r_sin, ctx=False)
            kc, vc = (t.reshape(batch, cx.shape[1], -1) for t in
                      _retproj(cx.reshape(1, -1, d), mod, ng, wi, r_cos, r_sin, ctx=True))
            y = _retention(ret_decay[j].reshape(-1), q, k, v, kc, vc)
            x = _outproj([y, gt], x, mod, _Layer(ret_wo, j), ctx=False)
            assert not ctx_out
        ffn = functools.partial(_convmlp, mod=mod, norm_g=_Layer(norm_ffn, i), w1=_Layer(ffn_w1, i),
                                ck=_Layer(ffn_conv_k, i), cb=_Layer(ffn_cb, i), w2=_Layer(ffn_w2, i),
                                final_g=final_g, kind="ffn")
        x = ffn(x, ctx=False, final_norm=last)
        if ctx_out:
            cx = ffn(cx, ctx=True)
    return x
```

```python
import functools
from typing import NamedTuple

import jax
import jax.numpy as jnp
from jax import lax
from jax.experimental import pallas as pl
from jax.experimental.pallas import tpu as pltpu

F32 = jnp.float32
BF16 = jnp.bfloat16

D_MODEL = 1024
DEPTH = 4
GRID_W = 64
N_MIXERS = 3
HEAD_DIM = 64
N_Q_HEADS = D_MODEL // HEAD_DIM
N_KV_HEADS = N_Q_HEADS // 4
GQA_GROUP = N_Q_HEADS // N_KV_HEADS
ROPE_THETA = 10000.0
RET_HEAD_DIM = 256
RET_HEADS = D_MODEL // RET_HEAD_DIM
RET_V_DIM = 2 * RET_HEAD_DIM
RET_BLOCK = 256
RET_HEADS_PER_STEP = 2
D_FF = ((8 * D_MODEL // 3 + 127) // 128) * 128
NORM_EPS = 1e-6
LOG2_E = 1.4426950408889634

LANES = 128
SUBLANES = 8
BF16_ROWS = 16
MXU_COLS = 256
VMEM_LIMIT = 56 * 1024 * 1024

FFN_TILE_ROWS = 512
TILE_ROWS = 1024
MOD_ROWS = 16
CTX_ROW = 8


def _params(n_axes):
    return pltpu.CompilerParams(dimension_semantics=("arbitrary",) * n_axes,
                                vmem_limit_bytes=VMEM_LIMIT)


class _Layer(NamedTuple):
    stack: jax.Array
    index: int

    @property
    def shape(self):
        return self.stack.shape[1:]


def _resident(p):
    zeros = (0,) * len(p.shape)
    return pl.BlockSpec((None,) + p.shape, lambda *_: (p.index,) + zeros,
                        pipeline_mode=pl.Buffered(1))


def _silu(v):
    return v * (1.0 / (1.0 + jnp.exp(-v)))


def _modulate(xf, g, shift, scale):
    ms = jnp.mean(xf * xf, axis=-1, keepdims=True)
    y = xf * lax.rsqrt(ms + NORM_EPS) * g
    return y * (1.0 + scale) + shift


def _dot(a, b):
    return jnp.dot(a, b, preferred_element_type=F32)


def _dot_nt(a, b):
    return lax.dot_general(a, b, (((1,), (1,)), ((), ())), preferred_element_type=F32)


def _ada_kernel(c_ref, w_ref, b_ref, o_ref):
    s = _silu(c_ref[...]).astype(BF16)
    o_ref[...] = _dot(s, w_ref[...].astype(BF16)) + b_ref[...]


def _ada(cvec, ada_w, ada_b):
    depth, d, n = ada_w.shape
    tn = 1536
    return pl.pallas_call(
        _ada_kernel,
        grid=(depth, n // tn),
        in_specs=[
            pl.BlockSpec((MOD_ROWS, d), lambda l, j: (0, 0)),
            pl.BlockSpec((None, d, tn), lambda l, j: (l, 0, j)),
            pl.BlockSpec((None, 1, tn), lambda l, j: (l, 0, j)),
        ],
        out_specs=pl.BlockSpec((None, MOD_ROWS, tn), lambda l, j: (l, 0, j)),
        out_shape=jax.ShapeDtypeStruct((depth, MOD_ROWS, n), F32),
        compiler_params=_params(2),
        name="ada",
    )(cvec, ada_w, ada_b.reshape(depth, 1, n))


def _mod_spec(mod, ctx):
    blk = (None, None, 6, 1, D_MODEL)
    if ctx:
        return pl.BlockSpec(blk, lambda b, t: (mod.index, CTX_ROW, 0, 0, 0))
    return pl.BlockSpec(blk, lambda b, t: (mod.index, b, 0, 0, 0))


def _convmlp_kernel(x_ref, xp_ref, xn_ref, mod_ref, g_ref, w1_ref, ck_ref, cb_ref, w2_ref, fg_ref,
                    *rest, kind, bm, mod_base, final_norm):
    if len(rest) == 4:
        o_ref, h_ref, u_ref, z_ref = rest
    else:
        nw1_ref, nw2_ref, o_ref, ow1_ref, ow2_ref, h_ref, u_ref, z_ref = rest
        ow1_ref[...] = nw1_ref[...].astype(BF16)
        ow2_ref[...] = nw2_ref[...].astype(BF16)
    t = pl.program_id(1)
    nt = pl.num_programs(1)
    shift = mod_ref[mod_base]
    scale = mod_ref[mod_base + 1]
    gate = mod_ref[mod_base + 2]
    g = g_ref[...]
    x = x_ref[...]
    h_ref[0:bm, :] = _modulate(x, g, shift, scale).astype(BF16)
    keep_n = (t < nt - 1).astype(F32)
    keep_p = (t > 0).astype(F32)
    halo = jnp.concatenate([_modulate(xn_ref[...], g, shift, scale) * keep_n,
                            _modulate(xp_ref[...], g, shift, scale) * keep_p], axis=0)
    h_ref[bm:bm + BF16_ROWS, :] = halo.astype(BF16)
    h = h_ref[...]
    n_mid = D_FF if kind == "ffn" else D_MODEL
    n_chunks = n_mid // MXU_COLS
    top = SUBLANES

    def rows_of(s, pos, n):
        return pl.ds(s + 2 * pos, n, stride=2)

    def put(slot, s, u):
        for j in range(MXU_COLS // LANES):
            uj = u[:, j * LANES:(j + 1) * LANES]
            u_ref[slot, j, rows_of(s, 0, top), :] = uj[bm + top:bm + 2 * top]
            u_ref[slot, j, rows_of(s, top, bm), :] = uj[0:bm]
            u_ref[slot, j, rows_of(s, top + bm, top), :] = uj[bm:bm + top]

    def conv3(slot, j, s, c0):
        k0 = ck_ref[0:1, c0:c0 + LANES]
        k1 = ck_ref[1:2, c0:c0 + LANES]
        k2 = ck_ref[2:3, c0:c0 + LANES]
        return (u_ref[slot, j, rows_of(s, top - 1, bm), :] * k0
                + u_ref[slot, j, rows_of(s, top, bm), :] * k1
                + u_ref[slot, j, rows_of(s, top + 1, bm), :] * k2)

    def up(ci):
        slot = ci
        cols = lambda s: slice(s * n_mid + ci * MXU_COLS, s * n_mid + (ci + 1) * MXU_COLS)
        if kind == "ffn":
            put(slot, 0, _dot(h, w1_ref[:, cols(0)]))
            put(slot, 1, _dot(h, w1_ref[:, cols(1)]))
        else:
            put(slot, 0, _dot(h, w1_ref[:, cols(0)]))
            put(slot, 1, _dot(h, w1_ref[:, cols(1)]) * _dot(h, w1_ref[:, cols(2)]))

    def mid(ci):
        slot = ci
        for j in range(MXU_COLS // LANES):
            c0 = ci * MXU_COLS + j * LANES
            if kind == "ffn":
                cv = conv3(slot, j, 0, c0) + cb_ref[:, c0:c0 + LANES]
                cg = conv3(slot, j, 1, D_FF + c0) + cb_ref[:, D_FF + c0:D_FF + c0 + LANES]
                z = _silu(cg) * cv
            else:
                z = u_ref[slot, j, rows_of(0, top, bm), :] * conv3(slot, j, 1, c0)
            z_ref[:, c0:c0 + LANES] = z.astype(BF16)

    up(0)
    for ci in range(n_chunks):
        if ci + 1 < n_chunks:
            up(ci + 1)
        mid(ci)
    out = x + gate * _dot(z_ref[...], w2_ref[...])
    if final_norm:
        ms = jnp.mean(out * out, axis=-1, keepdims=True)
        out = out * lax.rsqrt(ms + NORM_EPS) * fg_ref[...]
    o_ref[...] = out


def _convmlp(x, mod, norm_g, w1, ck, cb, w2, final_g, *, kind, ctx, final_norm=False, cast_next=None):
    b, l, d = x.shape
    bm = min(l, FFN_TILE_ROWS if kind == "ffn" else TILE_ROWS)
    n_chunks = w2.shape[0] // MXU_COLS
    nt = l // bm
    hb = bm // SUBLANES
    last_hb = l // SUBLANES - 1
    kern = functools.partial(_convmlp_kernel, kind=kind, bm=bm,
                             mod_base=3 if kind == "ffn" else 0, final_norm=final_norm)
    extra_in, extra_out, extra_shapes, extra_args = [], [], [], []
    if cast_next is not None:
        steps = b * nt
        for w, per in zip(cast_next, (1, 2)):
            rows_w, cols_w = w.shape
            blk = rows_w * per // steps
            assert blk % BF16_ROWS == 0 and blk * steps == rows_w * per
            extra_in.append(pl.BlockSpec((None, blk, cols_w),
                                         lambda i, t, w=w, per=per: (w.index, (i * nt + t) // per, 0)))
            extra_out.append(pl.BlockSpec((blk, cols_w), lambda i, t, per=per: ((i * nt + t) // per, 0)))
            extra_shapes.append(jax.ShapeDtypeStruct((rows_w, cols_w), BF16))
            extra_args.append(w.stack)
    x_spec = pl.BlockSpec((None, bm, d), lambda i, t: (i, t, 0))
    outs = pl.pallas_call(
        kern,
        grid=(b, nt),
        in_specs=[
            x_spec,
            pl.BlockSpec((None, SUBLANES, d), lambda i, t: (i, jnp.maximum(t * hb - 1, 0), 0)),
            pl.BlockSpec((None, SUBLANES, d), lambda i, t: (i, jnp.minimum((t + 1) * hb, last_hb), 0)),
            _mod_spec(mod, ctx),
            _resident(norm_g),
            _resident(w1),
            _resident(ck),
            _resident(cb),
            _resident(w2),
            _resident(final_g),
        ] + extra_in,
        out_specs=[x_spec] + extra_out,
        out_shape=[jax.ShapeDtypeStruct((b, l, d), F32)] + extra_shapes,
        scratch_shapes=[pltpu.VMEM((bm + BF16_ROWS, d), BF16),
                        pltpu.VMEM((n_chunks, MXU_COLS // LANES, 2 * (bm + 2 * SUBLANES), LANES), F32),
                        pltpu.VMEM((bm, w2.shape[0]), BF16)],
        compiler_params=_params(2),
        name="convmlp_" + kind,
    )(x, x, x, mod.stack, norm_g.stack, w1.stack, ck.stack, cb.stack, w2.stack, final_g.stack, *extra_args)
    return outs[0] if cast_next is None else outs


def _outproj_kernel(*refs, gated):
    if not gated:
        a_ref, x_ref, mod_ref, w_ref, o_ref = refs
        acc = _dot(a_ref[...], w_ref[...])
    else:
        y_ref, gt_ref, x_ref, mod_ref, w_ref, o_ref = refs

        def gate(hd):
            cols = slice(hd * RET_V_DIM, (hd + 1) * RET_V_DIM)
            y = y_ref[:, cols].astype(F32)
            ms = jnp.mean(y * y, axis=-1, keepdims=True)
            return (_silu(gt_ref[:, cols].astype(F32)) * (y * lax.rsqrt(ms + NORM_EPS))).astype(BF16)

        a_next = gate(0)
        acc = None
        for hd in range(RET_HEADS):
            a = a_next
            if hd + 1 < RET_HEADS:
                a_next = gate(hd + 1)
            part = _dot(a, w_ref[hd * RET_V_DIM:(hd + 1) * RET_V_DIM, :])
            acc = part if acc is None else acc + part
    o_ref[...] = x_ref[...] + mod_ref[2] * acc


def _outproj(acts, x, mod, w, *, ctx):
    b, l, d = x.shape
    bm = min(l, TILE_ROWS if len(acts) == 1 else TILE_ROWS // 2)
    act_specs = [pl.BlockSpec((None, bm, a.shape[-1]), lambda i, t: (i, t, 0)) for a in acts]
    return pl.pallas_call(
        functools.partial(_outproj_kernel, gated=len(acts) == 2),
        grid=(b, l // bm),
        in_specs=act_specs + [
            pl.BlockSpec((None, bm, d), lambda i, t: (i, t, 0)),
            _mod_spec(mod, ctx),
            _resident(w),
        ],
        out_specs=pl.BlockSpec((None, bm, d), lambda i, t: (i, t, 0)),
        out_shape=jax.ShapeDtypeStruct((b, l, d), F32),
        input_output_aliases={len(acts): 0},
        compiler_params=_params(2),
        name="outproj",
    )(*acts, x, mod.stack, w.stack)


def _head_meansq(v, bd_ref):
    return _dot((v * v).astype(BF16), bd_ref[...])


def _rope64(v, cos, s_up, s_dn):
    return v * cos + pltpu.roll(v, LANES - 16, axis=1) * s_up + pltpu.roll(v, 16, axis=1) * s_dn


def _qkv_kernel(x_ref, mod_ref, g_ref, w_ref, qg_ref, kg_ref, bd_ref, cos_ref, sup_ref, sdn_ref,
                q_ref, k_ref, v_ref, *, rope):
    h = _modulate(x_ref[...], g_ref[...], mod_ref[0], mod_ref[1]).astype(BF16)
    nq = N_Q_HEADS * HEAD_DIM
    nkv = N_KV_HEADS * HEAD_DIM
    n_norm = (nq + nkv) // MXU_COLS
    q_gain = qg_ref[...] * (HEAD_DIM ** -0.5 * LOG2_E)
    k_gain = kg_ref[...]
    heads_per_slab = LANES // HEAD_DIM

    def project(ci):
        return _dot(h, w_ref[:, ci * MXU_COLS:(ci + 1) * MXU_COLS])

    def finish(ci, u, ms):
        u = u * lax.rsqrt(ms + NORM_EPS)
        for j in range(MXU_COLS // LANES):
            s = u[:, j * LANES:(j + 1) * LANES] * (q_gain if ci < n_norm - 1 else k_gain)
            if rope:
                s = _rope64(s, cos_ref[...], sup_ref[...], sdn_ref[...])
            if ci < n_norm - 1:
                lo = ci * MXU_COLS + j * LANES
                q_ref[:, lo:lo + LANES] = s.astype(BF16)
            else:
                for e in range(heads_per_slab):
                    k_ref[j * heads_per_slab + e] = s[:, e * HEAD_DIM:(e + 1) * HEAD_DIM].astype(BF16)

    us = {0: project(0), 1: project(1)}
    mss = {0: _head_meansq(us[0], bd_ref)}
    for ci in range(n_norm):
        if ci + 2 <= n_norm:
            us[ci + 2] = project(ci + 2)
        if ci + 1 < n_norm:
            mss[ci + 1] = _head_meansq(us[ci + 1], bd_ref)
        finish(ci, us.pop(ci), mss.pop(ci))
    vv = us.pop(n_norm)
    ones = jnp.ones((vv.shape[0], HEAD_DIM), F32)
    for e in range(N_KV_HEADS):
        v_ref[e] = jnp.concatenate([vv[:, e * HEAD_DIM:(e + 1) * HEAD_DIM], ones], axis=-1).astype(BF16)


def _qkv(x, mod, norm_g, w, qg, kg, bd, cos, s_up, s_dn, *, ctx):
    b, l, d = x.shape
    bm = min(l, TILE_ROWS // 2)
    tab = pl.BlockSpec((bm, LANES), lambda i, t: (t, 0))
    kern = functools.partial(_qkv_kernel, rope=not ctx)
    return pl.pallas_call(
        kern,
        grid=(b, l // bm),
        in_specs=[
            pl.BlockSpec((None, bm, d), lambda i, t: (i, t, 0)),
            _mod_spec(mod, ctx),
            _resident(norm_g),
            _resident(w),
            _resident(qg),
            _resident(kg),
            _resident(bd),
            tab, tab, tab,
        ],
        out_specs=[
            pl.BlockSpec((None, bm, d), lambda i, t: (i, t, 0)),
            pl.BlockSpec((None, N_KV_HEADS, bm, HEAD_DIM), lambda i, t: (i, 0, t, 0)),
            pl.BlockSpec((None, N_KV_HEADS, bm, 2 * HEAD_DIM), lambda i, t: (i, 0, t, 0)),
        ],
        out_shape=[
            jax.ShapeDtypeStruct((b, l, d), BF16),
            jax.ShapeDtypeStruct((b, N_KV_HEADS, l, HEAD_DIM), BF16),
            jax.ShapeDtypeStruct((b, N_KV_HEADS, l, 2 * HEAD_DIM), BF16),
        ],
        compiler_params=_params(2),
        name="qkv",
    )(x, mod.stack, norm_g.stack, w.stack, qg.stack, kg.stack, bd.stack, cos, s_up, s_dn)


def _attn_kernel(*refs, n_sets):
    q_ref = refs[0]
    kv = refs[1:1 + 2 * n_sets]
    o_ref = refs[1 + 2 * n_sets]
    q = q_ref[...]
    ks = [kv[2 * i][...] for i in range(n_sets)]
    vs = [kv[2 * i + 1][...] for i in range(n_sets)]
    outs = []

    def scores(gi):
        qg = q[:, gi * HEAD_DIM:(gi + 1) * HEAD_DIM]
        return [_dot_nt(qg, k) for k in ks]

    ss_next = scores(0)
    for gi in range(GQA_GROUP):
        ss = ss_next
        if gi + 1 < GQA_GROUP:
            ss_next = scores(gi + 1)
        m = ss[0].max(axis=-1, keepdims=True)
        for s in ss[1:]:
            m = jnp.maximum(m, s.max(axis=-1, keepdims=True))
        acc = None
        for s, v in zip(ss, vs):
            pv = _dot(jnp.exp2(s - m).astype(BF16), v)
            acc = pv if acc is None else acc + pv
        o = acc * (1.0 / pltpu.roll(acc, HEAD_DIM, axis=1))
        outs.append(o[:, 0:HEAD_DIM])
    o_ref[...] = jnp.concatenate(outs, axis=-1).astype(BF16)


def _attention(q, kvs):
    b, l, d = q.shape
    bq = min(l, TILE_ROWS)
    width = GQA_GROUP * HEAD_DIM
    in_specs = [pl.BlockSpec((None, bq, width), lambda i, h, t: (i, t, h))]
    args = [q]
    for k, v in kvs:
        lk = k.shape[2]
        in_specs += [pl.BlockSpec((None, None, lk, HEAD_DIM), lambda i, h, t: (i, h, 0, 0)),
                     pl.BlockSpec((None, None, lk, 2 * HEAD_DIM), lambda i, h, t: (i, h, 0, 0))]
        args += [k, v]
    return pl.pallas_call(
        functools.partial(_attn_kernel, n_sets=len(kvs)),
        grid=(b, N_KV_HEADS, l // bq),
        in_specs=in_specs,
        out_specs=pl.BlockSpec((None, bq, width), lambda i, h, t: (i, t, h)),
        out_shape=jax.ShapeDtypeStruct((b, l, d), BF16),
        compiler_params=_params(3),
        name="attention",
    )(*args)


def _rope256(v, cos, sgn_sin):
    return v * cos + pltpu.roll(v, LANES // 2, axis=1) * sgn_sin


def _retproj_kernel(x_ref, mod_ref, g_ref, w_ref, cos_ref, sin_ref, *out_refs, ctx):
    h = _modulate(x_ref[...], g_ref[...], mod_ref[0], mod_ref[1]).astype(BF16)
    nqk = RET_HEADS * RET_HEAD_DIM
    nv = RET_HEADS * RET_V_DIM
    kscale = RET_HEAD_DIM ** -0.5
    col = nqk
    if ctx:
        k_ref, v_ref = out_refs
    else:
        q_ref, k_ref, v_ref, gt_ref = out_refs
        for ci in range(nqk // MXU_COLS):
            u = _dot(h, w_ref[:, ci * MXU_COLS:(ci + 1) * MXU_COLS])
            for j in range(MXU_COLS // LANES):
                sl = slice(j * LANES, (j + 1) * LANES)
                lo = ci * MXU_COLS + j * LANES
                q_ref[:, lo:lo + LANES] = _rope256(u[:, sl], cos_ref[:, sl], sin_ref[:, sl]).astype(BF16)
    for ci in range(nqk // MXU_COLS):
        u = _dot(h, w_ref[:, col + ci * MXU_COLS:col + (ci + 1) * MXU_COLS]) * kscale
        for j in range(MXU_COLS // LANES):
            sl = slice(j * LANES, (j + 1) * LANES)
            lo = ci * MXU_COLS + j * LANES
            s = u[:, sl]
            if not ctx:
                s = _rope256(s, cos_ref[:, sl], sin_ref[:, sl])
            k_ref[:, lo:lo + LANES] = s.astype(BF16)
    col += nqk
    for ci in range(nv // MXU_COLS):
        sl = slice(ci * MXU_COLS, (ci + 1) * MXU_COLS)
        v_ref[:, sl] = _dot(h, w_ref[:, col + ci * MXU_COLS:col + (ci + 1) * MXU_COLS]).astype(BF16)
    if not ctx:
        col += nv
        for ci in range(nv // MXU_COLS):
            sl = slice(ci * MXU_COLS, (ci + 1) * MXU_COLS)
            gt_ref[:, sl] = _dot(h, w_ref[:, col + ci * MXU_COLS:col + (ci + 1) * MXU_COLS]).astype(BF16)


def _retproj(x, mod, norm_g, w, cos, sin, *, ctx):
    b, l, d = x.shape
    bm = min(l, TILE_ROWS)
    nqk = RET_HEADS * RET_HEAD_DIM
    nv = RET_HEADS * RET_V_DIM
    row = lambda n: pl.BlockSpec((None, bm, n), lambda i, t: (i, t, 0))
    tab = pl.BlockSpec((bm, RET_HEAD_DIM), lambda i, t: (t, 0))
    widths = [nqk, nv] if ctx else [nqk, nqk, nv, nv]
    return pl.pallas_call(
        functools.partial(_retproj_kernel, ctx=ctx),
        grid=(b, l // bm),
        in_specs=[row(d), _mod_spec(mod, ctx), _resident(norm_g), _resident(w), tab, tab],
        out_specs=[row(n) for n in widths],
        out_shape=[jax.ShapeDtypeStruct((b, l, n), BF16) for n in widths],
        compiler_params=_params(2),
        name="retproj",
    )(x, mod.stack, norm_g.stack, w.stack, cos, sin)


def _retention_kernel(dec_ref, q_ref, k_ref, v_ref, kc_ref, vc_ref, o_ref, y_ref, *, seq, ctx_len):
    c = RET_BLOCK
    n_chunks = seq // c
    ii = lax.broadcasted_iota(jnp.int32, (c, c), 0)
    jj = lax.broadcasted_iota(jnp.int32, (c, c), 1)
    diff = (ii - jj).astype(F32)
    pos = lax.broadcasted_iota(jnp.int32, (c, 1), 0).astype(F32)
    posc = lax.broadcasted_iota(jnp.int32, (ctx_len, 1), 0).astype(F32)

    def outer(kd, v):
        return _dot(kd.T.astype(BF16), v)

    def rows(i):
        return slice(i * c, (i + 1) * c)

    class Head:
        def __init__(self, hh):
            hd = pl.program_id(1) * RET_HEADS_PER_STEP + hh

            def log_gamma(idx):
                dv = jnp.full((1, 1), dec_ref[idx], F32)
                return jnp.log1p(-jnp.exp2(-dv))

            lf = log_gamma(hd)
            lb = log_gamma(RET_HEADS + hd)
            self.hh = hh
            self.qk = slice(hh * RET_HEAD_DIM, (hh + 1) * RET_HEAD_DIM)
            self.vv = slice(hh * RET_V_DIM, (hh + 1) * RET_V_DIM)
            self.decay = jnp.where(diff >= 0, jnp.exp(diff * lf), jnp.exp(-diff * lb))
            self.qd_f = jnp.exp((pos + 1.0) * lf)
            self.kd_f = jnp.exp((c - 1.0 - pos) * lf)
            self.qd_b = jnp.exp((c - pos) * lb)
            self.kd_b = jnp.exp(pos * lb)
            self.cd_f = jnp.exp(c * lf)
            self.cd_b = jnp.exp(c * lb)
            kcf = kc_ref[:, self.qk].astype(F32)
            vc = vc_ref[:, self.vv]
            self.sf = outer(kcf * jnp.exp((ctx_len - 1.0 - posc) * lf), vc)
            self.sb = outer(kcf * jnp.exp(posc * lb), vc)

        def q(self, i):
            return q_ref[rows(i), self.qk]

        def k(self, i):
            return k_ref[rows(i), self.qk]

        def v(self, i):
            return v_ref[rows(i), self.vv]

        def intra(self, i):
            a = (_dot_nt(self.q(i), self.k(i)) * self.decay).astype(BF16)
            return _dot(a, self.v(i))

    heads = [Head(hh) for hh in range(RET_HEADS_PER_STEP)]
    for step in range(n_chunks):
        i = step
        j = n_chunks - 1 - step
        for h in heads:
            inter_f = _dot(h.q(i), h.sf.astype(BF16)) * h.qd_f
            inter_b = _dot(h.q(j), h.sb.astype(BF16)) * h.qd_b
            if i < j:
                y_ref[h.hh, rows(i), :] = h.intra(i) + inter_f
                y_ref[h.hh, rows(j), :] = h.intra(j) + inter_b
            else:
                o_ref[rows(i), h.vv] = (y_ref[h.hh, rows(i), :] + inter_f).astype(BF16)
                o_ref[rows(j), h.vv] = (y_ref[h.hh, rows(j), :] + inter_b).astype(BF16)
            if step < n_chunks - 1:
                h.sf = h.cd_f * h.sf + outer(h.k(i).astype(F32) * h.kd_f, h.v(i))
                h.sb = h.cd_b * h.sb + outer(h.k(j).astype(F32) * h.kd_b, h.v(j))


def _retention(dec, q, k, v, kc, vc):
    b, l, _ = q.shape
    lc = kc.shape[1]
    hs = RET_HEADS_PER_STEP
    qk = lambda n: pl.BlockSpec((None, n, hs * RET_HEAD_DIM), lambda i, h: (i, 0, h))
    vv = lambda n: pl.BlockSpec((None, n, hs * RET_V_DIM), lambda i, h: (i, 0, h))
    return pl.pallas_call(
        functools.partial(_retention_kernel, seq=l, ctx_len=lc),
        grid=(b, RET_HEADS // hs),
        in_specs=[pl.BlockSpec(memory_space=pltpu.SMEM),
                  qk(l), qk(l), vv(l), qk(lc), vv(lc)],
        out_specs=vv(l),
        out_shape=jax.ShapeDtypeStruct((b, l, RET_HEADS * RET_V_DIM), BF16),
        scratch_shapes=[pltpu.VMEM((hs, l, RET_V_DIM), F32)],
        compiler_params=_params(2),
        name="retention",
    )(dec, q, k, v, kc, vc)


def _rope_tables(seq, head_dim):
    quarter = head_dim // 4
    rows = jnp.repeat(jnp.arange(seq // GRID_W, dtype=F32), GRID_W)
    cols = jnp.tile(jnp.arange(GRID_W, dtype=F32), seq // GRID_W)
    inv = ROPE_THETA ** (-jnp.arange(quarter, dtype=F32) / quarter)
    ang = jnp.stack([rows[:, None] * inv, cols[:, None] * inv], axis=1)
    cos = jnp.cos(ang)
    sin = jnp.sin(ang)
    cos_t = jnp.stack([cos, cos], axis=2).reshape(seq, head_dim)
    sin_first = jnp.stack([-sin, jnp.zeros_like(sin)], axis=2).reshape(seq, head_dim)
    sin_second = jnp.stack([jnp.zeros_like(sin), sin], axis=2).reshape(seq, head_dim)
    return cos_t, sin_first, sin_second


def kernel(x, c, ctx, c_ctx, ada_w, ada_b, norm_mix_g, norm_ffn_g, final_norm_g, conv_w_in, conv_k, conv_w_out, attn_w_qkv, attn_q_norm_g, attn_k_norm_g, attn_w_out, ret_w_in, ret_decay, ret_w_out, ffn_w_up, ffn_conv_k, ffn_conv_b, ffn_w_down):
    batch, seq, d = x.shape
    assert d == D_MODEL and seq % (2 * RET_BLOCK) == 0 and ctx.shape[1] % SUBLANES == 0

    cvec = jnp.zeros((MOD_ROWS, d), F32).at[:batch].set(c).at[CTX_ROW].set(c_ctx)
    mod_all = _ada(cvec, ada_w, ada_b).reshape(DEPTH, MOD_ROWS, 6, 1, d)

    a_cos, a_first, a_second = _rope_tables(seq, HEAD_DIM)
    rep = LANES // HEAD_DIM
    a_cos, a_first, a_second = (jnp.tile(t, (1, rep)) for t in (a_cos, a_first, a_second))
    r_cos, r_first, r_second = _rope_tables(seq, RET_HEAD_DIM)
    r_sin = r_first + r_second
    head_sum = jnp.kron(jnp.eye(MXU_COLS // HEAD_DIM, dtype=F32),
                        jnp.full((HEAD_DIM, HEAD_DIM), 1.0 / HEAD_DIM, F32)).astype(BF16)[None]

    rows = lambda v: v.reshape(v.shape[0], 1, -1)
    norm_mix, norm_ffn = rows(norm_mix_g), rows(norm_ffn_g)
    final_g = _Layer(final_norm_g.reshape(1, 1, d), 0)
    mix_w1, mix_w2 = conv_w_in.astype(BF16), conv_w_out.astype(BF16)
    no_bias = _Layer(jnp.zeros((1, 1, d), F32), 0)
    att_wq, att_wo = attn_w_qkv.astype(BF16), attn_w_out.astype(BF16)
    att_qg = rows(jnp.tile(attn_q_norm_g, (1, rep)))
    att_kg = rows(jnp.tile(attn_k_norm_g, (1, rep)))
    ret_wi, ret_wo = ret_w_in.astype(BF16), ret_w_out.astype(BF16)
    ffn_cb = rows(ffn_conv_b)
    ffn_w1, ffn_w2 = _Layer(ffn_w_up[:1].astype(BF16), 0), _Layer(ffn_w_down[:1].astype(BF16), 0)

    kinds = [i % N_MIXERS for i in range(DEPTH)]
    reads_ctx = [kd in (1, 2) for kd in kinds]
    cx = ctx
    for i in range(DEPTH):
        kind = kinds[i]
        j = i // N_MIXERS
        ctx_out = any(reads_ctx[i + 1:])
        mod = _Layer(mod_all, i)
        ng = _Layer(norm_mix, i)
        last = i == DEPTH - 1
        if kind == 0:
            mix = functools.partial(_convmlp, mod=mod, norm_g=ng, w1=_Layer(mix_w1, j), ck=_Layer(conv_k, j),
                                    cb=no_bias, w2=_Layer(mix_w2, j), final_g=final_g, kind="mix")
            x = mix(x, ctx=False)
            if ctx_out:
                cx = mix(cx, ctx=True)
        elif kind == 1:
            wo = _Layer(att_wo, j)
            proj = functools.partial(_qkv, mod=mod, norm_g=ng, w=_Layer(att_wq, j), qg=_Layer(att_qg, j),
                                     kg=_Layer(att_kg, j), bd=_Layer(head_sum, 0),
                                     cos=a_cos, s_up=a_first, s_dn=a_second)
            q, k, v = proj(x, ctx=False)
            qc, kc, vc = proj(cx, ctx=True)
            o = _attention(q, [(k, v), (kc, vc)])
            x = _outproj([o], x, mod, wo, ctx=False)
            if ctx_out:
                oc = _attention(qc, [(kc, vc)])
                cx = _outproj([oc], cx, mod, wo, ctx=True)
        else:
            wi = _Layer(ret_wi, j)
            q, k, v, gt = _retproj(x, mod, ng, wi, r_cos, r_sin, ctx=False)
            kc, vc = _retproj(cx, mod, ng, wi, r_cos, r_sin, ctx=True)
            y = _retention(ret_decay[j].reshape(-1), q, k, v, kc, vc)
            x = _outproj([y, gt], x, mod, _Layer(ret_wo, j), ctx=False)
            assert not ctx_out
        ffn = functools.partial(_convmlp, mod=mod, norm_g=_Layer(norm_ffn, i), w1=ffn_w1,
                                ck=_Layer(ffn_conv_k, i), cb=_Layer(ffn_cb, i), w2=ffn_w2,
                                final_g=final_g, kind="ffn")
        if last:
            x = ffn(x, ctx=False, final_norm=True)
        else:
            x, nw1, nw2 = ffn(x, ctx=False, cast_next=(_Layer(ffn_w_up, i + 1), _Layer(ffn_w_down, i + 1)))
        if ctx_out:
            cx = ffn(cx, ctx=True)
        if not last:
            ffn_w1, ffn_w2 = _Layer(nw1[None], 0), _Layer(nw2[None], 0)
    return x
```

```python
import functools
from typing import NamedTuple

import jax
import jax.numpy as jnp
from jax import lax
from jax.experimental import pallas as pl
from jax.experimental.pallas import tpu as pltpu

F32 = jnp.float32
BF16 = jnp.bfloat16

D_MODEL = 1024
DEPTH = 4
GRID_W = 64
N_MIXERS = 3
HEAD_DIM = 64
N_Q_HEADS = D_MODEL // HEAD_DIM
N_KV_HEADS = N_Q_HEADS // 4
GQA_GROUP = N_Q_HEADS // N_KV_HEADS
ROPE_THETA = 10000.0
RET_HEAD_DIM = 256
RET_HEADS = D_MODEL // RET_HEAD_DIM
RET_V_DIM = 2 * RET_HEAD_DIM
RET_BLOCK = 256
RET_HEADS_PER_STEP = 2
D_FF = ((8 * D_MODEL // 3 + 127) // 128) * 128
NORM_EPS = 1e-6
LOG2_E = 1.4426950408889634

LANES = 128
SUBLANES = 8
BF16_ROWS = 16
MXU_COLS = 256
VMEM_LIMIT = 56 * 1024 * 1024

FFN_TILE_ROWS = 512
TILE_ROWS = 1024
MOD_ROWS = 16
CTX_ROW = 8


def _params(n_axes):
    return pltpu.CompilerParams(dimension_semantics=("arbitrary",) * n_axes,
                                vmem_limit_bytes=VMEM_LIMIT)


class _Layer(NamedTuple):
    stack: jax.Array
    index: int

    @property
    def shape(self):
        return self.stack.shape[1:]


def _resident(p):
    zeros = (0,) * len(p.shape)
    return pl.BlockSpec((None,) + p.shape, lambda *_: (p.index,) + zeros,
                        pipeline_mode=pl.Buffered(1))


def _silu(v):
    return v * (1.0 / (1.0 + jnp.exp(-v)))


def _modulate(xf, g, shift, scale):
    ms = jnp.mean(xf * xf, axis=-1, keepdims=True)
    y = xf * lax.rsqrt(ms + NORM_EPS) * g
    return y * (1.0 + scale) + shift


def _dot(a, b):
    return jnp.dot(a, b, preferred_element_type=F32)


def _dot_nt(a, b):
    return lax.dot_general(a, b, (((1,), (1,)), ((), ())), preferred_element_type=F32)


def _ada_kernel(c_ref, w_ref, b_ref, o_ref):
    s = _silu(c_ref[...]).astype(BF16)
    o_ref[...] = _dot(s, w_ref[...].astype(BF16)) + b_ref[...]


def _ada(cvec, ada_w, ada_b):
    depth, d, n = ada_w.shape
    tn = 1536
    return pl.pallas_call(
        _ada_kernel,
        grid=(depth, n // tn),
        in_specs=[
            pl.BlockSpec((MOD_ROWS, d), lambda l, j: (0, 0)),
            pl.BlockSpec((None, d, tn), lambda l, j: (l, 0, j)),
            pl.BlockSpec((None, 1, tn), lambda l, j: (l, 0, j)),
        ],
        out_specs=pl.BlockSpec((None, MOD_ROWS, tn), lambda l, j: (l, 0, j)),
        out_shape=jax.ShapeDtypeStruct((depth, MOD_ROWS, n), F32),
        compiler_params=_params(2),
        name="ada",
    )(cvec, ada_w, ada_b.reshape(depth, 1, n))


def _mod_spec(mod, ctx):
    blk = (None, None, 6, 1, D_MODEL)
    if ctx:
        return pl.BlockSpec(blk, lambda b, t: (mod.index, CTX_ROW, 0, 0, 0))
    return pl.BlockSpec(blk, lambda b, t: (mod.index, b, 0, 0, 0))


def _convmlp_kernel(x_ref, xp_ref, xn_ref, mod_ref, g_ref, w1_ref, ck_ref, cb_ref, w2_ref, fg_ref,
                    *rest, kind, bm, mod_base, final_norm):
    n_cast = (len(rest) - 4) // 2
    o_ref = rest[n_cast]
    h_ref, u_ref, z_ref = rest[2 * n_cast + 1:]
    for src, dst in zip(rest[:n_cast], rest[n_cast + 1:2 * n_cast + 1]):
        dst[...] = src[...].astype(BF16)
    t = pl.program_id(1)
    nt = pl.num_programs(1)
    shift = mod_ref[mod_base]
    scale = mod_ref[mod_base + 1]
    gate = mod_ref[mod_base + 2]
    g = g_ref[...]
    x = x_ref[...]
    h_ref[0:bm, :] = _modulate(x, g, shift, scale).astype(BF16)
    keep_n = (t < nt - 1).astype(F32)
    keep_p = (t > 0).astype(F32)
    halo = jnp.concatenate([_modulate(xn_ref[...], g, shift, scale) * keep_n,
                            _modulate(xp_ref[...], g, shift, scale) * keep_p], axis=0)
    h_ref[bm:bm + BF16_ROWS, :] = halo.astype(BF16)
    h = h_ref[...]
    n_mid = D_FF if kind == "ffn" else D_MODEL
    n_chunks = n_mid // MXU_COLS
    top = SUBLANES

    def rows_of(s, pos, n):
        return pl.ds(s + 2 * pos, n, stride=2)

    def put(slot, s, u):
        for j in range(MXU_COLS // LANES):
            uj = u[:, j * LANES:(j + 1) * LANES]
            u_ref[slot, j, rows_of(s, 0, top), :] = uj[bm + top:bm + 2 * top]
            u_ref[slot, j, rows_of(s, top, bm), :] = uj[0:bm]
            u_ref[slot, j, rows_of(s, top + bm, top), :] = uj[bm:bm + top]

    def conv3(slot, j, s, c0):
        k0 = ck_ref[0:1, c0:c0 + LANES]
        k1 = ck_ref[1:2, c0:c0 + LANES]
        k2 = ck_ref[2:3, c0:c0 + LANES]
        return (u_ref[slot, j, rows_of(s, top - 1, bm), :] * k0
                + u_ref[slot, j, rows_of(s, top, bm), :] * k1
                + u_ref[slot, j, rows_of(s, top + 1, bm), :] * k2)

    def up(ci):
        slot = ci
        cols = lambda s: slice(s * n_mid + ci * MXU_COLS, s * n_mid + (ci + 1) * MXU_COLS)
        if kind == "ffn":
            put(slot, 0, _dot(h, w1_ref[:, cols(0)]))
            put(slot, 1, _dot(h, w1_ref[:, cols(1)]))
        else:
            put(slot, 0, _dot(h, w1_ref[:, cols(0)]))
            put(slot, 1, _dot(h, w1_ref[:, cols(1)]) * _dot(h, w1_ref[:, cols(2)]))

    def mid(ci):
        slot = ci
        for j in range(MXU_COLS // LANES):
            c0 = ci * MXU_COLS + j * LANES
            if kind == "ffn":
                cv = conv3(slot, j, 0, c0) + cb_ref[:, c0:c0 + LANES]
                cg = conv3(slot, j, 1, D_FF + c0) + cb_ref[:, D_FF + c0:D_FF + c0 + LANES]
                z = _silu(cg) * cv
            else:
                z = u_ref[slot, j, rows_of(0, top, bm), :] * conv3(slot, j, 1, c0)
            z_ref[:, c0:c0 + LANES] = z.astype(BF16)

    up(0)
    for ci in range(n_chunks):
        if ci + 1 < n_chunks:
            up(ci + 1)
        mid(ci)
    out = x + gate * _dot(z_ref[...], w2_ref[...])
    if final_norm:
        ms = jnp.mean(out * out, axis=-1, keepdims=True)
        out = out * lax.rsqrt(ms + NORM_EPS) * fg_ref[...]
    o_ref[...] = out


def _convmlp(x, mod, norm_g, w1, ck, cb, w2, final_g, *, kind, ctx, final_norm=False, cast=()):
    b, l, d = x.shape
    bm = min(l, FFN_TILE_ROWS if kind == "ffn" else TILE_ROWS)
    n_chunks = w2.shape[0] // MXU_COLS
    nt = l // bm
    hb = bm // SUBLANES
    last_hb = l // SUBLANES - 1
    kern = functools.partial(_convmlp_kernel, kind=kind, bm=bm,
                             mod_base=3 if kind == "ffn" else 0, final_norm=final_norm)
    extra_in, extra_out, extra_shapes, extra_args = [], [], [], []
    steps = b * nt
    for w in cast:
        rows_w, cols_w = w.shape
        per = next(p for p in (1, 2, 4, 8) if (rows_w * p) % (steps * BF16_ROWS) == 0)
        blk = rows_w * per // steps
        extra_in.append(pl.BlockSpec((None, blk, cols_w),
                                     lambda i, t, w=w, per=per: (w.index, (i * nt + t) // per, 0)))
        extra_out.append(pl.BlockSpec((blk, cols_w), lambda i, t, per=per: ((i * nt + t) // per, 0)))
        extra_shapes.append(jax.ShapeDtypeStruct((rows_w, cols_w), BF16))
        extra_args.append(w.stack)
    x_spec = pl.BlockSpec((None, bm, d), lambda i, t: (i, t, 0))
    outs = pl.pallas_call(
        kern,
        grid=(b, nt),
        in_specs=[
            x_spec,
            pl.BlockSpec((None, SUBLANES, d), lambda i, t: (i, jnp.maximum(t * hb - 1, 0), 0)),
            pl.BlockSpec((None, SUBLANES, d), lambda i, t: (i, jnp.minimum((t + 1) * hb, last_hb), 0)),
            _mod_spec(mod, ctx),
            _resident(norm_g),
            _resident(w1),
            _resident(ck),
            _resident(cb),
            _resident(w2),
            _resident(final_g),
        ] + extra_in,
        out_specs=[x_spec] + extra_out,
        out_shape=[jax.ShapeDtypeStruct((b, l, d), F32)] + extra_shapes,
        scratch_shapes=[pltpu.VMEM((bm + BF16_ROWS, d), BF16),
                        pltpu.VMEM((n_chunks, MXU_COLS // LANES, 2 * (bm + 2 * SUBLANES), LANES), F32),
                        pltpu.VMEM((bm, w2.shape[0]), BF16)],
        compiler_params=_params(2),
        name="convmlp_" + kind,
    )(x, x, x, mod.stack, norm_g.stack, w1.stack, ck.stack, cb.stack, w2.stack, final_g.stack, *extra_args)
    return outs[0], [_Layer(o[None], 0) for o in outs[1:]]


def _outproj_kernel(*refs, gated):
    if not gated:
        a_ref, x_ref, mod_ref, w_ref, o_ref = refs
        acc = _dot(a_ref[...], w_ref[...])
    else:
        y_ref, gt_ref, x_ref, mod_ref, w_ref, o_ref = refs

        def gate(hd):
            cols = slice(hd * RET_V_DIM, (hd + 1) * RET_V_DIM)
            y = y_ref[:, cols].astype(F32)
            ms = jnp.mean(y * y, axis=-1, keepdims=True)
            return (_silu(gt_ref[:, cols].astype(F32)) * (y * lax.rsqrt(ms + NORM_EPS))).astype(BF16)

        a_next = gate(0)
        acc = None
        for hd in range(RET_HEADS):
            a = a_next
            if hd + 1 < RET_HEADS:
                a_next = gate(hd + 1)
            part = _dot(a, w_ref[hd * RET_V_DIM:(hd + 1) * RET_V_DIM, :])
            acc = part if acc is None else acc + part
    o_ref[...] = x_ref[...] + mod_ref[2] * acc


def _outproj(acts, x, mod, w, *, ctx):
    b, l, d = x.shape
    bm = min(l, TILE_ROWS if len(acts) == 1 else TILE_ROWS // 2)
    act_specs = [pl.BlockSpec((None, bm, a.shape[-1]), lambda i, t: (i, t, 0)) for a in acts]
    return pl.pallas_call(
        functools.partial(_outproj_kernel, gated=len(acts) == 2),
        grid=(b, l // bm),
        in_specs=act_specs + [
            pl.BlockSpec((None, bm, d), lambda i, t: (i, t, 0)),
            _mod_spec(mod, ctx),
            _resident(w),
        ],
        out_specs=pl.BlockSpec((None, bm, d), lambda i, t: (i, t, 0)),
        out_shape=jax.ShapeDtypeStruct((b, l, d), F32),
        input_output_aliases={len(acts): 0},
        compiler_params=_params(2),
        name="outproj",
    )(*acts, x, mod.stack, w.stack)


def _head_meansq(v, bd_ref):
    return _dot((v * v).astype(BF16), bd_ref[...])


def _rope64(v, cos, s_up, s_dn):
    return v * cos + pltpu.roll(v, LANES - 16, axis=1) * s_up + pltpu.roll(v, 16, axis=1) * s_dn


def _qkv_kernel(x_ref, mod_ref, g_ref, w_ref, qg_ref, kg_ref, bd_ref, cos_ref, sup_ref, sdn_ref,
                q_ref, k_ref, v_ref, *, rope):
    h = _modulate(x_ref[...], g_ref[...], mod_ref[0], mod_ref[1]).astype(BF16)
    nq = N_Q_HEADS * HEAD_DIM
    nkv = N_KV_HEADS * HEAD_DIM
    n_norm = (nq + nkv) // MXU_COLS
    q_gain = qg_ref[...] * (HEAD_DIM ** -0.5 * LOG2_E)
    k_gain = kg_ref[...]
    heads_per_slab = LANES // HEAD_DIM

    def project(ci):
        return _dot(h, w_ref[:, ci * MXU_COLS:(ci + 1) * MXU_COLS])

    def finish(ci, u, ms):
        u = u * lax.rsqrt(ms + NORM_EPS)
        for j in range(MXU_COLS // LANES):
            s = u[:, j * LANES:(j + 1) * LANES] * (q_gain if ci < n_norm - 1 else k_gain)
            if rope:
                s = _rope64(s, cos_ref[...], sup_ref[...], sdn_ref[...])
            if ci < n_norm - 1:
                lo = ci * MXU_COLS + j * LANES
                q_ref[:, lo:lo + LANES] = s.astype(BF16)
            else:
                for e in range(heads_per_slab):
                    k_ref[j * heads_per_slab + e] = s[:, e * HEAD_DIM:(e + 1) * HEAD_DIM].astype(BF16)

    us = {0: project(0), 1: project(1)}
    mss = {0: _head_meansq(us[0], bd_ref)}
    for ci in range(n_norm):
        if ci + 2 <= n_norm:
            us[ci + 2] = project(ci + 2)
        if ci + 1 < n_norm:
            mss[ci + 1] = _head_meansq(us[ci + 1], bd_ref)
        finish(ci, us.pop(ci), mss.pop(ci))
    vv = us.pop(n_norm)
    ones = jnp.ones((vv.shape[0], HEAD_DIM), F32)
    for e in range(N_KV_HEADS):
        v_ref[e] = jnp.concatenate([vv[:, e * HEAD_DIM:(e + 1) * HEAD_DIM], ones], axis=-1).astype(BF16)


def _qkv(x, mod, norm_g, w, qg, kg, bd, cos, s_up, s_dn, *, ctx):
    b, l, d = x.shape
    bm = min(l, TILE_ROWS // 2)
    tab = pl.BlockSpec((bm, LANES), lambda i, t: (t, 0))
    kern = functools.partial(_qkv_kernel, rope=not ctx)
    return pl.pallas_call(
        kern,
        grid=(b, l // bm),
        in_specs=[
            pl.BlockSpec((None, bm, d), lambda i, t: (i, t, 0)),
            _mod_spec(mod, ctx),
            _resident(norm_g),
            _resident(w),
            _resident(qg),
            _resident(kg),
            _resident(bd),
            tab, tab, tab,
        ],
        out_specs=[
            pl.BlockSpec((None, bm, d), lambda i, t: (i, t, 0)),
            pl.BlockSpec((None, N_KV_HEADS, bm, HEAD_DIM), lambda i, t: (i, 0, t, 0)),
            pl.BlockSpec((None, N_KV_HEADS, bm, 2 * HEAD_DIM), lambda i, t: (i, 0, t, 0)),
        ],
        out_shape=[
            jax.ShapeDtypeStruct((b, l, d), BF16),
            jax.ShapeDtypeStruct((b, N_KV_HEADS, l, HEAD_DIM), BF16),
            jax.ShapeDtypeStruct((b, N_KV_HEADS, l, 2 * HEAD_DIM), BF16),
        ],
        compiler_params=_params(2),
        name="qkv",
    )(x, mod.stack, norm_g.stack, w.stack, qg.stack, kg.stack, bd.stack, cos, s_up, s_dn)


def _attn_kernel(*refs, n_sets):
    q_ref = refs[0]
    kv = refs[1:1 + 2 * n_sets]
    o_ref = refs[1 + 2 * n_sets]
    q = q_ref[...]
    ks = [kv[2 * i][...] for i in range(n_sets)]
    vs = [kv[2 * i + 1][...] for i in range(n_sets)]
    outs = []

    def scores(gi):
        qg = q[:, gi * HEAD_DIM:(gi + 1) * HEAD_DIM]
        return [_dot_nt(qg, k) for k in ks]

    ss_next = scores(0)
    for gi in range(GQA_GROUP):
        ss = ss_next
        if gi + 1 < GQA_GROUP:
            ss_next = scores(gi + 1)
        m = ss[0].max(axis=-1, keepdims=True)
        for s in ss[1:]:
            m = jnp.maximum(m, s.max(axis=-1, keepdims=True))
        acc = None
        for s, v in zip(ss, vs):
            pv = _dot(jnp.exp2(s - m).astype(BF16), v)
            acc = pv if acc is None else acc + pv
        o = acc * (1.0 / pltpu.roll(acc, HEAD_DIM, axis=1))
        outs.append(o[:, 0:HEAD_DIM])
    o_ref[...] = jnp.concatenate(outs, axis=-1).astype(BF16)


def _attention(q, kvs):
    b, l, d = q.shape
    bq = min(l, TILE_ROWS)
    width = GQA_GROUP * HEAD_DIM
    in_specs = [pl.BlockSpec((None, bq, width), lambda i, h, t: (i, t, h))]
    args = [q]
    for k, v in kvs:
        lk = k.shape[2]
        in_specs += [pl.BlockSpec((None, None, lk, HEAD_DIM), lambda i, h, t: (i, h, 0, 0)),
                     pl.BlockSpec((None, None, lk, 2 * HEAD_DIM), lambda i, h, t: (i, h, 0, 0))]
        args += [k, v]
    return pl.pallas_call(
        functools.partial(_attn_kernel, n_sets=len(kvs)),
        grid=(b, N_KV_HEADS, l // bq),
        in_specs=in_specs,
        out_specs=pl.BlockSpec((None, bq, width), lambda i, h, t: (i, t, h)),
        out_shape=jax.ShapeDtypeStruct((b, l, d), BF16),
        compiler_params=_params(3),
        name="attention",
    )(*args)


def _rope256(v, cos, sgn_sin):
    return v * cos + pltpu.roll(v, LANES // 2, axis=1) * sgn_sin


def _retproj_kernel(x_ref, mod_ref, g_ref, w_ref, cos_ref, sin_ref, *out_refs, ctx):
    h = _modulate(x_ref[...], g_ref[...], mod_ref[0], mod_ref[1]).astype(BF16)
    nqk = RET_HEADS * RET_HEAD_DIM
    nv = RET_HEADS * RET_V_DIM
    kscale = RET_HEAD_DIM ** -0.5
    col = nqk
    if ctx:
        k_ref, v_ref = out_refs
    else:
        q_ref, k_ref, v_ref, gt_ref = out_refs
        for ci in range(nqk // MXU_COLS):
            u = _dot(h, w_ref[:, ci * MXU_COLS:(ci + 1) * MXU_COLS])
            for j in range(MXU_COLS // LANES):
                sl = slice(j * LANES, (j + 1) * LANES)
                lo = ci * MXU_COLS + j * LANES
                q_ref[:, lo:lo + LANES] = _rope256(u[:, sl], cos_ref[:, sl], sin_ref[:, sl]).astype(BF16)
    for ci in range(nqk // MXU_COLS):
        u = _dot(h, w_ref[:, col + ci * MXU_COLS:col + (ci + 1) * MXU_COLS]) * kscale
        for j in range(MXU_COLS // LANES):
            sl = slice(j * LANES, (j + 1) * LANES)
            lo = ci * MXU_COLS + j * LANES
            s = u[:, sl]
            if not ctx:
                s = _rope256(s, cos_ref[:, sl], sin_ref[:, sl])
            k_ref[:, lo:lo + LANES] = s.astype(BF16)
    col += nqk
    for ci in range(nv // MXU_COLS):
        sl = slice(ci * MXU_COLS, (ci + 1) * MXU_COLS)
        v_ref[:, sl] = _dot(h, w_ref[:, col + ci * MXU_COLS:col + (ci + 1) * MXU_COLS]).astype(BF16)
    if not ctx:
        col += nv
        for ci in range(nv // MXU_COLS):
            sl = slice(ci * MXU_COLS, (ci + 1) * MXU_COLS)
            gt_ref[:, sl] = _dot(h, w_ref[:, col + ci * MXU_COLS:col + (ci + 1) * MXU_COLS]).astype(BF16)


def _retproj(x, mod, norm_g, w, cos, sin, *, ctx):
    b, l, d = x.shape
    bm = min(l, TILE_ROWS)
    nqk = RET_HEADS * RET_HEAD_DIM
    nv = RET_HEADS * RET_V_DIM
    row = lambda n: pl.BlockSpec((None, bm, n), lambda i, t: (i, t, 0))
    tab = pl.BlockSpec((bm, RET_HEAD_DIM), lambda i, t: (t, 0))
    widths = [nqk, nv] if ctx else [nqk, nqk, nv, nv]
    return pl.pallas_call(
        functools.partial(_retproj_kernel, ctx=ctx),
        grid=(b, l // bm),
        in_specs=[row(d), _mod_spec(mod, ctx), _resident(norm_g), _resident(w), tab, tab],
        out_specs=[row(n) for n in widths],
        out_shape=[jax.ShapeDtypeStruct((b, l, n), BF16) for n in widths],
        compiler_params=_params(2),
        name="retproj",
    )(x, mod.stack, norm_g.stack, w.stack, cos, sin)


def _retention_kernel(dec_ref, q_ref, k_ref, v_ref, kc_ref, vc_ref, o_ref, y_ref, *, seq, ctx_len):
    c = RET_BLOCK
    n_chunks = seq // c
    ii = lax.broadcasted_iota(jnp.int32, (c, c), 0)
    jj = lax.broadcasted_iota(jnp.int32, (c, c), 1)
    diff = (ii - jj).astype(F32)
    pos = lax.broadcasted_iota(jnp.int32, (c, 1), 0).astype(F32)
    posc = lax.broadcasted_iota(jnp.int32, (ctx_len, 1), 0).astype(F32)

    def outer(kd, v):
        return _dot(kd.T.astype(BF16), v)

    def rows(i):
        return slice(i * c, (i + 1) * c)

    class Head:
        def __init__(self, hh):
            hd = pl.program_id(1) * RET_HEADS_PER_STEP + hh

            def log_gamma(idx):
                dv = jnp.full((1, 1), dec_ref[idx], F32)
                return jnp.log1p(-jnp.exp2(-dv))

            lf = log_gamma(hd)
            lb = log_gamma(RET_HEADS + hd)
            self.hh = hh
            self.qk = slice(hh * RET_HEAD_DIM, (hh + 1) * RET_HEAD_DIM)
            self.vv = slice(hh * RET_V_DIM, (hh + 1) * RET_V_DIM)
            self.decay = jnp.where(diff >= 0, jnp.exp(diff * lf), jnp.exp(-diff * lb))
            self.qd_f = jnp.exp((pos + 1.0) * lf)
            self.kd_f = jnp.exp((c - 1.0 - pos) * lf)
            self.qd_b = jnp.exp((c - pos) * lb)
            self.kd_b = jnp.exp(pos * lb)
            self.cd_f = jnp.exp(c * lf)
            self.cd_b = jnp.exp(c * lb)
            kcf = kc_ref[:, self.qk].astype(F32)
            vc = vc_ref[:, self.vv]
            self.sf = outer(kcf * jnp.exp((ctx_len - 1.0 - posc) * lf), vc)
            self.sb = outer(kcf * jnp.exp(posc * lb), vc)

        def q(self, i):
            return q_ref[rows(i), self.qk]

        def k(self, i):
            return k_ref[rows(i), self.qk]

        def v(self, i):
            return v_ref[rows(i), self.vv]

        def intra(self, i):
            a = (_dot_nt(self.q(i), self.k(i)) * self.decay).astype(BF16)
            return _dot(a, self.v(i))

    heads = [Head(hh) for hh in range(RET_HEADS_PER_STEP)]
    for step in range(n_chunks):
        i = step
        j = n_chunks - 1 - step
        for h in heads:
            inter_f = _dot(h.q(i), h.sf.astype(BF16)) * h.qd_f
            inter_b = _dot(h.q(j), h.sb.astype(BF16)) * h.qd_b
            if i < j:
                y_ref[h.hh, rows(i), :] = h.intra(i) + inter_f
                y_ref[h.hh, rows(j), :] = h.intra(j) + inter_b
            else:
                o_ref[rows(i), h.vv] = (y_ref[h.hh, rows(i), :] + inter_f).astype(BF16)
                o_ref[rows(j), h.vv] = (y_ref[h.hh, rows(j), :] + inter_b).astype(BF16)
            if step < n_chunks - 1:
                h.sf = h.cd_f * h.sf + outer(h.k(i).astype(F32) * h.kd_f, h.v(i))
                h.sb = h.cd_b * h.sb + outer(h.k(j).astype(F32) * h.kd_b, h.v(j))


def _retention(dec, q, k, v, kc, vc):
    b, l, _ = q.shape
    lc = kc.shape[1]
    hs = RET_HEADS_PER_STEP
    qk = lambda n: pl.BlockSpec((None, n, hs * RET_HEAD_DIM), lambda i, h: (i, 0, h))
    vv = lambda n: pl.BlockSpec((None, n, hs * RET_V_DIM), lambda i, h: (i, 0, h))
    return pl.pallas_call(
        functools.partial(_retention_kernel, seq=l, ctx_len=lc),
        grid=(b, RET_HEADS // hs),
        in_specs=[pl.BlockSpec(memory_space=pltpu.SMEM),
                  qk(l), qk(l), vv(l), qk(lc), vv(lc)],
        out_specs=vv(l),
        out_shape=jax.ShapeDtypeStruct((b, l, RET_HEADS * RET_V_DIM), BF16),
        scratch_shapes=[pltpu.VMEM((hs, l, RET_V_DIM), F32)],
        compiler_params=_params(2),
        name="retention",
    )(dec, q, k, v, kc, vc)


def _rope_tables(seq, head_dim):
    quarter = head_dim // 4
    rows = jnp.repeat(jnp.arange(seq // GRID_W, dtype=F32), GRID_W)
    cols = jnp.tile(jnp.arange(GRID_W, dtype=F32), seq // GRID_W)
    inv = ROPE_THETA ** (-jnp.arange(quarter, dtype=F32) / quarter)
    ang = jnp.stack([rows[:, None] * inv, cols[:, None] * inv], axis=1)
    cos = jnp.cos(ang)
    sin = jnp.sin(ang)
    cos_t = jnp.stack([cos, cos], axis=2).reshape(seq, head_dim)
    sin_first = jnp.stack([-sin, jnp.zeros_like(sin)], axis=2).reshape(seq, head_dim)
    sin_second = jnp.stack([jnp.zeros_like(sin), sin], axis=2).reshape(seq, head_dim)
    return cos_t, sin_first, sin_second


def kernel(x, c, ctx, c_ctx, ada_w, ada_b, norm_mix_g, norm_ffn_g, final_norm_g, conv_w_in, conv_k, conv_w_out, attn_w_qkv, attn_q_norm_g, attn_k_norm_g, attn_w_out, ret_w_in, ret_decay, ret_w_out, ffn_w_up, ffn_conv_k, ffn_conv_b, ffn_w_down):
    batch, seq, d = x.shape
    assert d == D_MODEL and seq % (2 * RET_BLOCK) == 0 and ctx.shape[1] % SUBLANES == 0

    cvec = jnp.zeros((MOD_ROWS, d), F32).at[:batch].set(c).at[CTX_ROW].set(c_ctx)
    mod_all = _ada(cvec, ada_w, ada_b).reshape(DEPTH, MOD_ROWS, 6, 1, d)

    a_cos, a_first, a_second = _rope_tables(seq, HEAD_DIM)
    rep = LANES // HEAD_DIM
    a_cos, a_first, a_second = (jnp.tile(t, (1, rep)) for t in (a_cos, a_first, a_second))
    r_cos, r_first, r_second = _rope_tables(seq, RET_HEAD_DIM)
    r_sin = r_first + r_second
    head_sum = jnp.kron(jnp.eye(MXU_COLS // HEAD_DIM, dtype=F32),
                        jnp.full((HEAD_DIM, HEAD_DIM), 1.0 / HEAD_DIM, F32)).astype(BF16)[None]

    rows = lambda v: v.reshape(v.shape[0], 1, -1)
    norm_mix, norm_ffn = rows(norm_mix_g), rows(norm_ffn_g)
    final_g = _Layer(final_norm_g.reshape(1, 1, d), 0)
    no_bias = _Layer(jnp.zeros((1, 1, d), F32), 0)
    att_qg = rows(jnp.tile(attn_q_norm_g, (1, rep)))
    att_kg = rows(jnp.tile(attn_k_norm_g, (1, rep)))
    ffn_cb = rows(ffn_conv_b)

    kinds = [i % N_MIXERS for i in range(DEPTH)]
    reads_ctx = [kd in (1, 2) for kd in kinds]

    def f32_weights(i):
        j = i // N_MIXERS
        mixer = [(conv_w_in, conv_w_out), (attn_w_qkv, attn_w_out), (ret_w_in, ret_w_out)][kinds[i]]
        return [_Layer(w, j) for w in mixer], [_Layer(ffn_w_up, i), _Layer(ffn_w_down, i)]

    cast_here = lambda w: _Layer(w.stack[w.index:w.index + 1].astype(BF16), 0)
    mixer_w = [cast_here(w) for w in f32_weights(0)[0]]
    ffn_w = None
    cx = ctx
    for i in range(DEPTH):
        kind = kinds[i]
        j = i // N_MIXERS
        ctx_out = any(reads_ctx[i + 1:])
        mod = _Layer(mod_all, i)
        ng = _Layer(norm_mix, i)
        last = i == DEPTH - 1
        if kind == 0:
            mix = functools.partial(_convmlp, mod=mod, norm_g=ng, w1=mixer_w[0], ck=_Layer(conv_k, j),
                                    cb=no_bias, w2=mixer_w[1], final_g=final_g, kind="mix")
            x, cast_out = mix(x, ctx=False, cast=f32_weights(i)[1] if ffn_w is None else ())
            ffn_w = ffn_w or cast_out
            if ctx_out:
                cx, _ = mix(cx, ctx=True)
        elif kind == 1:
            proj = functools.partial(_qkv, mod=mod, norm_g=ng, w=mixer_w[0], qg=_Layer(att_qg, j),
                                     kg=_Layer(att_kg, j), bd=_Layer(head_sum, 0),
                                     cos=a_cos, s_up=a_first, s_dn=a_second)
            q, k, v = proj(x, ctx=False)
            qc, kc, vc = proj(cx, ctx=True)
            o = _attention(q, [(k, v), (kc, vc)])
            x = _outproj([o], x, mod, mixer_w[1], ctx=False)
            if ctx_out:
                oc = _attention(qc, [(kc, vc)])
                cx = _outproj([oc], cx, mod, mixer_w[1], ctx=True)
        else:
            q, k, v, gt = _retproj(x, mod, ng, mixer_w[0], r_cos, r_sin, ctx=False)
            kc, vc = _retproj(cx, mod, ng, mixer_w[0], r_cos, r_sin, ctx=True)
            y = _retention(ret_decay[j].reshape(-1), q, k, v, kc, vc)
            x = _outproj([y, gt], x, mod, mixer_w[1], ctx=False)
            assert not ctx_out
        if ffn_w is None:
            ffn_w = [cast_here(w) for w in f32_weights(i)[1]]
        ffn = functools.partial(_convmlp, mod=mod, norm_g=_Layer(norm_ffn, i), w1=ffn_w[0],
                                ck=_Layer(ffn_conv_k, i), cb=_Layer(ffn_cb, i), w2=ffn_w[1],
                                final_g=final_g, kind="ffn")
        nxt = [] if last else [w for pair in f32_weights(i + 1) for w in pair]
        x, cast_out = ffn(x, ctx=False, final_norm=last, cast=nxt)
        if ctx_out:
            cx, _ = ffn(cx, ctx=True)
        mixer_w, ffn_w = cast_out[:2], cast_out[2:] or None
    return x
```

```python
import functools
from typing import NamedTuple

import jax
import jax.numpy as jnp
from jax import lax
from jax.experimental import pallas as pl
from jax.experimental.pallas import tpu as pltpu

F32 = jnp.float32
BF16 = jnp.bfloat16

D_MODEL = 1024
DEPTH = 4
GRID_W = 64
N_MIXERS = 3
HEAD_DIM = 64
N_Q_HEADS = D_MODEL // HEAD_DIM
N_KV_HEADS = N_Q_HEADS // 4
GQA_GROUP = N_Q_HEADS // N_KV_HEADS
ROPE_THETA = 10000.0
RET_HEAD_DIM = 256
RET_HEADS = D_MODEL // RET_HEAD_DIM
RET_V_DIM = 2 * RET_HEAD_DIM
RET_BLOCK = 256
RET_HEADS_PER_STEP = 2
D_FF = ((8 * D_MODEL // 3 + 127) // 128) * 128
NORM_EPS = 1e-6
LOG2_E = 1.4426950408889634

LANES = 128
SUBLANES = 8
BF16_ROWS = 16
MXU_COLS = 256
VMEM_LIMIT = 56 * 1024 * 1024

FFN_TILE_ROWS = 512
TILE_ROWS = 1024
MOD_ROWS = 16
CTX_ROW = 8


def _params(n_axes):
    return pltpu.CompilerParams(dimension_semantics=("arbitrary",) * n_axes,
                                vmem_limit_bytes=VMEM_LIMIT)


class _Layer(NamedTuple):
    stack: jax.Array
    index: int

    @property
    def shape(self):
        return self.stack.shape[1:]


def _resident(p):
    zeros = (0,) * len(p.shape)
    return pl.BlockSpec((None,) + p.shape, lambda *_: (p.index,) + zeros,
                        pipeline_mode=pl.Buffered(1))


def _silu(v):
    return v * (1.0 / (1.0 + jnp.exp(-v)))


def _modulate(xf, g, shift, scale):
    ms = jnp.mean(xf * xf, axis=-1, keepdims=True)
    y = xf * lax.rsqrt(ms + NORM_EPS) * g
    return y * (1.0 + scale) + shift


def _dot(a, b):
    return jnp.dot(a, b, preferred_element_type=F32)


def _dot_nt(a, b):
    return lax.dot_general(a, b, (((1,), (1,)), ((), ())), preferred_element_type=F32)


def _ada_kernel(c_ref, w_ref, b_ref, o_ref):
    s = _silu(c_ref[...]).astype(BF16)
    o_ref[...] = _dot(s, w_ref[...].astype(BF16)) + b_ref[...]


def _ada(cvec, ada_w, ada_b):
    depth, d, n = ada_w.shape
    tn = 1536
    return pl.pallas_call(
        _ada_kernel,
        grid=(depth, n // tn),
        in_specs=[
            pl.BlockSpec((MOD_ROWS, d), lambda l, j: (0, 0)),
            pl.BlockSpec((None, d, tn), lambda l, j: (l, 0, j)),
            pl.BlockSpec((None, 1, tn), lambda l, j: (l, 0, j)),
        ],
        out_specs=pl.BlockSpec((None, MOD_ROWS, tn), lambda l, j: (l, 0, j)),
        out_shape=jax.ShapeDtypeStruct((depth, MOD_ROWS, n), F32),
        compiler_params=_params(2),
        name="ada",
    )(cvec, ada_w, ada_b.reshape(depth, 1, n))


def _mod_spec(mod, ctx):
    blk = (None, None, 6, 1, D_MODEL)
    if ctx:
        return pl.BlockSpec(blk, lambda b, t: (mod.index, CTX_ROW, 0, 0, 0))
    return pl.BlockSpec(blk, lambda b, t: (mod.index, b, 0, 0, 0))


def _convmlp_kernel(x_ref, xp_ref, xn_ref, mod_ref, g_ref, w1_ref, ck_ref, cb_ref, w2_ref, fg_ref,
                    *rest, kind, bm, mod_base, final_norm):
    n_cast = (len(rest) - 4) // 2
    o_ref = rest[n_cast]
    h_ref, u_ref, z_ref = rest[2 * n_cast + 1:]
    for src, dst in zip(rest[:n_cast], rest[n_cast + 1:2 * n_cast + 1]):
        dst[...] = src[...].astype(BF16)
    t = pl.program_id(1)
    nt = pl.num_programs(1)
    shift = mod_ref[mod_base]
    scale = mod_ref[mod_base + 1]
    gate = mod_ref[mod_base + 2]
    g = g_ref[...]
    x = x_ref[...]
    h_ref[0:bm, :] = _modulate(x, g, shift, scale).astype(BF16)
    keep_n = (t < nt - 1).astype(F32)
    keep_p = (t > 0).astype(F32)
    halo = jnp.concatenate([_modulate(xn_ref[...], g, shift, scale) * keep_n,
                            _modulate(xp_ref[...], g, shift, scale) * keep_p], axis=0)
    h_ref[bm:bm + BF16_ROWS, :] = halo.astype(BF16)
    h = h_ref[...]
    n_mid = D_FF if kind == "ffn" else D_MODEL
    n_chunks = n_mid // MXU_COLS
    top = SUBLANES

    def rows_of(s, pos, n):
        return pl.ds(s + 2 * pos, n, stride=2)

    def put(slot, s, u):
        for j in range(MXU_COLS // LANES):
            uj = u[:, j * LANES:(j + 1) * LANES]
            u_ref[slot, j, rows_of(s, 0, top), :] = uj[bm + top:bm + 2 * top]
            u_ref[slot, j, rows_of(s, top, bm), :] = uj[0:bm]
            u_ref[slot, j, rows_of(s, top + bm, top), :] = uj[bm:bm + top]

    def conv3(slot, j, s, c0):
        k0 = ck_ref[0:1, c0:c0 + LANES]
        k1 = ck_ref[1:2, c0:c0 + LANES]
        k2 = ck_ref[2:3, c0:c0 + LANES]
        return (u_ref[slot, j, rows_of(s, top - 1, bm), :] * k0
                + u_ref[slot, j, rows_of(s, top, bm), :] * k1
                + u_ref[slot, j, rows_of(s, top + 1, bm), :] * k2)

    def up(ci):
        slot = ci
        cols = lambda s: slice(s * n_mid + ci * MXU_COLS, s * n_mid + (ci + 1) * MXU_COLS)
        if kind == "ffn":
            put(slot, 0, _dot(h, w1_ref[:, cols(0)]))
            put(slot, 1, _dot(h, w1_ref[:, cols(1)]))
        else:
            put(slot, 0, _dot(h, w1_ref[:, cols(0)]))
            put(slot, 1, _dot(h, w1_ref[:, cols(1)]) * _dot(h, w1_ref[:, cols(2)]))

    def mid(ci):
        slot = ci
        for j in range(MXU_COLS // LANES):
            c0 = ci * MXU_COLS + j * LANES
            if kind == "ffn":
                cv = conv3(slot, j, 0, c0) + cb_ref[:, c0:c0 + LANES]
                cg = conv3(slot, j, 1, D_FF + c0) + cb_ref[:, D_FF + c0:D_FF + c0 + LANES]
                z = _silu(cg) * cv
            else:
                z = u_ref[slot, j, rows_of(0, top, bm), :] * conv3(slot, j, 1, c0)
            z_ref[:, c0:c0 + LANES] = z.astype(BF16)

    up(0)
    for ci in range(n_chunks):
        if ci + 1 < n_chunks:
            up(ci + 1)
        mid(ci)
    out = x + gate * _dot(z_ref[...], w2_ref[...])
    if final_norm:
        ms = jnp.mean(out * out, axis=-1, keepdims=True)
        out = out * lax.rsqrt(ms + NORM_EPS) * fg_ref[...]
    o_ref[...] = out


def _convmlp(x, mod, norm_g, w1, ck, cb, w2, final_g, *, kind, ctx, final_norm=False, cast=()):
    b, l, d = x.shape
    bm = min(l, FFN_TILE_ROWS if kind == "ffn" else TILE_ROWS)
    n_chunks = w2.shape[0] // MXU_COLS
    nt = l // bm
    hb = bm // SUBLANES
    last_hb = l // SUBLANES - 1
    kern = functools.partial(_convmlp_kernel, kind=kind, bm=bm,
                             mod_base=3 if kind == "ffn" else 0, final_norm=final_norm)
    extra_in, extra_out, extra_shapes, extra_args = [], [], [], []
    steps = b * nt
    for w in cast:
        rows_w, cols_w = w.shape
        per = next(p for p in (1, 2, 4, 8) if (rows_w * p) % (steps * BF16_ROWS) == 0)
        blk = rows_w * per // steps
        extra_in.append(pl.BlockSpec((None, blk, cols_w),
                                     lambda i, t, w=w, per=per: (w.index, (i * nt + t) // per, 0)))
        extra_out.append(pl.BlockSpec((blk, cols_w), lambda i, t, per=per: ((i * nt + t) // per, 0)))
        extra_shapes.append(jax.ShapeDtypeStruct((rows_w, cols_w), BF16))
        extra_args.append(w.stack)
    x_spec = pl.BlockSpec((None, bm, d), lambda i, t: (i, t, 0))
    outs = pl.pallas_call(
        kern,
        grid=(b, nt),
        in_specs=[
            x_spec,
            pl.BlockSpec((None, SUBLANES, d), lambda i, t: (i, jnp.maximum(t * hb - 1, 0), 0)),
            pl.BlockSpec((None, SUBLANES, d), lambda i, t: (i, jnp.minimum((t + 1) * hb, last_hb), 0)),
            _mod_spec(mod, ctx),
            _resident(norm_g),
            _resident(w1),
            _resident(ck),
            _resident(cb),
            _resident(w2),
            _resident(final_g),
        ] + extra_in,
        out_specs=[x_spec] + extra_out,
        out_shape=[jax.ShapeDtypeStruct((b, l, d), F32)] + extra_shapes,
        scratch_shapes=[pltpu.VMEM((bm + BF16_ROWS, d), BF16),
                        pltpu.VMEM((n_chunks, MXU_COLS // LANES, 2 * (bm + 2 * SUBLANES), LANES), F32),
                        pltpu.VMEM((bm, w2.shape[0]), BF16)],
        compiler_params=_params(2),
        name="convmlp_" + kind,
    )(x, x, x, mod.stack, norm_g.stack, w1.stack, ck.stack, cb.stack, w2.stack, final_g.stack, *extra_args)
    return outs[0], [_Layer(o[None], 0) for o in outs[1:]]


def _outproj_kernel(*refs, gated):
    if not gated:
        a_ref, x_ref, mod_ref, w_ref, o_ref = refs
        acc = _dot(a_ref[...], w_ref[...])
    else:
        y_ref, gt_ref, x_ref, mod_ref, w_ref, o_ref = refs

        def gate(hd):
            cols = slice(hd * RET_V_DIM, (hd + 1) * RET_V_DIM)
            y = y_ref[:, cols].astype(F32)
            ms = jnp.mean(y * y, axis=-1, keepdims=True)
            return (_silu(gt_ref[:, cols].astype(F32)) * (y * lax.rsqrt(ms + NORM_EPS))).astype(BF16)

        a_next = gate(0)
        acc = None
        for hd in range(RET_HEADS):
            a = a_next
            if hd + 1 < RET_HEADS:
                a_next = gate(hd + 1)
            part = _dot(a, w_ref[hd * RET_V_DIM:(hd + 1) * RET_V_DIM, :])
            acc = part if acc is None else acc + part
    o_ref[...] = x_ref[...] + mod_ref[2] * acc


def _outproj(acts, x, mod, w, *, ctx):
    b, l, d = x.shape
    bm = min(l, TILE_ROWS if len(acts) == 1 else TILE_ROWS // 2)
    act_specs = [pl.BlockSpec((None, bm, a.shape[-1]), lambda i, t: (i, t, 0)) for a in acts]
    return pl.pallas_call(
        functools.partial(_outproj_kernel, gated=len(acts) == 2),
        grid=(b, l // bm),
        in_specs=act_specs + [
            pl.BlockSpec((None, bm, d), lambda i, t: (i, t, 0)),
            _mod_spec(mod, ctx),
            _resident(w),
        ],
        out_specs=pl.BlockSpec((None, bm, d), lambda i, t: (i, t, 0)),
        out_shape=jax.ShapeDtypeStruct((b, l, d), F32),
        input_output_aliases={len(acts): 0},
        compiler_params=_params(2),
        name="outproj",
    )(*acts, x, mod.stack, w.stack)


def _head_meansq(v, bd_ref):
    return _dot((v * v).astype(BF16), bd_ref[...])


def _rope64(v, cos, s_up, s_dn):
    return v * cos + pltpu.roll(v, LANES - 16, axis=1) * s_up + pltpu.roll(v, 16, axis=1) * s_dn


def _qkv_kernel(x_ref, mod_ref, g_ref, w_ref, qg_ref, kg_ref, bd_ref, cos_ref, sup_ref, sdn_ref,
                q_ref, k_ref, v_ref, *, rope):
    h = _modulate(x_ref[...], g_ref[...], mod_ref[0], mod_ref[1]).astype(BF16)
    nq = N_Q_HEADS * HEAD_DIM
    nkv = N_KV_HEADS * HEAD_DIM
    n_norm = (nq + nkv) // MXU_COLS
    q_gain = qg_ref[...] * (HEAD_DIM ** -0.5 * LOG2_E)
    k_gain = kg_ref[...]
    heads_per_slab = LANES // HEAD_DIM

    def project(ci):
        return _dot(h, w_ref[:, ci * MXU_COLS:(ci + 1) * MXU_COLS])

    def finish(ci, u, ms):
        u = u * lax.rsqrt(ms + NORM_EPS)
        for j in range(MXU_COLS // LANES):
            s = u[:, j * LANES:(j + 1) * LANES] * (q_gain if ci < n_norm - 1 else k_gain)
            if rope:
                s = _rope64(s, cos_ref[...], sup_ref[...], sdn_ref[...])
            if ci < n_norm - 1:
                lo = ci * MXU_COLS + j * LANES
                q_ref[:, lo:lo + LANES] = s.astype(BF16)
            else:
                for e in range(heads_per_slab):
                    k_ref[j * heads_per_slab + e] = s[:, e * HEAD_DIM:(e + 1) * HEAD_DIM].astype(BF16)

    us = {0: project(0), 1: project(1)}
    mss = {0: _head_meansq(us[0], bd_ref)}
    for ci in range(n_norm):
        if ci + 2 <= n_norm:
            us[ci + 2] = project(ci + 2)
        if ci + 1 < n_norm:
            mss[ci + 1] = _head_meansq(us[ci + 1], bd_ref)
        finish(ci, us.pop(ci), mss.pop(ci))
    vv = us.pop(n_norm)
    ones = jnp.ones((vv.shape[0], HEAD_DIM), F32)
    for e in range(N_KV_HEADS):
        v_ref[e] = jnp.concatenate([vv[:, e * HEAD_DIM:(e + 1) * HEAD_DIM], ones], axis=-1).astype(BF16)


def _qkv(x, mod, norm_g, w, qg, kg, bd, cos, s_up, s_dn, *, ctx):
    b, l, d = x.shape
    bm = min(l, TILE_ROWS // 2)
    tab = pl.BlockSpec((bm, LANES), lambda i, t: (t, 0))
    kern = functools.partial(_qkv_kernel, rope=not ctx)
    return pl.pallas_call(
        kern,
        grid=(b, l // bm),
        in_specs=[
            pl.BlockSpec((None, bm, d), lambda i, t: (i, t, 0)),
            _mod_spec(mod, ctx),
            _resident(norm_g),
            _resident(w),
            _resident(qg),
            _resident(kg),
            _resident(bd),
            tab, tab, tab,
        ],
        out_specs=[
            pl.BlockSpec((None, bm, d), lambda i, t: (i, t, 0)),
            pl.BlockSpec((None, N_KV_HEADS, bm, HEAD_DIM), lambda i, t: (i, 0, t, 0)),
            pl.BlockSpec((None, N_KV_HEADS, bm, 2 * HEAD_DIM), lambda i, t: (i, 0, t, 0)),
        ],
        out_shape=[
            jax.ShapeDtypeStruct((b, l, d), BF16),
            jax.ShapeDtypeStruct((b, N_KV_HEADS, l, HEAD_DIM), BF16),
            jax.ShapeDtypeStruct((b, N_KV_HEADS, l, 2 * HEAD_DIM), BF16),
        ],
        compiler_params=_params(2),
        name="qkv",
    )(x, mod.stack, norm_g.stack, w.stack, qg.stack, kg.stack, bd.stack, cos, s_up, s_dn)


def _attend(q, ks, vs):
    outs = []

    def scores(gi):
        qg = q[:, gi * HEAD_DIM:(gi + 1) * HEAD_DIM]
        return [_dot_nt(qg, k) for k in ks]

    ss_next = scores(0)
    for gi in range(GQA_GROUP):
        ss = ss_next
        if gi + 1 < GQA_GROUP:
            ss_next = scores(gi + 1)
        m = ss[0].max(axis=-1, keepdims=True)
        for s in ss[1:]:
            m = jnp.maximum(m, s.max(axis=-1, keepdims=True))
        acc = None
        for s, v in zip(ss, vs):
            pv = _dot(jnp.exp2(s - m).astype(BF16), v)
            acc = pv if acc is None else acc + pv
        o = acc * (1.0 / pltpu.roll(acc, HEAD_DIM, axis=1))
        outs.append(o[:, 0:HEAD_DIM])
    return jnp.concatenate(outs, axis=-1).astype(BF16)


def _attn_kernel(*refs, n_sets):
    q_ref = refs[0]
    kv = refs[1:1 + 2 * n_sets]
    o_ref = refs[1 + 2 * n_sets]
    o_ref[...] = _attend(q_ref[...], [kv[2 * i][...] for i in range(n_sets)],
                         [kv[2 * i + 1][...] for i in range(n_sets)])


def _ctx_attn_kernel(x_ref, mod_ref, g_ref, w_ref, qg_ref, kg_ref, bd_ref, wo_ref,
                     o_ref, k_ref, v_ref, q_scr):
    _qkv_kernel(x_ref, mod_ref, g_ref, w_ref, qg_ref, kg_ref, bd_ref, None, None, None,
                q_scr, k_ref, v_ref, rope=False)
    width = GQA_GROUP * HEAD_DIM
    a = jnp.concatenate([_attend(q_scr[:, hd * width:(hd + 1) * width], [k_ref[hd]], [v_ref[hd]])
                         for hd in range(N_KV_HEADS)], axis=-1)
    o_ref[...] = x_ref[...] + mod_ref[2] * _dot(a, wo_ref[...])


def _ctx_attention(cx, mod, norm_g, w, qg, kg, bd, wo):
    b, l, d = cx.shape
    ms = _mod_spec(mod, True)
    kv_shape = lambda n: jax.ShapeDtypeStruct((b, N_KV_HEADS, l, n), BF16)
    kv_spec = lambda n: pl.BlockSpec((None, N_KV_HEADS, l, n), lambda i: (i, 0, 0, 0))
    rows = pl.BlockSpec((None, l, d), lambda i: (i, 0, 0))
    return pl.pallas_call(
        _ctx_attn_kernel,
        grid=(b,),
        in_specs=[rows, pl.BlockSpec(ms.block_shape, lambda i: ms.index_map(i, 0)),
                  _resident(norm_g), _resident(w), _resident(qg), _resident(kg), _resident(bd),
                  _resident(wo)],
        out_specs=[rows, kv_spec(HEAD_DIM), kv_spec(2 * HEAD_DIM)],
        out_shape=[jax.ShapeDtypeStruct((b, l, d), F32), kv_shape(HEAD_DIM), kv_shape(2 * HEAD_DIM)],
        scratch_shapes=[pltpu.VMEM((l, d), BF16)],
        compiler_params=_params(1),
        name="ctx_attention",
    )(cx, mod.stack, norm_g.stack, w.stack, qg.stack, kg.stack, bd.stack, wo.stack)


def _attention(q, kvs):
    b, l, d = q.shape
    bq = min(l, TILE_ROWS)
    width = GQA_GROUP * HEAD_DIM
    in_specs = [pl.BlockSpec((None, bq, width), lambda i, h, t: (i, t, h))]
    args = [q]
    for k, v in kvs:
        lk = k.shape[2]
        in_specs += [pl.BlockSpec((None, None, lk, HEAD_DIM), lambda i, h, t: (i, h, 0, 0)),
                     pl.BlockSpec((None, None, lk, 2 * HEAD_DIM), lambda i, h, t: (i, h, 0, 0))]
        args += [k, v]
    return pl.pallas_call(
        functools.partial(_attn_kernel, n_sets=len(kvs)),
        grid=(b, N_KV_HEADS, l // bq),
        in_specs=in_specs,
        out_specs=pl.BlockSpec((None, bq, width), lambda i, h, t: (i, t, h)),
        out_shape=jax.ShapeDtypeStruct((b, l, d), BF16),
        compiler_params=_params(3),
        name="attention",
    )(*args)


def _rope256(v, cos, sgn_sin):
    return v * cos + pltpu.roll(v, LANES // 2, axis=1) * sgn_sin


def _retproj_kernel(x_ref, mod_ref, g_ref, w_ref, cos_ref, sin_ref, *out_refs, ctx):
    h = _modulate(x_ref[...], g_ref[...], mod_ref[0], mod_ref[1]).astype(BF16)
    nqk = RET_HEADS * RET_HEAD_DIM
    nv = RET_HEADS * RET_V_DIM
    kscale = RET_HEAD_DIM ** -0.5
    col = nqk
    if ctx:
        k_ref, v_ref = out_refs
    else:
        q_ref, k_ref, v_ref, gt_ref = out_refs
        for ci in range(nqk // MXU_COLS):
            u = _dot(h, w_ref[:, ci * MXU_COLS:(ci + 1) * MXU_COLS])
            for j in range(MXU_COLS // LANES):
                sl = slice(j * LANES, (j + 1) * LANES)
                lo = ci * MXU_COLS + j * LANES
                q_ref[:, lo:lo + LANES] = _rope256(u[:, sl], cos_ref[:, sl], sin_ref[:, sl]).astype(BF16)
    for ci in range(nqk // MXU_COLS):
        u = _dot(h, w_ref[:, col + ci * MXU_COLS:col + (ci + 1) * MXU_COLS]) * kscale
        for j in range(MXU_COLS // LANES):
            sl = slice(j * LANES, (j + 1) * LANES)
            lo = ci * MXU_COLS + j * LANES
            s = u[:, sl]
            if not ctx:
                s = _rope256(s, cos_ref[:, sl], sin_ref[:, sl])
            k_ref[:, lo:lo + LANES] = s.astype(BF16)
    col += nqk
    for ci in range(nv // MXU_COLS):
        sl = slice(ci * MXU_COLS, (ci + 1) * MXU_COLS)
        v_ref[:, sl] = _dot(h, w_ref[:, col + ci * MXU_COLS:col + (ci + 1) * MXU_COLS]).astype(BF16)
    if not ctx:
        col += nv
        for ci in range(nv // MXU_COLS):
            sl = slice(ci * MXU_COLS, (ci + 1) * MXU_COLS)
            gt_ref[:, sl] = _dot(h, w_ref[:, col + ci * MXU_COLS:col + (ci + 1) * MXU_COLS]).astype(BF16)


def _retproj(x, mod, norm_g, w, cos, sin, *, ctx):
    b, l, d = x.shape
    bm = min(l, TILE_ROWS)
    nqk = RET_HEADS * RET_HEAD_DIM
    nv = RET_HEADS * RET_V_DIM
    row = lambda n: pl.BlockSpec((None, bm, n), lambda i, t: (i, t, 0))
    tab = pl.BlockSpec((bm, RET_HEAD_DIM), lambda i, t: (t, 0))
    widths = [nqk, nv] if ctx else [nqk, nqk, nv, nv]
    return pl.pallas_call(
        functools.partial(_retproj_kernel, ctx=ctx),
        grid=(b, l // bm),
        in_specs=[row(d), _mod_spec(mod, ctx), _resident(norm_g), _resident(w), tab, tab],
        out_specs=[row(n) for n in widths],
        out_shape=[jax.ShapeDtypeStruct((b, l, n), BF16) for n in widths],
        compiler_params=_params(2),
        name="retproj",
    )(x, mod.stack, norm_g.stack, w.stack, cos, sin)


def _retention_kernel(dec_ref, q_ref, k_ref, v_ref, kc_ref, vc_ref, o_ref, y_ref, *, seq, ctx_len):
    c = RET_BLOCK
    n_chunks = seq // c
    ii = lax.broadcasted_iota(jnp.int32, (c, c), 0)
    jj = lax.broadcasted_iota(jnp.int32, (c, c), 1)
    diff = (ii - jj).astype(F32)
    pos = lax.broadcasted_iota(jnp.int32, (c, 1), 0).astype(F32)
    posc = lax.broadcasted_iota(jnp.int32, (ctx_len, 1), 0).astype(F32)

    def outer(kd, v):
        return _dot(kd.T.astype(BF16), v)

    def rows(i):
        return slice(i * c, (i + 1) * c)

    class Head:
        def __init__(self, hh):
            hd = pl.program_id(1) * RET_HEADS_PER_STEP + hh

            def log_gamma(idx):
                dv = jnp.full((1, 1), dec_ref[idx], F32)
                return jnp.log1p(-jnp.exp2(-dv))

            lf = log_gamma(hd)
            lb = log_gamma(RET_HEADS + hd)
            self.hh = hh
            self.qk = slice(hh * RET_HEAD_DIM, (hh + 1) * RET_HEAD_DIM)
            self.vv = slice(hh * RET_V_DIM, (hh + 1) * RET_V_DIM)
            self.decay = jnp.where(diff >= 0, jnp.exp(diff * lf), jnp.exp(-diff * lb))
            self.qd_f = jnp.exp((pos + 1.0) * lf)
            self.kd_f = jnp.exp((c - 1.0 - pos) * lf)
            self.qd_b = jnp.exp((c - pos) * lb)
            self.kd_b = jnp.exp(pos * lb)
            self.cd_f = jnp.exp(c * lf)
            self.cd_b = jnp.exp(c * lb)
            kcf = kc_ref[:, self.qk].astype(F32)
            vc = vc_ref[:, self.vv]
            self.sf = outer(kcf * jnp.exp((ctx_len - 1.0 - posc) * lf), vc)
            self.sb = outer(kcf * jnp.exp(posc * lb), vc)

        def q(self, i):
            return q_ref[rows(i), self.qk]

        def k(self, i):
            return k_ref[rows(i), self.qk]

        def v(self, i):
            return v_ref[rows(i), self.vv]

        def intra(self, i):
            a = (_dot_nt(self.q(i), self.k(i)) * self.decay).astype(BF16)
            return _dot(a, self.v(i))

    heads = [Head(hh) for hh in range(RET_HEADS_PER_STEP)]
    for step in range(n_chunks):
        i = step
        j = n_chunks - 1 - step
        for h in heads:
            inter_f = _dot(h.q(i), h.sf.astype(BF16)) * h.qd_f
            inter_b = _dot(h.q(j), h.sb.astype(BF16)) * h.qd_b
            if i < j:
                y_ref[h.hh, rows(i), :] = h.intra(i) + inter_f
                y_ref[h.hh, rows(j), :] = h.intra(j) + inter_b
            else:
                o_ref[rows(i), h.vv] = (y_ref[h.hh, rows(i), :] + inter_f).astype(BF16)
                o_ref[rows(j), h.vv] = (y_ref[h.hh, rows(j), :] + inter_b).astype(BF16)
            if step < n_chunks - 1:
                h.sf = h.cd_f * h.sf + outer(h.k(i).astype(F32) * h.kd_f, h.v(i))
                h.sb = h.cd_b * h.sb + outer(h.k(j).astype(F32) * h.kd_b, h.v(j))


def _retention(dec, q, k, v, kc, vc):
    b, l, _ = q.shape
    lc = kc.shape[1]
    hs = RET_HEADS_PER_STEP
    qk = lambda n: pl.BlockSpec((None, n, hs * RET_HEAD_DIM), lambda i, h: (i, 0, h))
    vv = lambda n: pl.BlockSpec((None, n, hs * RET_V_DIM), lambda i, h: (i, 0, h))
    return pl.pallas_call(
        functools.partial(_retention_kernel, seq=l, ctx_len=lc),
        grid=(b, RET_HEADS // hs),
        in_specs=[pl.BlockSpec(memory_space=pltpu.SMEM),
                  qk(l), qk(l), vv(l), qk(lc), vv(lc)],
        out_specs=vv(l),
        out_shape=jax.ShapeDtypeStruct((b, l, RET_HEADS * RET_V_DIM), BF16),
        scratch_shapes=[pltpu.VMEM((hs, l, RET_V_DIM), F32)],
        compiler_params=_params(2),
        name="retention",
    )(dec, q, k, v, kc, vc)


def _rope_tables(seq, head_dim):
    quarter = head_dim // 4
    rows = jnp.repeat(jnp.arange(seq // GRID_W, dtype=F32), GRID_W)
    cols = jnp.tile(jnp.arange(GRID_W, dtype=F32), seq // GRID_W)
    inv = ROPE_THETA ** (-jnp.arange(quarter, dtype=F32) / quarter)
    ang = jnp.stack([rows[:, None] * inv, cols[:, None] * inv], axis=1)
    cos = jnp.cos(ang)
    sin = jnp.sin(ang)
    cos_t = jnp.stack([cos, cos], axis=2).reshape(seq, head_dim)
    sin_first = jnp.stack([-sin, jnp.zeros_like(sin)], axis=2).reshape(seq, head_dim)
    sin_second = jnp.stack([jnp.zeros_like(sin), sin], axis=2).reshape(seq, head_dim)
    return cos_t, sin_first, sin_second


def kernel(x, c, ctx, c_ctx, ada_w, ada_b, norm_mix_g, norm_ffn_g, final_norm_g, conv_w_in, conv_k, conv_w_out, attn_w_qkv, attn_q_norm_g, attn_k_norm_g, attn_w_out, ret_w_in, ret_decay, ret_w_out, ffn_w_up, ffn_conv_k, ffn_conv_b, ffn_w_down):
    batch, seq, d = x.shape
    assert d == D_MODEL and seq % (2 * RET_BLOCK) == 0 and ctx.shape[1] % SUBLANES == 0

    cvec = jnp.zeros((MOD_ROWS, d), F32).at[:batch].set(c).at[CTX_ROW].set(c_ctx)
    mod_all = _ada(cvec, ada_w, ada_b).reshape(DEPTH, MOD_ROWS, 6, 1, d)

    a_cos, a_first, a_second = _rope_tables(seq, HEAD_DIM)
    rep = LANES // HEAD_DIM
    a_cos, a_first, a_second = (jnp.tile(t, (1, rep)) for t in (a_cos, a_first, a_second))
    r_cos, r_first, r_second = _rope_tables(seq, RET_HEAD_DIM)
    r_sin = r_first + r_second
    head_sum = jnp.kron(jnp.eye(MXU_COLS // HEAD_DIM, dtype=F32),
                        jnp.full((HEAD_DIM, HEAD_DIM), 1.0 / HEAD_DIM, F32)).astype(BF16)[None]

    rows = lambda v: v.reshape(v.shape[0], 1, -1)
    norm_mix, norm_ffn = rows(norm_mix_g), rows(norm_ffn_g)
    final_g = _Layer(final_norm_g.reshape(1, 1, d), 0)
    no_bias = _Layer(jnp.zeros((1, 1, d), F32), 0)
    att_qg = rows(jnp.tile(attn_q_norm_g, (1, rep)))
    att_kg = rows(jnp.tile(attn_k_norm_g, (1, rep)))
    ffn_cb = rows(ffn_conv_b)

    kinds = [i % N_MIXERS for i in range(DEPTH)]
    reads_ctx = [kd in (1, 2) for kd in kinds]

    def f32_weights(i):
        j = i // N_MIXERS
        mixer = [(conv_w_in, conv_w_out), (attn_w_qkv, attn_w_out), (ret_w_in, ret_w_out)][kinds[i]]
        return [_Layer(w, j) for w in mixer], [_Layer(ffn_w_up, i), _Layer(ffn_w_down, i)]

    cast_here = lambda w: _Layer(w.stack[w.index:w.index + 1].astype(BF16), 0)
    mixer_w = [cast_here(w) for w in f32_weights(0)[0]]
    ffn_w = None
    cx = ctx
    for i in range(DEPTH):
        kind = kinds[i]
        j = i // N_MIXERS
        ctx_out = any(reads_ctx[i + 1:])
        mod = _Layer(mod_all, i)
        ng = _Layer(norm_mix, i)
        last = i == DEPTH - 1
        if kind == 0:
            mix = functools.partial(_convmlp, mod=mod, norm_g=ng, w1=mixer_w[0], ck=_Layer(conv_k, j),
                                    cb=no_bias, w2=mixer_w[1], final_g=final_g, kind="mix")
            x, cast_out = mix(x, ctx=False, cast=f32_weights(i)[1] if ffn_w is None else ())
            ffn_w = ffn_w or cast_out
            if ctx_out:
                cx, _ = mix(cx, ctx=True)
        elif kind == 1:
            proj = functools.partial(_qkv, mod=mod, norm_g=ng, w=mixer_w[0], qg=_Layer(att_qg, j),
                                     kg=_Layer(att_kg, j), bd=_Layer(head_sum, 0),
                                     cos=a_cos, s_up=a_first, s_dn=a_second)
            q, k, v = proj(x, ctx=False)
            if ctx_out:
                cx_next, kc, vc = _ctx_attention(cx, mod, ng, mixer_w[0], _Layer(att_qg, j), _Layer(att_kg, j),
                                                 _Layer(head_sum, 0), mixer_w[1])
            else:
                (_, kc, vc), cx_next = proj(cx, ctx=True), cx
            o = _attention(q, [(k, v), (kc, vc)])
            x = _outproj([o], x, mod, mixer_w[1], ctx=False)
            cx = cx_next
        else:
            q, k, v, gt = _retproj(x, mod, ng, mixer_w[0], r_cos, r_sin, ctx=False)
            kc, vc = _retproj(cx, mod, ng, mixer_w[0], r_cos, r_sin, ctx=True)
            y = _retention(ret_decay[j].reshape(-1), q, k, v, kc, vc)
            x = _outproj([y, gt], x, mod, mixer_w[1], ctx=False)
            assert not ctx_out
        if ffn_w is None:
            ffn_w = [cast_here(w) for w in f32_weights(i)[1]]
        ffn = functools.partial(_convmlp, mod=mod, norm_g=_Layer(norm_ffn, i), w1=ffn_w[0],
                                ck=_Layer(ffn_conv_k, i), cb=_Layer(ffn_cb, i), w2=ffn_w[1],
                                final_g=final_g, kind="ffn")
        nxt = [] if last else [w for pair in f32_weights(i + 1) for w in pair]
        x, cast_out = ffn(x, ctx=False, final_norm=last, cast=nxt)
        if ctx_out:
            cx, _ = ffn(cx, ctx=True)
        mixer_w, ffn_w = cast_out[:2], cast_out[2:] or None
    return x
```

```python
import functools
from typing import NamedTuple

import jax
import jax.numpy as jnp
from jax import lax
from jax.experimental import pallas as pl
from jax.experimental.pallas import tpu as pltpu

F32 = jnp.float32
BF16 = jnp.bfloat16

D_MODEL = 1024
DEPTH = 4
GRID_W = 64
N_MIXERS = 3
HEAD_DIM = 64
N_Q_HEADS = D_MODEL // HEAD_DIM
N_KV_HEADS = N_Q_HEADS // 4
GQA_GROUP = N_Q_HEADS // N_KV_HEADS
ROPE_THETA = 10000.0
RET_HEAD_DIM = 256
RET_HEADS = D_MODEL // RET_HEAD_DIM
RET_V_DIM = 2 * RET_HEAD_DIM
RET_BLOCK = 256
RET_HEADS_PER_STEP = 2
D_FF = ((8 * D_MODEL // 3 + 127) // 128) * 128
NORM_EPS = 1e-6
LOG2_E = 1.4426950408889634

LANES = 128
SUBLANES = 8
BF16_ROWS = 16
MXU_COLS = 256
VMEM_LIMIT = 56 * 1024 * 1024

FFN_TILE_ROWS = 512
TILE_ROWS = 1024
MOD_ROWS = 16
CTX_ROW = 8


def _params(n_axes):
    return pltpu.CompilerParams(dimension_semantics=("arbitrary",) * n_axes,
                                vmem_limit_bytes=VMEM_LIMIT)


class _Layer(NamedTuple):
    stack: jax.Array
    index: int

    @property
    def shape(self):
        return self.stack.shape[1:]


def _resident(p):
    zeros = (0,) * len(p.shape)
    return pl.BlockSpec((None,) + p.shape, lambda *_: (p.index,) + zeros,
                        pipeline_mode=pl.Buffered(1))


def _silu(v):
    return v * (1.0 / (1.0 + jnp.exp(-v)))


def _modulate(xf, g, shift, scale):
    ms = jnp.mean(xf * xf, axis=-1, keepdims=True)
    y = xf * lax.rsqrt(ms + NORM_EPS) * g
    return y * (1.0 + scale) + shift


def _dot(a, b):
    return jnp.dot(a, b, preferred_element_type=F32)


def _dot_nt(a, b):
    return lax.dot_general(a, b, (((1,), (1,)), ((), ())), preferred_element_type=F32)


def _ada_kernel(c_ref, w_ref, b_ref, *rest):
    n_cast = (len(rest) - 1) // 2
    s = _silu(c_ref[...]).astype(BF16)
    rest[n_cast][...] = _dot(s, w_ref[...].astype(BF16)) + b_ref[...]
    for src, dst in zip(rest[:n_cast], rest[n_cast + 1:]):
        dst[...] = src[...].astype(BF16)


def _ada(cvec, ada_w, ada_b, cast=()):
    depth, d, n = ada_w.shape
    tn = 1536
    nj = n // tn
    steps = depth * nj
    extra_in, extra_out, extra_shapes = [], [], []
    for w in cast:
        rows_w, cols_w = w.shape
        blk = rows_w // steps
        assert blk % BF16_ROWS == 0 and blk * steps == rows_w
        extra_in.append(pl.BlockSpec((None, blk, cols_w), lambda l, j, w=w: (w.index, l * nj + j, 0)))
        extra_out.append(pl.BlockSpec((blk, cols_w), lambda l, j: (l * nj + j, 0)))
        extra_shapes.append(jax.ShapeDtypeStruct((rows_w, cols_w), BF16))
    outs = pl.pallas_call(
        _ada_kernel,
        grid=(depth, nj),
        in_specs=[
            pl.BlockSpec((MOD_ROWS, d), lambda l, j: (0, 0)),
            pl.BlockSpec((None, d, tn), lambda l, j: (l, 0, j)),
            pl.BlockSpec((None, 1, tn), lambda l, j: (l, 0, j)),
        ] + extra_in,
        out_specs=[pl.BlockSpec((None, MOD_ROWS, tn), lambda l, j: (l, 0, j))] + extra_out,
        out_shape=[jax.ShapeDtypeStruct((depth, MOD_ROWS, n), F32)] + extra_shapes,
        compiler_params=_params(2),
        name="ada",
    )(cvec, ada_w, ada_b.reshape(depth, 1, n), *[w.stack for w in cast])
    return outs[0], [_Layer(o[None], 0) for o in outs[1:]]


def _mod_spec(mod, ctx):
    blk = (None, None, 6, 1, D_MODEL)
    if ctx:
        return pl.BlockSpec(blk, lambda b, t: (mod.index, CTX_ROW, 0, 0, 0))
    return pl.BlockSpec(blk, lambda b, t: (mod.index, b, 0, 0, 0))


def _convmlp_kernel(x_ref, xp_ref, xn_ref, mod_ref, g_ref, w1_ref, ck_ref, cb_ref, w2_ref, fg_ref,
                    *rest, kind, bm, mod_base, final_norm):
    n_cast = (len(rest) - 4) // 2
    o_ref = rest[n_cast]
    h_ref, u_ref, z_ref = rest[2 * n_cast + 1:]
    for src, dst in zip(rest[:n_cast], rest[n_cast + 1:2 * n_cast + 1]):
        dst[...] = src[...].astype(BF16)
    t = pl.program_id(1)
    nt = pl.num_programs(1)
    shift = mod_ref[mod_base]
    scale = mod_ref[mod_base + 1]
    gate = mod_ref[mod_base + 2]
    g = g_ref[...]
    x = x_ref[...]
    h_ref[0:bm, :] = _modulate(x, g, shift, scale).astype(BF16)
    keep_n = (t < nt - 1).astype(F32)
    keep_p = (t > 0).astype(F32)
    halo = jnp.concatenate([_modulate(xn_ref[...], g, shift, scale) * keep_n,
                            _modulate(xp_ref[...], g, shift, scale) * keep_p], axis=0)
    h_ref[bm:bm + BF16_ROWS, :] = halo.astype(BF16)
    h = h_ref[...]
    n_mid = D_FF if kind == "ffn" else D_MODEL
    n_chunks = n_mid // MXU_COLS
    top = SUBLANES

    def rows_of(s, pos, n):
        return pl.ds(s + 2 * pos, n, stride=2)

    def put(slot, s, u):
        for j in range(MXU_COLS // LANES):
            uj = u[:, j * LANES:(j + 1) * LANES]
            u_ref[slot, j, rows_of(s, 0, top), :] = uj[bm + top:bm + 2 * top]
            u_ref[slot, j, rows_of(s, top, bm), :] = uj[0:bm]
            u_ref[slot, j, rows_of(s, top + bm, top), :] = uj[bm:bm + top]

    def conv3(slot, j, s, c0):
        k0 = ck_ref[0:1, c0:c0 + LANES]
        k1 = ck_ref[1:2, c0:c0 + LANES]
        k2 = ck_ref[2:3, c0:c0 + LANES]
        return (u_ref[slot, j, rows_of(s, top - 1, bm), :] * k0
                + u_ref[slot, j, rows_of(s, top, bm), :] * k1
                + u_ref[slot, j, rows_of(s, top + 1, bm), :] * k2)

    def up(ci):
        slot = ci
        cols = lambda s: slice(s * n_mid + ci * MXU_COLS, s * n_mid + (ci + 1) * MXU_COLS)
        if kind == "ffn":
            put(slot, 0, _dot(h, w1_ref[:, cols(0)]))
            put(slot, 1, _dot(h, w1_ref[:, cols(1)]))
        else:
            put(slot, 0, _dot(h, w1_ref[:, cols(0)]))
            put(slot, 1, _dot(h, w1_ref[:, cols(1)]) * _dot(h, w1_ref[:, cols(2)]))

    def mid(ci):
        slot = ci
        for j in range(MXU_COLS // LANES):
            c0 = ci * MXU_COLS + j * LANES
            if kind == "ffn":
                cv = conv3(slot, j, 0, c0) + cb_ref[:, c0:c0 + LANES]
                cg = conv3(slot, j, 1, D_FF + c0) + cb_ref[:, D_FF + c0:D_FF + c0 + LANES]
                z = _silu(cg) * cv
            else:
                z = u_ref[slot, j, rows_of(0, top, bm), :] * conv3(slot, j, 1, c0)
            z_ref[:, c0:c0 + LANES] = z.astype(BF16)

    up(0)
    for ci in range(n_chunks):
        if ci + 1 < n_chunks:
            up(ci + 1)
        mid(ci)
    out = x + gate * _dot(z_ref[...], w2_ref[...])
    if final_norm:
        ms = jnp.mean(out * out, axis=-1, keepdims=True)
        out = out * lax.rsqrt(ms + NORM_EPS) * fg_ref[...]
    o_ref[...] = out


def _convmlp(x, mod, norm_g, w1, ck, cb, w2, final_g, *, kind, ctx, final_norm=False, cast=()):
    b, l, d = x.shape
    bm = min(l, FFN_TILE_ROWS if kind == "ffn" else TILE_ROWS)
    n_chunks = w2.shape[0] // MXU_COLS
    nt = l // bm
    hb = bm // SUBLANES
    last_hb = l // SUBLANES - 1
    kern = functools.partial(_convmlp_kernel, kind=kind, bm=bm,
                             mod_base=3 if kind == "ffn" else 0, final_norm=final_norm)
    extra_in, extra_out, extra_shapes, extra_args = [], [], [], []
    steps = b * nt
    for w in cast:
        rows_w, cols_w = w.shape
        per = next(p for p in (1, 2, 4, 8) if (rows_w * p) % (steps * BF16_ROWS) == 0)
        blk = rows_w * per // steps
        extra_in.append(pl.BlockSpec((None, blk, cols_w),
                                     lambda i, t, w=w, per=per: (w.index, (i * nt + t) // per, 0)))
        extra_out.append(pl.BlockSpec((blk, cols_w), lambda i, t, per=per: ((i * nt + t) // per, 0)))
        extra_shapes.append(jax.ShapeDtypeStruct((rows_w, cols_w), BF16))
        extra_args.append(w.stack)
    x_spec = pl.BlockSpec((None, bm, d), lambda i, t: (i, t, 0))
    outs = pl.pallas_call(
        kern,
        grid=(b, nt),
        in_specs=[
            x_spec,
            pl.BlockSpec((None, SUBLANES, d), lambda i, t: (i, jnp.maximum(t * hb - 1, 0), 0)),
            pl.BlockSpec((None, SUBLANES, d), lambda i, t: (i, jnp.minimum((t + 1) * hb, last_hb), 0)),
            _mod_spec(mod, ctx),
            _resident(norm_g),
            _resident(w1),
            _resident(ck),
            _resident(cb),
            _resident(w2),
            _resident(final_g),
        ] + extra_in,
        out_specs=[x_spec] + extra_out,
        out_shape=[jax.ShapeDtypeStruct((b, l, d), F32)] + extra_shapes,
        scratch_shapes=[pltpu.VMEM((bm + BF16_ROWS, d), BF16),
                        pltpu.VMEM((n_chunks, MXU_COLS // LANES, 2 * (bm + 2 * SUBLANES), LANES), F32),
                        pltpu.VMEM((bm, w2.shape[0]), BF16)],
        compiler_params=_params(2),
        name="convmlp_" + kind,
    )(x, x, x, mod.stack, norm_g.stack, w1.stack, ck.stack, cb.stack, w2.stack, final_g.stack, *extra_args)
    return outs[0], [_Layer(o[None], 0) for o in outs[1:]]


def _outproj_kernel(*refs, gated):
    if not gated:
        a_ref, x_ref, mod_ref, w_ref, o_ref = refs
        acc = _dot(a_ref[...], w_ref[...])
    else:
        y_ref, gt_ref, x_ref, mod_ref, w_ref, o_ref = refs

        def gate(hd):
            cols = slice(hd * RET_V_DIM, (hd + 1) * RET_V_DIM)
            y = y_ref[:, cols].astype(F32)
            ms = jnp.mean(y * y, axis=-1, keepdims=True)
            return (_silu(gt_ref[:, cols].astype(F32)) * (y * lax.rsqrt(ms + NORM_EPS))).astype(BF16)

        a_next = gate(0)
        acc = None
        for hd in range(RET_HEADS):
            a = a_next
            if hd + 1 < RET_HEADS:
                a_next = gate(hd + 1)
            part = _dot(a, w_ref[hd * RET_V_DIM:(hd + 1) * RET_V_DIM, :])
            acc = part if acc is None else acc + part
    o_ref[...] = x_ref[...] + mod_ref[2] * acc


def _outproj(acts, x, mod, w, *, ctx):
    b, l, d = x.shape
    bm = min(l, TILE_ROWS if len(acts) == 1 else TILE_ROWS // 2)
    act_specs = [pl.BlockSpec((None, bm, a.shape[-1]), lambda i, t: (i, t, 0)) for a in acts]
    return pl.pallas_call(
        functools.partial(_outproj_kernel, gated=len(acts) == 2),
        grid=(b, l // bm),
        in_specs=act_specs + [
            pl.BlockSpec((None, bm, d), lambda i, t: (i, t, 0)),
            _mod_spec(mod, ctx),
            _resident(w),
        ],
        out_specs=pl.BlockSpec((None, bm, d), lambda i, t: (i, t, 0)),
        out_shape=jax.ShapeDtypeStruct((b, l, d), F32),
        input_output_aliases={len(acts): 0},
        compiler_params=_params(2),
        name="outproj",
    )(*acts, x, mod.stack, w.stack)


def _head_meansq(v, bd_ref):
    return _dot((v * v).astype(BF16), bd_ref[...])


def _rope64(v, cos, s_up, s_dn):
    return v * cos + pltpu.roll(v, LANES - 16, axis=1) * s_up + pltpu.roll(v, 16, axis=1) * s_dn


def _qkv_kernel(x_ref, mod_ref, g_ref, w_ref, qg_ref, kg_ref, bd_ref, cos_ref, sup_ref, sdn_ref,
                q_ref, k_ref, v_ref, *, rope):
    h = _modulate(x_ref[...], g_ref[...], mod_ref[0], mod_ref[1]).astype(BF16)
    nq = N_Q_HEADS * HEAD_DIM
    nkv = N_KV_HEADS * HEAD_DIM
    n_norm = (nq + nkv) // MXU_COLS
    q_gain = qg_ref[...] * (HEAD_DIM ** -0.5 * LOG2_E)
    k_gain = kg_ref[...]
    heads_per_slab = LANES // HEAD_DIM

    def project(ci):
        return _dot(h, w_ref[:, ci * MXU_COLS:(ci + 1) * MXU_COLS])

    def finish(ci, u, ms):
        u = u * lax.rsqrt(ms + NORM_EPS)
        for j in range(MXU_COLS // LANES):
            s = u[:, j * LANES:(j + 1) * LANES] * (q_gain if ci < n_norm - 1 else k_gain)
            if rope:
                s = _rope64(s, cos_ref[...], sup_ref[...], sdn_ref[...])
            if ci < n_norm - 1:
                lo = ci * MXU_COLS + j * LANES
                q_ref[:, lo:lo + LANES] = s.astype(BF16)
            else:
                for e in range(heads_per_slab):
                    k_ref[j * heads_per_slab + e] = s[:, e * HEAD_DIM:(e + 1) * HEAD_DIM].astype(BF16)

    us = {0: project(0), 1: project(1)}
    mss = {0: _head_meansq(us[0], bd_ref)}
    for ci in range(n_norm):
        if ci + 2 <= n_norm:
            us[ci + 2] = project(ci + 2)
        if ci + 1 < n_norm:
            mss[ci + 1] = _head_meansq(us[ci + 1], bd_ref)
        finish(ci, us.pop(ci), mss.pop(ci))
    vv = us.pop(n_norm)
    ones = jnp.ones((vv.shape[0], HEAD_DIM), F32)
    for e in range(N_KV_HEADS):
        v_ref[e] = jnp.concatenate([vv[:, e * HEAD_DIM:(e + 1) * HEAD_DIM], ones], axis=-1).astype(BF16)


def _qkv(x, mod, norm_g, w, qg, kg, bd, cos, s_up, s_dn, *, ctx):
    b, l, d = x.shape
    bm = min(l, TILE_ROWS // 2)
    tab = pl.BlockSpec((bm, LANES), lambda i, t: (t, 0))
    kern = functools.partial(_qkv_kernel, rope=not ctx)
    return pl.pallas_call(
        kern,
        grid=(b, l // bm),
        in_specs=[
            pl.BlockSpec((None, bm, d), lambda i, t: (i, t, 0)),
            _mod_spec(mod, ctx),
            _resident(norm_g),
            _resident(w),
            _resident(qg),
            _resident(kg),
            _resident(bd),
            tab, tab, tab,
        ],
        out_specs=[
            pl.BlockSpec((None, bm, d), lambda i, t: (i, t, 0)),
            pl.BlockSpec((None, N_KV_HEADS, bm, HEAD_DIM), lambda i, t: (i, 0, t, 0)),
            pl.BlockSpec((None, N_KV_HEADS, bm, 2 * HEAD_DIM), lambda i, t: (i, 0, t, 0)),
        ],
        out_shape=[
            jax.ShapeDtypeStruct((b, l, d), BF16),
            jax.ShapeDtypeStruct((b, N_KV_HEADS, l, HEAD_DIM), BF16),
            jax.ShapeDtypeStruct((b, N_KV_HEADS, l, 2 * HEAD_DIM), BF16),
        ],
        compiler_params=_params(2),
        name="qkv",
    )(x, mod.stack, norm_g.stack, w.stack, qg.stack, kg.stack, bd.stack, cos, s_up, s_dn)


def _attend(q, ks, vs):
    outs = []

    def scores(gi):
        qg = q[:, gi * HEAD_DIM:(gi + 1) * HEAD_DIM]
        return [_dot_nt(qg, k) for k in ks]

    ss_next = scores(0)
    for gi in range(GQA_GROUP):
        ss = ss_next
        if gi + 1 < GQA_GROUP:
            ss_next = scores(gi + 1)
        m = ss[0].max(axis=-1, keepdims=True)
        for s in ss[1:]:
            m = jnp.maximum(m, s.max(axis=-1, keepdims=True))
        acc = None
        for s, v in zip(ss, vs):
            pv = _dot(jnp.exp2(s - m).astype(BF16), v)
            acc = pv if acc is None else acc + pv
        o = acc * (1.0 / pltpu.roll(acc, HEAD_DIM, axis=1))
        outs.append(o[:, 0:HEAD_DIM])
    return jnp.concatenate(outs, axis=-1).astype(BF16)


def _attn_kernel(*refs, n_sets):
    q_ref = refs[0]
    kv = refs[1:1 + 2 * n_sets]
    o_ref = refs[1 + 2 * n_sets]
    o_ref[...] = _attend(q_ref[...], [kv[2 * i][...] for i in range(n_sets)],
                         [kv[2 * i + 1][...] for i in range(n_sets)])


def _ctx_attn_kernel(x_ref, mod_ref, g_ref, w_ref, qg_ref, kg_ref, bd_ref, wo_ref,
                     o_ref, k_ref, v_ref, q_scr):
    _qkv_kernel(x_ref, mod_ref, g_ref, w_ref, qg_ref, kg_ref, bd_ref, None, None, None,
                q_scr, k_ref, v_ref, rope=False)
    width = GQA_GROUP * HEAD_DIM
    a = jnp.concatenate([_attend(q_scr[:, hd * width:(hd + 1) * width], [k_ref[hd]], [v_ref[hd]])
                         for hd in range(N_KV_HEADS)], axis=-1)
    o_ref[...] = x_ref[...] + mod_ref[2] * _dot(a, wo_ref[...])


def _ctx_attention(cx, mod, norm_g, w, qg, kg, bd, wo):
    b, l, d = cx.shape
    ms = _mod_spec(mod, True)
    kv_shape = lambda n: jax.ShapeDtypeStruct((b, N_KV_HEADS, l, n), BF16)
    kv_spec = lambda n: pl.BlockSpec((None, N_KV_HEADS, l, n), lambda i: (i, 0, 0, 0))
    rows = pl.BlockSpec((None, l, d), lambda i: (i, 0, 0))
    return pl.pallas_call(
        _ctx_attn_kernel,
        grid=(b,),
        in_specs=[rows, pl.BlockSpec(ms.block_shape, lambda i: ms.index_map(i, 0)),
                  _resident(norm_g), _resident(w), _resident(qg), _resident(kg), _resident(bd),
                  _resident(wo)],
        out_specs=[rows, kv_spec(HEAD_DIM), kv_spec(2 * HEAD_DIM)],
        out_shape=[jax.ShapeDtypeStruct((b, l, d), F32), kv_shape(HEAD_DIM), kv_shape(2 * HEAD_DIM)],
        scratch_shapes=[pltpu.VMEM((l, d), BF16)],
        compiler_params=_params(1),
        name="ctx_attention",
    )(cx, mod.stack, norm_g.stack, w.stack, qg.stack, kg.stack, bd.stack, wo.stack)


def _attention(q, kvs):
    b, l, d = q.shape
    bq = min(l, TILE_ROWS)
    width = GQA_GROUP * HEAD_DIM
    in_specs = [pl.BlockSpec((None, bq, width), lambda i, h, t: (i, t, h))]
    args = [q]
    for k, v in kvs:
        lk = k.shape[2]
        in_specs += [pl.BlockSpec((None, None, lk, HEAD_DIM), lambda i, h, t: (i, h, 0, 0)),
                     pl.BlockSpec((None, None, lk, 2 * HEAD_DIM), lambda i, h, t: (i, h, 0, 0))]
        args += [k, v]
    return pl.pallas_call(
        functools.partial(_attn_kernel, n_sets=len(kvs)),
        grid=(b, N_KV_HEADS, l // bq),
        in_specs=in_specs,
        out_specs=pl.BlockSpec((None, bq, width), lambda i, h, t: (i, t, h)),
        out_shape=jax.ShapeDtypeStruct((b, l, d), BF16),
        compiler_params=_params(3),
        name="attention",
    )(*args)


def _rope256(v, cos, sgn_sin):
    return v * cos + pltpu.roll(v, LANES // 2, axis=1) * sgn_sin


def _retproj_kernel(x_ref, mod_ref, g_ref, w_ref, cos_ref, sin_ref, *out_refs, ctx):
    h = _modulate(x_ref[...], g_ref[...], mod_ref[0], mod_ref[1]).astype(BF16)
    nqk = RET_HEADS * RET_HEAD_DIM
    nv = RET_HEADS * RET_V_DIM
    kscale = RET_HEAD_DIM ** -0.5
    col = nqk
    if ctx:
        k_ref, v_ref = out_refs
    else:
        q_ref, k_ref, v_ref, gt_ref = out_refs
        for ci in range(nqk // MXU_COLS):
            u = _dot(h, w_ref[:, ci * MXU_COLS:(ci + 1) * MXU_COLS])
            for j in range(MXU_COLS // LANES):
                sl = slice(j * LANES, (j + 1) * LANES)
                lo = ci * MXU_COLS + j * LANES
                q_ref[:, lo:lo + LANES] = _rope256(u[:, sl], cos_ref[:, sl], sin_ref[:, sl]).astype(BF16)
    for ci in range(nqk // MXU_COLS):
        u = _dot(h, w_ref[:, col + ci * MXU_COLS:col + (ci + 1) * MXU_COLS]) * kscale
        for j in range(MXU_COLS // LANES):
            sl = slice(j * LANES, (j + 1) * LANES)
            lo = ci * MXU_COLS + j * LANES
            s = u[:, sl]
            if not ctx:
                s = _rope256(s, cos_ref[:, sl], sin_ref[:, sl])
            k_ref[:, lo:lo + LANES] = s.astype(BF16)
    col += nqk
    for ci in range(nv // MXU_COLS):
        sl = slice(ci * MXU_COLS, (ci + 1) * MXU_COLS)
        v_ref[:, sl] = _dot(h, w_ref[:, col + ci * MXU_COLS:col + (ci + 1) * MXU_COLS]).astype(BF16)
    if not ctx:
        col += nv
        for ci in range(nv // MXU_COLS):
            sl = slice(ci * MXU_COLS, (ci + 1) * MXU_COLS)
            gt_ref[:, sl] = _dot(h, w_ref[:, col + ci * MXU_COLS:col + (ci + 1) * MXU_COLS]).astype(BF16)


def _retproj(x, mod, norm_g, w, cos, sin, *, ctx):
    b, l, d = x.shape
    bm = min(l, TILE_ROWS)
    nqk = RET_HEADS * RET_HEAD_DIM
    nv = RET_HEADS * RET_V_DIM
    row = lambda n: pl.BlockSpec((None, bm, n), lambda i, t: (i, t, 0))
    tab = pl.BlockSpec((bm, RET_HEAD_DIM), lambda i, t: (t, 0))
    widths = [nqk, nv] if ctx else [nqk, nqk, nv, nv]
    return pl.pallas_call(
        functools.partial(_retproj_kernel, ctx=ctx),
        grid=(b, l // bm),
        in_specs=[row(d), _mod_spec(mod, ctx), _resident(norm_g), _resident(w), tab, tab],
        out_specs=[row(n) for n in widths],
        out_shape=[jax.ShapeDtypeStruct((b, l, n), BF16) for n in widths],
        compiler_params=_params(2),
        name="retproj",
    )(x, mod.stack, norm_g.stack, w.stack, cos, sin)


def _retention_kernel(dec_ref, q_ref, k_ref, v_ref, kc_ref, vc_ref, o_ref, y_ref, *, seq, ctx_len):
    c = RET_BLOCK
    n_chunks = seq // c
    ii = lax.broadcasted_iota(jnp.int32, (c, c), 0)
    jj = lax.broadcasted_iota(jnp.int32, (c, c), 1)
    diff = (ii - jj).astype(F32)
    pos = lax.broadcasted_iota(jnp.int32, (c, 1), 0).astype(F32)
    posc = lax.broadcasted_iota(jnp.int32, (ctx_len, 1), 0).astype(F32)

    def outer(kd, v):
        return _dot(kd.T.astype(BF16), v)

    def rows(i):
        return slice(i * c, (i + 1) * c)

    class Head:
        def __init__(self, hh):
            hd = pl.program_id(1) * RET_HEADS_PER_STEP + hh

            def log_gamma(idx):
                dv = jnp.full((1, 1), dec_ref[idx], F32)
                return jnp.log1p(-jnp.exp2(-dv))

            lf = log_gamma(hd)
            lb = log_gamma(RET_HEADS + hd)
            self.hh = hh
            self.qk = slice(hh * RET_HEAD_DIM, (hh + 1) * RET_HEAD_DIM)
            self.vv = slice(hh * RET_V_DIM, (hh + 1) * RET_V_DIM)
            self.decay = jnp.where(diff >= 0, jnp.exp(diff * lf), jnp.exp(-diff * lb))
            self.qd_f = jnp.exp((pos + 1.0) * lf)
            self.kd_f = jnp.exp((c - 1.0 - pos) * lf)
            self.qd_b = jnp.exp((c - pos) * lb)
            self.kd_b = jnp.exp(pos * lb)
            self.cd_f = jnp.exp(c * lf)
            self.cd_b = jnp.exp(c * lb)
            kcf = kc_ref[:, self.qk].astype(F32)
            vc = vc_ref[:, self.vv]
            self.sf = outer(kcf * jnp.exp((ctx_len - 1.0 - posc) * lf), vc)
            self.sb = outer(kcf * jnp.exp(posc * lb), vc)

        def q(self, i):
            return q_ref[rows(i), self.qk]

        def k(self, i):
            return k_ref[rows(i), self.qk]

        def v(self, i):
            return v_ref[rows(i), self.vv]

        def intra(self, i):
            a = (_dot_nt(self.q(i), self.k(i)) * self.decay).astype(BF16)
            return _dot(a, self.v(i))

    heads = [Head(hh) for hh in range(RET_HEADS_PER_STEP)]
    for step in range(n_chunks):
        i = step
        j = n_chunks - 1 - step
        for h in heads:
            inter_f = _dot(h.q(i), h.sf.astype(BF16)) * h.qd_f
            inter_b = _dot(h.q(j), h.sb.astype(BF16)) * h.qd_b
            if i < j:
                y_ref[h.hh, rows(i), :] = h.intra(i) + inter_f
                y_ref[h.hh, rows(j), :] = h.intra(j) + inter_b
            else:
                o_ref[rows(i), h.vv] = (y_ref[h.hh, rows(i), :] + inter_f).astype(BF16)
                o_ref[rows(j), h.vv] = (y_ref[h.hh, rows(j), :] + inter_b).astype(BF16)
            if step < n_chunks - 1:
                h.sf = h.cd_f * h.sf + outer(h.k(i).astype(F32) * h.kd_f, h.v(i))
                h.sb = h.cd_b * h.sb + outer(h.k(j).astype(F32) * h.kd_b, h.v(j))


def _retention(dec, q, k, v, kc, vc):
    b, l, _ = q.shape
    lc = kc.shape[1]
    hs = RET_HEADS_PER_STEP
    qk = lambda n: pl.BlockSpec((None, n, hs * RET_HEAD_DIM), lambda i, h: (i, 0, h))
    vv = lambda n: pl.BlockSpec((None, n, hs * RET_V_DIM), lambda i, h: (i, 0, h))
    return pl.pallas_call(
        functools.partial(_retention_kernel, seq=l, ctx_len=lc),
        grid=(b, RET_HEADS // hs),
        in_specs=[pl.BlockSpec(memory_space=pltpu.SMEM),
                  qk(l), qk(l), vv(l), qk(lc), vv(lc)],
        out_specs=vv(l),
        out_shape=jax.ShapeDtypeStruct((b, l, RET_HEADS * RET_V_DIM), BF16),
        scratch_shapes=[pltpu.VMEM((hs, l, RET_V_DIM), F32)],
        compiler_params=_params(2),
        name="retention",
    )(dec, q, k, v, kc, vc)


def _rope_tables(seq, head_dim):
    quarter = head_dim // 4
    rows = jnp.repeat(jnp.arange(seq // GRID_W, dtype=F32), GRID_W)
    cols = jnp.tile(jnp.arange(GRID_W, dtype=F32), seq // GRID_W)
    inv = ROPE_THETA ** (-jnp.arange(quarter, dtype=F32) / quarter)
    ang = jnp.stack([rows[:, None] * inv, cols[:, None] * inv], axis=1)
    cos = jnp.cos(ang)
    sin = jnp.sin(ang)
    cos_t = jnp.stack([cos, cos], axis=2).reshape(seq, head_dim)
    sin_first = jnp.stack([-sin, jnp.zeros_like(sin)], axis=2).reshape(seq, head_dim)
    sin_second = jnp.stack([jnp.zeros_like(sin), sin], axis=2).reshape(seq, head_dim)
    return cos_t, sin_first, sin_second


def kernel(x, c, ctx, c_ctx, ada_w, ada_b, norm_mix_g, norm_ffn_g, final_norm_g, conv_w_in, conv_k, conv_w_out, attn_w_qkv, attn_q_norm_g, attn_k_norm_g, attn_w_out, ret_w_in, ret_decay, ret_w_out, ffn_w_up, ffn_conv_k, ffn_conv_b, ffn_w_down):
    batch, seq, d = x.shape
    assert d == D_MODEL and seq % (2 * RET_BLOCK) == 0 and ctx.shape[1] % SUBLANES == 0

    cvec = jnp.zeros((MOD_ROWS, d), F32).at[:batch].set(c).at[CTX_ROW].set(c_ctx)

    a_cos, a_first, a_second = _rope_tables(seq, HEAD_DIM)
    rep = LANES // HEAD_DIM
    a_cos, a_first, a_second = (jnp.tile(t, (1, rep)) for t in (a_cos, a_first, a_second))
    r_cos, r_first, r_second = _rope_tables(seq, RET_HEAD_DIM)
    r_sin = r_first + r_second
    head_sum = jnp.kron(jnp.eye(MXU_COLS // HEAD_DIM, dtype=F32),
                        jnp.full((HEAD_DIM, HEAD_DIM), 1.0 / HEAD_DIM, F32)).astype(BF16)[None]

    rows = lambda v: v.reshape(v.shape[0], 1, -1)
    norm_mix, norm_ffn = rows(norm_mix_g), rows(norm_ffn_g)
    final_g = _Layer(final_norm_g.reshape(1, 1, d), 0)
    no_bias = _Layer(jnp.zeros((1, 1, d), F32), 0)
    att_qg = rows(jnp.tile(attn_q_norm_g, (1, rep)))
    att_kg = rows(jnp.tile(attn_k_norm_g, (1, rep)))
    ffn_cb = rows(ffn_conv_b)

    kinds = [i % N_MIXERS for i in range(DEPTH)]
    reads_ctx = [kd in (1, 2) for kd in kinds]

    def f32_weights(i):
        j = i // N_MIXERS
        mixer = [(conv_w_in, conv_w_out), (attn_w_qkv, attn_w_out), (ret_w_in, ret_w_out)][kinds[i]]
        return [_Layer(w, j) for w in mixer], [_Layer(ffn_w_up, i), _Layer(ffn_w_down, i)]

    cast_here = lambda w: _Layer(w.stack[w.index:w.index + 1].astype(BF16), 0)
    mod_all, mixer_w = _ada(cvec, ada_w, ada_b, cast=f32_weights(0)[0])
    mod_all = mod_all.reshape(DEPTH, MOD_ROWS, 6, 1, d)
    ffn_w = None
    cx = ctx
    for i in range(DEPTH):
        kind = kinds[i]
        j = i // N_MIXERS
        ctx_out = any(reads_ctx[i + 1:])
        mod = _Layer(mod_all, i)
        ng = _Layer(norm_mix, i)
        last = i == DEPTH - 1
        if kind == 0:
            mix = functools.partial(_convmlp, mod=mod, norm_g=ng, w1=mixer_w[0], ck=_Layer(conv_k, j),
                                    cb=no_bias, w2=mixer_w[1], final_g=final_g, kind="mix")
            x, cast_out = mix(x, ctx=False, cast=f32_weights(i)[1] if ffn_w is None else ())
            ffn_w = ffn_w or cast_out
            if ctx_out:
                cx, _ = mix(cx, ctx=True)
        elif kind == 1:
            proj = functools.partial(_qkv, mod=mod, norm_g=ng, w=mixer_w[0], qg=_Layer(att_qg, j),
                                     kg=_Layer(att_kg, j), bd=_Layer(head_sum, 0),
                                     cos=a_cos, s_up=a_first, s_dn=a_second)
            q, k, v = proj(x, ctx=False)
            if ctx_out:
                cx_next, kc, vc = _ctx_attention(cx, mod, ng, mixer_w[0], _Layer(att_qg, j), _Layer(att_kg, j),
                                                 _Layer(head_sum, 0), mixer_w[1])
            else:
                (_, kc, vc), cx_next = proj(cx, ctx=True), cx
            o = _attention(q, [(k, v), (kc, vc)])
            x = _outproj([o], x, mod, mixer_w[1], ctx=False)
            cx = cx_next
        else:
            q, k, v, gt = _retproj(x, mod, ng, mixer_w[0], r_cos, r_sin, ctx=False)
            kc, vc = _retproj(cx, mod, ng, mixer_w[0], r_cos, r_sin, ctx=True)
            y = _retention(ret_decay[j].reshape(-1), q, k, v, kc, vc)
            x = _outproj([y, gt], x, mod, mixer_w[1], ctx=False)
            assert not ctx_out
        if ffn_w is None:
            ffn_w = [cast_here(w) for w in f32_weights(i)[1]]
        ffn = functools.partial(_convmlp, mod=mod, norm_g=_Layer(norm_ffn, i), w1=ffn_w[0],
                                ck=_Layer(ffn_conv_k, i), cb=_Layer(ffn_cb, i), w2=ffn_w[1],
                                final_g=final_g, kind="ffn")
        nxt = [] if last else [w for pair in f32_weights(i + 1) for w in pair]
        x, cast_out = ffn(x, ctx=False, final_norm=last, cast=nxt)
        if ctx_out:
            cx, _ = ffn(cx, ctx=True)
        mixer_w, ffn_w = cast_out[:2], cast_out[2:] or None
    return x
```
